```python
import math
import jax, jax.numpy as jnp
from jax import lax
import numpy as np

D_MODEL = 1024
BATCH = 32
SEQ = 2048
DEPTH = 2

CHUNK = 64
Q_BLOCK = 128
N_MIXERS = 2
HEAD_DIM = 64
SB_HEADS = D_MODEL // HEAD_DIM
DSA_HEADS = D_MODEL // HEAD_DIM
DSA_KV_HEADS = 2
DSA_GROUP = DSA_HEADS // DSA_KV_HEADS
IDX_HEADS = 8
IDX_DIM = 64
TOPK_MAX = 256
ROPE_THETA = 500000.0
ROPE_FRACTION = 4
N_EXPERTS = 32
TOP_K = 4
D_FF = D_MODEL
SWIGLU_ALPHA = 1.702
SWIGLU_LIMIT = 7.0
EPS = 1e-6
N_A = (DEPTH + 1) // 2
N_B = DEPTH // 2
DSA_SPLITS = (DSA_HEADS * HEAD_DIM, DSA_KV_HEADS * HEAD_DIM, DSA_KV_HEADS * HEAD_DIM,
              IDX_HEADS * IDX_DIM, IDX_DIM, IDX_HEADS)
DSA_PROJ = sum(DSA_SPLITS)

kernel_name = 'hybrid_stickbreak_dsa_moe_adaln'


def _rms(x, g):
    x32 = x.astype(jnp.float32)
    y = x32 * lax.rsqrt(jnp.mean(x32 * x32, axis=-1, keepdims=True) + EPS)
    return (y * g.astype(jnp.float32)).astype(x.dtype)


def _partial_rope(x, positions):
    d = x.shape[-1]
    rd = d // ROPE_FRACTION
    half = rd // 2
    inv = jnp.exp(-math.log(ROPE_THETA) * (2.0 * jnp.arange(half, dtype=jnp.float32) / rd))
    ang = positions.astype(jnp.float32)[:, :, None, None] * inv
    cos, sin = jnp.cos(ang), jnp.sin(ang)
    x32 = x.astype(jnp.float32)
    x1, x2, rest = x32[..., :half], x32[..., half:rd], x32[..., rd:]
    out = jnp.concatenate([x1 * cos - x2 * sin, x2 * cos + x1 * sin, rest], axis=-1)
    return out.astype(x.dtype)


def _stick_breaking(h, w_qkv, w_o):
    B, S, _ = h.shape
    qkv = (h @ w_qkv).reshape(B, S, 3, SB_HEADS, HEAD_DIM)
    q, k, v = qkv[:, :, 0], qkv[:, :, 1], qkv[:, :, 2]
    scale = HEAD_DIM ** -0.5
    outs = []
    for q0 in range(0, S, Q_BLOCK):
        kl = q0 + Q_BLOCK
        z = jnp.einsum('bqhd,bkhd->bhqk', q[:, q0:kl], k[:, :kl]).astype(jnp.float32) * scale
        qpos = q0 + jnp.arange(Q_BLOCK)
        kpos = jnp.arange(kl)
        before = kpos[None, :] < qpos[:, None]
        log_1m = jnp.where(before, jax.nn.log_sigmoid(-z), 0.0)
        tail = lax.cumsum(log_1m, axis=3, reverse=True) - log_1m
        a = jnp.where(before, jnp.exp(jax.nn.log_sigmoid(z) + tail), 0.0)
        outs.append(jnp.einsum('bhqk,bkhd->bqhd', a, v[:, :kl].astype(jnp.float32)))
    o = jnp.concatenate(outs, axis=1).astype(h.dtype).reshape(B, S, SB_HEADS * HEAD_DIM)
    return o @ w_o


def _dsa(h, positions, w_in, q_gain, k_gain, w_o):
    B, S, _ = h.shape
    topk = min(TOPK_MAX, S // 4)
    splits = np.cumsum(DSA_SPLITS)[:-1].tolist()
    q, k, v, qi, ki, wi = jnp.split(h @ w_in, splits, axis=-1)
    q = _partial_rope(_rms(q.reshape(B, S, DSA_HEADS, HEAD_DIM), q_gain), positions)
    k = _partial_rope(_rms(k.reshape(B, S, DSA_KV_HEADS, HEAD_DIM), k_gain), positions)
    v = v.reshape(B, S, DSA_KV_HEADS, HEAD_DIM)
    qi = _partial_rope(qi.reshape(B, S, IDX_HEADS, IDX_DIM), positions)
    ki = _partial_rope(ki.reshape(B, S, 1, IDX_DIM), positions)[:, :, 0]
    wi = wi.astype(jnp.float32) * (IDX_HEADS ** -0.5)
    q = q.reshape(B, S, DSA_KV_HEADS, DSA_GROUP, HEAD_DIM)
    kchunk = jnp.arange(S) // CHUNK
    gather = jax.vmap(lambda t, i: t[i])
    outs = []
    for q0 in range(0, S, Q_BLOCK):
        qe = q0 + Q_BLOCK
        qchunk = (q0 + jnp.arange(Q_BLOCK)) // CHUNK
        adm = kchunk[None, :] <= qchunk[:, None]
        rel = jax.nn.relu(jnp.einsum('bqhd,bsd->bqhs', qi[:, q0:qe], ki).astype(jnp.float32))
        score = jnp.einsum('bqhs,bqh->bqs', rel * (IDX_DIM ** -0.5), wi[:, q0:qe])
        score = jnp.where(adm[None], score, -jnp.inf)
        top_val, top_idx = lax.top_k(score, topk)
        valid = top_val > -jnp.inf
        kg = gather(k, top_idx)
        vg = gather(v, top_idx)
        s = jnp.einsum('bqgrd,bqkgd->bqgrk', q[:, q0:qe], kg).astype(jnp.float32) * (HEAD_DIM ** -0.5)
        s = jnp.where(valid[:, :, None, None, :], s, -jnp.inf)
        p = jax.nn.softmax(s, axis=-1)
        outs.append(jnp.einsum('bqgrk,bqkgd->bqgrd', p, vg.astype(jnp.float32)))
    o = jnp.concatenate(outs, axis=1).astype(h.dtype).reshape(B, S, DSA_HEADS * HEAD_DIM)
    return o @ w_o


def _moe(h, router_w, router_b, w_gu, b_gu, w_down, b_down):
    B, S, D = h.shape
    t = h.reshape(B * S, D)
    logits = (t @ router_w).astype(jnp.float32) + router_b.astype(jnp.float32)
    top_v, top_i = lax.top_k(logits, TOP_K)
    gates = jax.nn.softmax(top_v, axis=-1)
    comb = jnp.sum(jax.nn.one_hot(top_i, N_EXPERTS, dtype=jnp.float32) * gates[..., None], axis=1)
    out = jnp.zeros((B * S, D), jnp.float32)
    for e in range(N_EXPERTS):
        gu = t @ w_gu[e] + b_gu[e]
        g = jnp.minimum(gu[:, 0::2], SWIGLU_LIMIT)
        u = jnp.clip(gu[:, 1::2], -SWIGLU_LIMIT, SWIGLU_LIMIT)
        y = ((u + 1.0) * (g * jax.nn.sigmoid(g * SWIGLU_ALPHA))) @ w_down[e] + b_down[e]
        out = out + comb[:, e:e + 1] * y.astype(jnp.float32)
    return out.astype(h.dtype).reshape(B, S, D)


def setup_inputs(seed: int = 0) -> dict:
    key = jax.random.key(seed)
    ks = jax.random.split(key, 20)
    f32 = jnp.float32

    def nrm(k, shape, scale):
        return jax.random.normal(k, shape, f32) * scale

    x = nrm(ks[0], (BATCH, SEQ, D_MODEL), 1.0)
    c = nrm(ks[1], (BATCH, D_MODEL), 1.0)
    offs = jax.random.randint(ks[2], (BATCH, 1), 0, 64, dtype=jnp.int32) * CHUNK
    positions = (offs + jnp.arange(SEQ, dtype=jnp.int32)[None, :]).astype(jnp.int32)
    ds = D_MODEL ** -0.5
    return {
        'x': x,
        'c': c,
        'positions': positions,
        'ada_w': nrm(ks[3], (DEPTH, D_MODEL, 6 * D_MODEL), 0.5 * ds),
        'ada_b': nrm(ks[4], (DEPTH, 6 * D_MODEL), 0.01),
        'norm1_g': 1.0 + nrm(ks[5], (DEPTH, D_MODEL), 0.01),
        'norm2_g': 1.0 + nrm(ks[6], (DEPTH, D_MODEL), 0.01),
        'sb_w_qkv': nrm(ks[7], (N_A, D_MODEL, 3 * SB_HEADS * HEAD_DIM), ds),
        'sb_w_o': nrm(ks[8], (N_A, SB_HEADS * HEAD_DIM, D_MODEL), (SB_HEADS * HEAD_DIM) ** -0.5),
        'dsa_w_in': nrm(ks[9], (N_B, D_MODEL, DSA_PROJ), ds),
        'dsa_q_gain': 1.0 + nrm(ks[10], (N_B, HEAD_DIM), 0.01),
        'dsa_k_gain': 1.0 + nrm(ks[11], (N_B, HEAD_DIM), 0.01),
        'dsa_w_o': nrm(ks[12], (N_B, DSA_HEADS * HEAD_DIM, D_MODEL), (DSA_HEADS * HEAD_DIM) ** -0.5),
        'router_w': nrm(ks[13], (DEPTH, D_MODEL, N_EXPERTS), ds),
        'router_b': nrm(ks[14], (DEPTH, N_EXPERTS), 0.01),
        'exp_w_gu': nrm(ks[15], (DEPTH, N_EXPERTS, D_MODEL, 2 * D_FF), ds),
        'exp_b_gu': nrm(ks[16], (DEPTH, N_EXPERTS, 2 * D_FF), 0.01),
        'exp_w_down': nrm(ks[17], (DEPTH, N_EXPERTS, D_FF, D_MODEL), D_FF ** -0.5),
        'exp_b_down': nrm(ks[18], (DEPTH, N_EXPERTS, D_MODEL), 0.01),
    }


def reference(x, c, positions, ada_w, ada_b, norm1_g, norm2_g, sb_w_qkv, sb_w_o,
              dsa_w_in, dsa_q_gain, dsa_k_gain, dsa_w_o, router_w, router_b,
              exp_w_gu, exp_b_gu, exp_w_down, exp_b_down):
    cs = jax.nn.silu(c)
    for i in range(DEPTH):
        mod = cs @ ada_w[i] + ada_b[i]
        sh1, sc1, g1, sh2, sc2, g2 = [m[:, None, :] for m in jnp.split(mod, 6, axis=-1)]
        h = _rms(x, norm1_g[i]) * (1.0 + sc1) + sh1
        j = i // N_MIXERS
        if i % N_MIXERS == 0:
            y = _stick_breaking(h, sb_w_qkv[j], sb_w_o[j])
        else:
            y = _dsa(h, positions, dsa_w_in[j], dsa_q_gain[j], dsa_k_gain[j], dsa_w_o[j])
        x = x + g1 * y
        h = _rms(x, norm2_g[i]) * (1.0 + sc2) + sh2
        x = x + g2 * _moe(h, router_w[i], router_b[i], exp_w_gu[i], exp_b_gu[i],
                          exp_w_down[i], exp_b_down[i])
    return x
```

```python
import functools
import math

import jax
import jax.numpy as jnp
from jax import lax
from jax.experimental import pallas as pl
from jax.experimental.pallas import tpu as pltpu

HEAD_DIM = 64
DSA_KV_HEADS = 2
IDX_HEADS = 8
IDX_DIM = 64
CHUNK_SHIFT = 6
TOPK_MAX = 256
TOP_K = 4
ROPE_THETA = 500000.0
ROPE_DIMS = HEAD_DIM // 4
SWIGLU_ALPHA = 1.702
SWIGLU_LIMIT = 7.0
EPS = 1e-6

LANES = 128
QB = 128
KB = 256
NEG = -1e30
INT_MIN = -(2 ** 31)
KEY_NEG_INF = -2139095041
VMEM_LIMIT = 48 * 1024 * 1024

f32 = jnp.float32
bf16 = jnp.bfloat16


def _cparams(sem):
    return pltpu.CompilerParams(dimension_semantics=sem, vmem_limit_bytes=VMEM_LIMIT)


def _dot(a, b):
    return jnp.dot(a, b, preferred_element_type=f32)


def _dot_nt(a, b):
    return lax.dot_general(a, b, (((1,), (1,)), ((), ())), preferred_element_type=f32)


def _dot_split(x, m01, passes):
    acc = None
    r = x
    for p in range(passes):
        t = r.astype(bf16)
        d = _dot(t, m01)
        acc = d if acc is None else acc + d
        if p + 1 < passes:
            r = r - t.astype(f32)
    return acc


def _norm_mod(x, g, sc, sh):
    ms = jnp.mean(x * x, axis=-1, keepdims=True)
    return (x * lax.rsqrt(ms + EPS) * g) * (1.0 + sc) + sh


def _mod_kernel(c_ref, w_ref, b_ref, o_ref):
    c = c_ref[...]
    cs = c * (1.0 / (1.0 + jnp.exp(-c)))
    o_ref[0] = jnp.dot(cs, w_ref[0], preferred_element_type=f32,
                       precision=lax.Precision.HIGHEST) + b_ref[0]


def _modulation(c, ada_w, ada_b):
    depth, d, n6 = ada_w.shape
    b = c.shape[0]
    tn = 1024
    return pl.pallas_call(
        _mod_kernel,
        out_shape=jax.ShapeDtypeStruct((depth, b, n6), f32),
        grid=(depth, n6 // tn),
        in_specs=[pl.BlockSpec((b, d), lambda l, j: (0, 0)),
                  pl.BlockSpec((1, d, tn), lambda l, j: (l, 0, j)),
                  pl.BlockSpec((1, 1, tn), lambda l, j: (l, 0, j))],
        out_specs=pl.BlockSpec((1, b, tn), lambda l, j: (l, 0, j)),
        compiler_params=_cparams(("arbitrary", "arbitrary")),
        name="adaln_mod",
    )(c, ada_w, ada_b.reshape(depth, 1, n6))


def _proj_kernel(x_ref, g_ref, sc_ref, sh_ref, w_ref, o_ref):
    h = _norm_mod(x_ref[...], g_ref[...], sc_ref[0], sh_ref[0])
    o_ref[...] = _dot(h.astype(bf16), w_ref[...]).astype(o_ref.dtype)


def _proj(x2, g, sc, sh, w, seq, tm=512):
    n, d = x2.shape
    nc = w.shape[1]
    per = seq // tm
    return pl.pallas_call(
        _proj_kernel,
        out_shape=jax.ShapeDtypeStruct((n, nc), bf16),
        grid=(n // tm,),
        in_specs=[pl.BlockSpec((tm, d), lambda i: (i, 0)),
                  pl.BlockSpec((1, d), lambda i: (0, 0)),
                  pl.BlockSpec((1, 1, d), lambda i: (i // per, 0, 0)),
                  pl.BlockSpec((1, 1, d), lambda i: (i // per, 0, 0)),
                  pl.BlockSpec((d, nc), lambda i: (0, 0))],
        out_specs=pl.BlockSpec((tm, nc), lambda i: (i, 0)),
        compiler_params=_cparams(("arbitrary",)),
        name="sb_qkv_proj",
    )(x2, g, sc, sh, w)


def _sb_kernel(q_ref, k_ref, v_ref, o_ref, k0_ref, k1_ref):
    seq = q_ref.shape[0]
    nqb = seq // QB
    lane = lax.broadcasted_iota(jnp.int32, (1, LANES), 1)
    kk = k_ref[...]
    zero = jnp.zeros_like(kk)
    k0_ref[...] = jnp.where(lane < HEAD_DIM, kk, zero)
    k1_ref[...] = jnp.where(lane >= HEAD_DIM, kk, zero)
    row = lax.broadcasted_iota(jnp.int32, (QB, QB), 0)
    col = lax.broadcasted_iota(jnp.int32, (QB, QB), 1)
    before = col < row
    tri = jnp.where(row > col, 1.0, 0.0).astype(bf16)

    def step(q, kb, vb, c, acc, diag):
        z = _dot_nt(q, kb)
        l1p = jnp.log(1.0 + jnp.exp(-jnp.abs(z)))
        lm = jnp.minimum(-z, 0.0) - l1p
        ls = jnp.minimum(z, 0.0) - l1p
        if diag:
            lm = jnp.where(before, lm, 0.0)
        tail = _dot_split(lm, tri, 3) + c
        a = jnp.exp(ls + tail)
        if diag:
            a = jnp.where(before, a, 0.0)
        acc = acc + _dot(a.astype(bf16), vb)
        c = c + jnp.sum(lm, axis=1, keepdims=True)
        return c, acc

    def qblock(i, _):
        q0 = pl.multiple_of(i * QB, QB)
        q = q_ref[pl.ds(q0, QB), :] * 0.125
        vb = v_ref[pl.ds(q0, QB), :]
        zc = jnp.zeros((QB, 1), f32)
        za = jnp.zeros((QB, LANES), f32)
        c0, a0 = step(q, k0_ref[pl.ds(q0, QB), :], vb, zc, za, True)
        c1, a1 = step(q, k1_ref[pl.ds(q0, QB), :], vb, zc, za, True)

        def kblock(jj, carry):
            c0, a0, c1, a1 = carry
            k0 = pl.multiple_of((i - 1 - jj) * QB, QB)
            vb = v_ref[pl.ds(k0, QB), :]
            c0, a0 = step(q, k0_ref[pl.ds(k0, QB), :], vb, c0, a0, False)
            c1, a1 = step(q, k1_ref[pl.ds(k0, QB), :], vb, c1, a1, False)
            return c0, a0, c1, a1

        c0, a0, c1, a1 = lax.fori_loop(0, i, kblock, (c0, a0, c1, a1))
        o_ref[pl.ds(q0, QB), :] = jnp.where(lane < HEAD_DIM, a0, a1).astype(o_ref.dtype)
        return 0

    lax.fori_loop(0, nqb, qblock, 0)


def _stick_breaking(qkv, batch, seq, d):
    npair = d // LANES
    return pl.pallas_call(
        _sb_kernel,
        out_shape=jax.ShapeDtypeStruct((batch * seq, d), bf16),
        grid=(batch, npair),
        in_specs=[pl.BlockSpec((seq, LANES), lambda b, p: (b, p)),
                  pl.BlockSpec((seq, LANES), lambda b, p: (b, npair + p)),
                  pl.BlockSpec((seq, LANES), lambda b, p: (b, 2 * npair + p))],
        out_specs=pl.BlockSpec((seq, LANES), lambda b, p: (b, p)),
        scratch_shapes=[pltpu.VMEM((seq, LANES), bf16), pltpu.VMEM((seq, LANES), bf16)],
        compiler_params=_cparams(("arbitrary", "arbitrary")),
        name="stick_breaking_attn",
    )(qkv, qkv, qkv)


def _dsa_proj_kernel(x_ref, g_ref, sc_ref, sh_ref, w_ref, pos_ref, inv_ref, qg_ref, kg_ref,
                     bd_ref, o_ref, wi_ref, *, d):
    h = _norm_mod(x_ref[...], g_ref[...], sc_ref[0], sh_ref[0])
    p = _dot(h.astype(bf16), w_ref[...])
    lane = lax.broadcasted_iota(jnp.int32, (1, LANES), 1)
    ang = pos_ref[...].astype(f32) * inv_ref[...]
    cos_t = jnp.cos(ang)
    sin_t = jnp.sin(ang)
    upper = (lane % ROPE_DIMS) >= (ROPE_DIMS // 2)
    s_up = jnp.where(upper, sin_t, 0.0)
    s_lo = jnp.where(upper, 0.0, -sin_t)
    half = ROPE_DIMS // 2

    def rope(y):
        return y * cos_t + pltpu.roll(y, half, 1) * s_up + pltpu.roll(y, LANES - half, 1) * s_lo

    def headnorm(y, gain):
        ms = _dot_split(y * y, bd_ref[...], 2) * (1.0 / HEAD_DIM)
        return y * lax.rsqrt(ms + EPS) * gain

    nq = d // LANES
    for c in range(nq):
        y = p[:, c * LANES:(c + 1) * LANES]
        o_ref[:, c * LANES:(c + 1) * LANES] = rope(headnorm(y, qg_ref[...])).astype(bf16)
    c0 = nq
    for c in range(c0, c0 + IDX_HEADS * IDX_DIM // LANES):
        o_ref[:, c * LANES:(c + 1) * LANES] = rope(p[:, c * LANES:(c + 1) * LANES]).astype(bf16)
    c0 += IDX_HEADS * IDX_DIM // LANES
    y = p[:, c0 * LANES:(c0 + 1) * LANES]
    o_ref[:, c0 * LANES:(c0 + 1) * LANES] = rope(headnorm(y, kg_ref[...])).astype(bf16)
    c0 += 1
    o_ref[:, c0 * LANES:(c0 + 1) * LANES] = p[:, c0 * LANES:(c0 + 1) * LANES].astype(bf16)
    c0 += 1
    y = p[:, c0 * LANES:(c0 + 1) * LANES]
    o_ref[:, c0 * LANES:(c0 + 1) * LANES] = jnp.where(lane < IDX_DIM, rope(y), 0.0).astype(bf16)
    wi = pltpu.roll(y, LANES - IDX_DIM, 1) * (IDX_HEADS ** -0.5)
    wi_ref[...] = jnp.where(lane < IDX_HEADS, wi, 0.0)


def _dsa_proj(x2, g, sc, sh, w, pos, inv_lane, qg, kg, bd, seq, tm=512):
    n, d = x2.shape
    nc = w.shape[1]
    per = seq // tm
    return pl.pallas_call(
        functools.partial(_dsa_proj_kernel, d=d),
        out_shape=(jax.ShapeDtypeStruct((n, nc), bf16), jax.ShapeDtypeStruct((n, LANES), f32)),
        grid=(n // tm,),
        in_specs=[pl.BlockSpec((tm, d), lambda i: (i, 0)),
                  pl.BlockSpec((1, d), lambda i: (0, 0)),
                  pl.BlockSpec((1, 1, d), lambda i: (i // per, 0, 0)),
                  pl.BlockSpec((1, 1, d), lambda i: (i // per, 0, 0)),
                  pl.BlockSpec((d, nc), lambda i: (0, 0)),
                  pl.BlockSpec((tm, 1), lambda i: (i, 0)),
                  pl.BlockSpec((1, LANES), lambda i: (0, 0)),
                  pl.BlockSpec((1, LANES), lambda i: (0, 0)),
                  pl.BlockSpec((1, LANES), lambda i: (0, 0)),
                  pl.BlockSpec((LANES, LANES), lambda i: (0, 0))],
        out_specs=(pl.BlockSpec((tm, nc), lambda i: (i, 0)),
                   pl.BlockSpec((tm, LANES), lambda i: (i, 0))),
        compiler_params=_cparams(("arbitrary",)),
        name="dsa_in_proj",
    )(x2, g, sc, sh, w, pos, inv_lane, qg, kg, bd)


def _dsa_kernel(q_ref, k_ref, v_ref, qi_ref, ki_ref, wi_ref, o_ref,
                kk_ref, vv_ref, ki2_ref, keys_ref, bias_ref, m_ref, l_ref, acc_ref, *, topk):
    seq = k_ref.shape[0]
    nchunk = q_ref.shape[1] // LANES
    group = (2 * nchunk) // DSA_KV_HEADS
    i = pl.program_id(1)
    lane = lax.broadcasted_iota(jnp.int32, (1, LANES), 1)
    lo = lane < HEAD_DIM

    @pl.when(i == 0)
    def _():
        k = k_ref[...]
        v = v_ref[...]
        ki = ki_ref[...]
        kr = pltpu.roll(k.astype(f32), HEAD_DIM, 1).astype(bf16)
        vr = pltpu.roll(v.astype(f32), HEAD_DIM, 1).astype(bf16)
        zero = jnp.zeros_like(k)
        kk_ref[0] = jnp.where(lo, k, zero)
        kk_ref[1] = jnp.where(lo, zero, kr)
        kk_ref[2] = jnp.where(lo, kr, zero)
        kk_ref[3] = jnp.where(lo, zero, k)
        vv_ref[0] = jnp.where(lo, v, zero)
        vv_ref[1] = jnp.where(lo, zero, vr)
        vv_ref[2] = jnp.where(lo, vr, zero)
        vv_ref[3] = jnp.where(lo, zero, v)
        ki2_ref[0] = ki
        ki2_ref[1] = pltpu.roll(ki.astype(f32), HEAD_DIM, 1).astype(bf16)

    nkb = (i + 2) // 2
    qrow = lax.broadcasted_iota(jnp.int32, (QB, KB), 0) + i * QB
    kcol = lax.broadcasted_iota(jnp.int32, (QB, KB), 1)

    wi = wi_ref[...]
    wcols = [wi[:, hd:hd + 1] for hd in range(IDX_HEADS)]

    def score_block(jb, _):
        k0 = pl.multiple_of(jb * KB, KB)
        score = jnp.zeros((QB, KB), f32)
        for hd in range(IDX_HEADS):
            qc = qi_ref[:, (hd // 2) * LANES:(hd // 2 + 1) * LANES]
            kb = ki2_ref[hd % 2, pl.ds(k0, KB), :]
            rel = jnp.maximum(_dot_nt(qc, kb), 0.0) * (IDX_DIM ** -0.5)
            score = score + rel * wcols[hd]
        score = score + 0.0
        bits = pltpu.bitcast(score, jnp.int32)
        key = bits ^ ((bits >> 31) & 0x7FFFFFFF)
        adm = ((kcol + k0) >> CHUNK_SHIFT) <= (qrow >> CHUNK_SHIFT)
        keys_ref[jb] = jnp.where(adm, key, INT_MIN)
        return 0

    lax.fori_loop(0, nkb, score_block, 0)

    def count(pred_fn):
        def body(jb, part):
            return part + jnp.where(pred_fn(keys_ref[jb]), 1.0, 0.0)
        part = lax.fori_loop(0, nkb, body, jnp.zeros((QB, KB), f32))
        return jnp.sum(part, axis=1, keepdims=True)

    kf = float(topk)
    t0 = jnp.where(count(lambda key: key >= 0) >= kf, 0, INT_MIN).astype(jnp.int32)

    def bit_step(s, t):
        cand = t + jnp.left_shift(jnp.int32(1), 30 - s)
        return jnp.where(count(lambda key: key >= cand) >= kf, cand, t)

    thr = lax.fori_loop(0, 31, bit_step, t0)
    need = kf - count(lambda key: key > thr)

    r2 = lax.broadcasted_iota(jnp.int32, (KB, KB), 0)
    c2 = lax.broadcasted_iota(jnp.int32, (KB, KB), 1)
    tri = jnp.where(r2 < c2, 1.0, 0.0).astype(bf16)

    def bias_block(jb, carry):
        key = keys_ref[jb]
        eq = key == thr
        eqf = jnp.where(eq, 1.0, 0.0)
        rank = _dot(eqf.astype(bf16), tri) + carry
        sel = (key > thr) | (eq & (rank < need))
        sel = sel & (key > KEY_NEG_INF)
        bias_ref[jb] = jnp.where(sel, 0.0, NEG)
        return carry + jnp.sum(eqf, axis=1, keepdims=True)

    lax.fori_loop(0, nkb, bias_block, jnp.zeros((QB, 1), f32))

    m_ref[...] = jnp.full(m_ref.shape, NEG, f32)
    l_ref[...] = jnp.zeros(l_ref.shape, f32)
    acc_ref[...] = jnp.zeros(acc_ref.shape, f32)

    def attn_block(jb, _):
        k0 = pl.multiple_of(jb * KB, KB)
        bias = bias_ref[jb]
        for c in range(nchunk):
            q = q_ref[:, c * LANES:(c + 1) * LANES] * (HEAD_DIM ** -0.5)
            g = (2 * c) // group
            alphas = []
            pv = None
            for hf in range(2):
                s = _dot_nt(q, kk_ref[2 * g + hf, pl.ds(k0, KB), :]) + bias
                m_old = m_ref[2 * c + hf]
                m_new = jnp.maximum(m_old, jnp.max(s, axis=1, keepdims=True))
                alpha = jnp.exp(m_old - m_new)
                p = jnp.exp(s - m_new)
                l_ref[2 * c + hf] = alpha * l_ref[2 * c + hf] + jnp.sum(p, axis=1, keepdims=True)
                m_ref[2 * c + hf] = m_new
                alphas.append(alpha)
                d_pv = _dot(p.astype(bf16), vv_ref[2 * g + hf, pl.ds(k0, KB), :])
                pv = d_pv if pv is None else pv + d_pv
            acc_ref[c] = acc_ref[c] * jnp.where(lo, alphas[0], alphas[1]) + pv
        return 0

    lax.fori_loop(0, nkb, attn_block, 0)

    for c in range(nchunk):
        inv_l = jnp.where(lo, 1.0 / l_ref[2 * c], 1.0 / l_ref[2 * c + 1])
        o_ref[:, c * LANES:(c + 1) * LANES] = (acc_ref[c] * inv_l).astype(o_ref.dtype)


def group_even(d):
    return ((d // HEAD_DIM) // DSA_KV_HEADS) % 2 == 0


def _dsa_attention(proj, wi, batch, seq, d, topk):
    nqb = seq // QB
    nchunk = d // LANES
    qiw = IDX_HEADS * IDX_DIM
    assert d % qiw == 0 and group_even(d)
    qicol = d // qiw
    kcol = nchunk + qiw // LANES
    vcol = kcol + 1
    kicol = kcol + 2
    return pl.pallas_call(
        functools.partial(_dsa_kernel, topk=topk),
        out_shape=jax.ShapeDtypeStruct((batch * seq, d), bf16),
        grid=(batch, nqb),
        in_specs=[pl.BlockSpec((QB, d), lambda b, i: (b * nqb + i, 0)),
                  pl.BlockSpec((seq, LANES), lambda b, i: (b, kcol)),
                  pl.BlockSpec((seq, LANES), lambda b, i: (b, vcol)),
                  pl.BlockSpec((QB, IDX_HEADS * IDX_DIM), lambda b, i: (b * nqb + i, qicol)),
                  pl.BlockSpec((seq, LANES), lambda b, i: (b, kicol)),
                  pl.BlockSpec((QB, LANES), lambda b, i: (b * nqb + i, 0))],
        out_specs=pl.BlockSpec((QB, d), lambda b, i: (b * nqb + i, 0)),
        scratch_shapes=[pltpu.VMEM((4, seq, LANES), bf16),
                        pltpu.VMEM((4, seq, LANES), bf16),
                        pltpu.VMEM((2, seq, LANES), bf16),
                        pltpu.VMEM((seq // KB, QB, KB), jnp.int32),
                        pltpu.VMEM((seq // KB, QB, KB), f32),
                        pltpu.VMEM((2 * nchunk, QB, 1), f32),
                        pltpu.VMEM((2 * nchunk, QB, 1), f32),
                        pltpu.VMEM((nchunk, QB, LANES), f32)],
        compiler_params=_cparams(("arbitrary", "arbitrary")),
        name="dsa_sparse_attn",
    )(proj, proj, proj, proj, proj, wi)


def _post_kernel(o_ref, wo_ref, x_ref, g1_ref, n2_ref, sc_ref, sh_ref, rw_ref, rb_ref,
                 xo_ref, h_ref, comb_ref):
    y = _dot(o_ref[...], wo_ref[...])
    x = x_ref[...] + g1_ref[0] * y
    xo_ref[...] = x
    h = _norm_mod(x, n2_ref[...], sc_ref[0], sh_ref[0])
    h_ref[...] = h
    logits = jnp.dot(h, rw_ref[...], preferred_element_type=f32,
                     precision=lax.Precision.HIGHEST) + rb_ref[...]
    lane = lax.broadcasted_iota(jnp.int32, logits.shape, 1).astype(f32)
    work = logits
    vals = []
    hits = []
    for _ in range(TOP_K):
        m = jnp.max(work, axis=1, keepdims=True)
        idx = jnp.min(jnp.where(work == m, lane, float(LANES)), axis=1, keepdims=True)
        hit = lane == idx
        vals.append(m)
        hits.append(hit)
        work = jnp.where(hit, -jnp.inf, work)
    es = [jnp.exp(v - vals[0]) for v in vals]
    inv = 1.0 / (es[0] + es[1] + es[2] + es[3])
    comb = jnp.zeros_like(logits)
    for e, hit in zip(es, hits):
        comb = comb + jnp.where(hit, e * inv, 0.0)
    comb_ref[...] = comb


def _post_attention(o, wo, x2, g1, n2, sc2, sh2, rw, rb, seq, tm=512):
    n, d = x2.shape
    per = seq // tm
    row = lambda i: (i, 0)
    fix = lambda i: (0, 0)
    bat = lambda i: (i // per, 0, 0)
    return pl.pallas_call(
        _post_kernel,
        out_shape=(jax.ShapeDtypeStruct((n, d), f32), jax.ShapeDtypeStruct((n, d), f32),
                   jax.ShapeDtypeStruct((n, LANES), f32)),
        grid=(n // tm,),
        in_specs=[pl.BlockSpec((tm, d), row), pl.BlockSpec((d, d), fix),
                  pl.BlockSpec((tm, d), row), pl.BlockSpec((1, 1, d), bat),
                  pl.BlockSpec((1, d), fix), pl.BlockSpec((1, 1, d), bat),
                  pl.BlockSpec((1, 1, d), bat), pl.BlockSpec((d, LANES), fix),
                  pl.BlockSpec((1, LANES), fix)],
        out_specs=(pl.BlockSpec((tm, d), row), pl.BlockSpec((tm, d), row),
                   pl.BlockSpec((tm, LANES), row)),
        compiler_params=_cparams(("arbitrary",)),
        name="attn_out_norm_router",
    )(o, wo, x2, g1, n2, sc2, sh2, rw, rb)


def _moe_kernel(h_ref, comb_ref, x_ref, g2_ref, wgu_ref, bgu_ref, wd_ref, bd_ref, o_ref,
                hb_ref, acc_ref, *, ff):
    e = pl.program_id(1)

    @pl.when(e == 0)
    def _():
        hb_ref[...] = h_ref[...].astype(bf16)
        acc_ref[...] = jnp.zeros(acc_ref.shape, f32)

    gu = _dot(hb_ref[...], wgu_ref[0]) + bgu_ref[0]
    g = jnp.minimum(gu[:, :ff], SWIGLU_LIMIT)
    u = jnp.clip(gu[:, ff:], -SWIGLU_LIMIT, SWIGLU_LIMIT)
    act = (u + 1.0) * (g * (1.0 / (1.0 + jnp.exp(-SWIGLU_ALPHA * g))))
    y = _dot(act.astype(bf16), wd_ref[0]) + bd_ref[0]
    comb = comb_ref[...]
    lane = lax.broadcasted_iota(jnp.int32, comb.shape, 1)
    w = jnp.sum(jnp.where(lane == e, comb, 0.0), axis=1, keepdims=True)
    acc_ref[...] += w * y

    @pl.when(e == pl.num_programs(1) - 1)
    def _():
        o_ref[...] = x_ref[...] + g2_ref[0] * acc_ref[...]


def _moe(h, comb, x2, g2, wgu, bgu, wd, bdn, seq, tm=512):
    n, d = x2.shape
    ne, _, ff2 = wgu.shape
    ff = ff2 // 2
    per = seq // tm
    row = lambda i, e: (i, 0)
    exp3 = lambda i, e: (e, 0, 0)
    return pl.pallas_call(
        functools.partial(_moe_kernel, ff=ff),
        out_shape=jax.ShapeDtypeStruct((n, d), f32),
        grid=(n // tm, ne),
        in_specs=[pl.BlockSpec((tm, d), row), pl.BlockSpec((tm, LANES), row),
                  pl.BlockSpec((tm, d), row),
                  pl.BlockSpec((1, 1, d), lambda i, e: (i // per, 0, 0)),
                  pl.BlockSpec((1, d, ff2), exp3), pl.BlockSpec((1, 1, ff2), exp3),
                  pl.BlockSpec((1, ff, d), exp3), pl.BlockSpec((1, 1, d), exp3)],
        out_specs=pl.BlockSpec((tm, d), row),
        scratch_shapes=[pltpu.VMEM((tm, d), bf16), pltpu.VMEM((tm, d), f32)],
        compiler_params=_cparams(("arbitrary", "arbitrary")),
        name="moe_experts",
    )(h, comb, x2, g2, wgu, bgu, wd, bdn)


def _lane_tile(v):
    return jnp.tile(v.astype(f32), LANES // v.shape[0]).reshape(1, LANES)


def kernel(x, c, positions, ada_w, ada_b, norm1_g, norm2_g, sb_w_qkv, sb_w_o, dsa_w_in,
           dsa_q_gain, dsa_k_gain, dsa_w_o, router_w, router_b, exp_w_gu, exp_b_gu,
           exp_w_down, exp_b_down):
    batch, seq, d = x.shape
    depth = ada_w.shape[0]
    n = batch * seq
    ne = router_w.shape[-1]
    ff = exp_w_down.shape[2]
    topk = min(TOPK_MAX, seq // 4)
    assert seq % KB == 0 and d % LANES == 0 and ne <= LANES

    mod = _modulation(c, ada_w, ada_b)
    x2 = x.reshape(n, d)

    ncols = dsa_w_in.shape[-1]
    ncols_pad = -(-ncols // LANES) * LANES
    half = ROPE_DIMS // 2
    inv = jnp.exp(-math.log(ROPE_THETA) * (2.0 * jnp.arange(half, dtype=f32) / ROPE_DIMS))
    l64 = jnp.arange(LANES) % HEAD_DIM
    inv_lane = jnp.where(l64 < ROPE_DIMS, inv[l64 % half], 0.0).reshape(1, LANES)
    bd = (jnp.arange(LANES)[:, None] // HEAD_DIM == jnp.arange(LANES)[None, :] // HEAD_DIM)
    bd = bd.astype(bf16)
    pos = positions.reshape(n, 1)

    for layer in range(depth):
        m6 = [mod[layer][:, k * d:(k + 1) * d].reshape(batch, 1, d) for k in range(6)]
        sh1, sc1, g1, sh2, sc2, g2 = m6
        n1 = norm1_g[layer].reshape(1, d)
        n2 = norm2_g[layer].reshape(1, d)
        j = layer // 2
        if layer % 2 == 0:
            qkv = _proj(x2, n1, sc1, sh1, sb_w_qkv[j].astype(bf16), seq)
            o = _stick_breaking(qkv, batch, seq, d)
            wo = sb_w_o[j].astype(bf16)
        else:
            w = dsa_w_in[j]
            kv0 = d
            qi0 = d + 2 * LANES
            ki0 = qi0 + IDX_HEADS * IDX_DIM
            w = jnp.concatenate([w[:, :kv0], w[:, qi0:ki0], w[:, kv0:qi0], w[:, ki0:]], axis=1)
            w = jnp.pad(w, ((0, 0), (0, ncols_pad - ncols))).astype(bf16)
            proj, wi = _dsa_proj(x2, n1, sc1, sh1, w, pos, inv_lane, _lane_tile(dsa_q_gain[j]),
                                 _lane_tile(dsa_k_gain[j]), bd, seq)
            o = _dsa_attention(proj, wi, batch, seq, d, topk)
            wo = dsa_w_o[j].astype(bf16)
        rw = jnp.pad(router_w[layer], ((0, 0), (0, LANES - ne)))
        rb = jnp.pad(router_b[layer], (0, LANES - ne), constant_values=-jnp.inf).reshape(1, LANES)
        x2, h, comb = _post_attention(o, wo, x2, g1, n2, sc2, sh2, rw, rb, seq)
        wgu = exp_w_gu[layer]
        wgu = jnp.concatenate([wgu[..., 0::2], wgu[..., 1::2]], axis=-1).astype(bf16)
        bgu = exp_b_gu[layer]
        bgu = jnp.concatenate([bgu[..., 0::2], bgu[..., 1::2]], axis=-1).reshape(ne, 1, 2 * ff)
        x2 = _moe(h, comb, x2, g2, wgu, bgu, exp_w_down[layer].astype(bf16),
                  exp_b_down[layer].reshape(ne, 1, d), seq)
    return x2.reshape(batch, seq, d)
```

```python
import functools
import math

import jax
import jax.numpy as jnp
from jax import lax
from jax.experimental import pallas as pl
from jax.experimental.pallas import tpu as pltpu

HEAD_DIM = 64
DSA_KV_HEADS = 2
IDX_HEADS = 8
IDX_DIM = 64
CHUNK_SHIFT = 6
TOPK_MAX = 256
TOP_K = 4
ROPE_THETA = 500000.0
ROPE_DIMS = HEAD_DIM // 4
SWIGLU_ALPHA = 1.702
SWIGLU_LIMIT = 7.0
EPS = 1e-6

LANES = 128
QB = 128
KB = 256
DQ = 256
NEG = -1e30
INT_MIN = -(2 ** 31)
KEY_NEG_INF = -2139095041
VMEM_LIMIT = 48 * 1024 * 1024
POST_TM = 512
MOE_TM = 512
MOE_TOK = 256
DMA_UNROLL = 8
SB_WINDOW = 3
SB_CUTOFF = -110.0

f32 = jnp.float32
bf16 = jnp.bfloat16


def _cparams(sem):
    return pltpu.CompilerParams(dimension_semantics=sem, vmem_limit_bytes=VMEM_LIMIT)


def _dot(a, b):
    return jnp.dot(a, b, preferred_element_type=f32)


def _dot_nt(a, b):
    return lax.dot_general(a, b, (((1,), (1,)), ((), ())), preferred_element_type=f32)


def _dot_split(x, m01, passes):
    acc = None
    r = x
    for p in range(passes):
        t = r.astype(bf16)
        d = _dot(t, m01)
        acc = d if acc is None else acc + d
        if p + 1 < passes:
            r = r - t.astype(f32)
    return acc


def _norm_mod(x, g, sc, sh):
    ms = jnp.mean(x * x, axis=-1, keepdims=True)
    return (x * lax.rsqrt(ms + EPS) * g) * (1.0 + sc) + sh


def _mod_kernel(c_ref, w_ref, b_ref, o_ref):
    c = c_ref[...]
    cs = c * (1.0 / (1.0 + jnp.exp(-c)))
    o_ref[0] = jnp.dot(cs, w_ref[0], preferred_element_type=f32,
                       precision=lax.Precision.HIGHEST) + b_ref[0]


def _modulation(c, ada_w, ada_b):
    depth, d, n6 = ada_w.shape
    b = c.shape[0]
    tn = 1024
    return pl.pallas_call(
        _mod_kernel,
        out_shape=jax.ShapeDtypeStruct((depth, b, n6), f32),
        grid=(depth, n6 // tn),
        in_specs=[pl.BlockSpec((b, d), lambda l, j: (0, 0)),
                  pl.BlockSpec((1, d, tn), lambda l, j: (l, 0, j)),
                  pl.BlockSpec((1, 1, tn), lambda l, j: (l, 0, j))],
        out_specs=pl.BlockSpec((1, b, tn), lambda l, j: (l, 0, j)),
        compiler_params=_cparams(("arbitrary", "arbitrary")),
        name="adaln_mod",
    )(c, ada_w, ada_b.reshape(depth, 1, n6))


def _proj_kernel(x_ref, g_ref, sc_ref, sh_ref, w_ref, o_ref):
    h = _norm_mod(x_ref[...], g_ref[...], sc_ref[0], sh_ref[0])
    o_ref[...] = _dot(h.astype(bf16), w_ref[...]).astype(o_ref.dtype)


def _proj(x2, g, sc, sh, w, seq, tm=512):
    n, d = x2.shape
    nc = w.shape[1]
    per = seq // tm
    return pl.pallas_call(
        _proj_kernel,
        out_shape=jax.ShapeDtypeStruct((n, nc), bf16),
        grid=(n // tm,),
        in_specs=[pl.BlockSpec((tm, d), lambda i: (i, 0)),
                  pl.BlockSpec((1, d), lambda i: (0, 0)),
                  pl.BlockSpec((1, 1, d), lambda i: (i // per, 0, 0)),
                  pl.BlockSpec((1, 1, d), lambda i: (i // per, 0, 0)),
                  pl.BlockSpec((d, nc), lambda i: (0, 0))],
        out_specs=pl.BlockSpec((tm, nc), lambda i: (i, 0)),
        compiler_params=_cparams(("arbitrary",)),
        name="sb_qkv_proj",
    )(x2, g, sc, sh, w)


def _sb_kernel(q_ref, k_ref, v_ref, o_ref, k0_ref, k1_ref):
    seq = q_ref.shape[0]
    nqb = seq // QB
    lane = lax.broadcasted_iota(jnp.int32, (1, LANES), 1)
    kk = k_ref[...]
    zero = jnp.zeros_like(kk)
    k0_ref[...] = jnp.where(lane < HEAD_DIM, kk, zero)
    k1_ref[...] = jnp.where(lane >= HEAD_DIM, kk, zero)
    row = lax.broadcasted_iota(jnp.int32, (QB, QB), 0)
    col = lax.broadcasted_iota(jnp.int32, (QB, QB), 1)
    before = col < row
    tri = jnp.where(row > col, 1.0, 0.0).astype(bf16)

    def window(q, kh_ref, k0, nb, c_in, diag):
        z = _dot_nt(q, kh_ref[pl.ds(k0, nb * QB), :])
        l1p = jnp.log(1.0 + jnp.exp(-jnp.abs(z)))
        lm = jnp.minimum(-z, 0.0) - l1p
        ls = jnp.minimum(z, 0.0) - l1p
        lms = [lm[:, s * QB:(s + 1) * QB] for s in range(nb)]
        if diag:
            lms[-1] = jnp.where(before, lms[-1], 0.0)
        c = c_in
        probs = [None] * nb
        for s in reversed(range(nb)):
            tail = _dot_split(lms[s], tri, 3) + c
            a = jnp.exp(ls[:, s * QB:(s + 1) * QB] + tail)
            if diag and s == nb - 1:
                a = jnp.where(before, a, 0.0)
            probs[s] = a.astype(bf16)
            c = c + jnp.sum(lms[s], axis=1, keepdims=True)
        a_all = probs[0] if nb == 1 else jnp.concatenate(probs, axis=1)
        return c, _dot(a_all, v_ref[pl.ds(k0, nb * QB), :])

    def qblock(i, nb, static):
        q0 = i * QB if static else pl.multiple_of(i * QB, QB)
        k0 = q0 - (nb - 1) * QB
        if not static:
            k0 = pl.multiple_of(k0, QB)
        q = q_ref[pl.ds(q0, QB), :] * (HEAD_DIM ** -0.5)
        zc = jnp.zeros((QB, 1), f32)
        c0, a0 = window(q, k0_ref, k0, nb, zc, True)
        c1, a1 = window(q, k1_ref, k0, nb, zc, True)
        if not static:
            def cond(st):
                jb, c0, _, c1, _ = st
                live = jnp.max(jnp.maximum(c0, c1)) > SB_CUTOFF
                return jnp.logical_and(jb >= 0, live)

            def body(st):
                jb, c0, a0, c1, a1 = st
                kb = pl.multiple_of(jb * QB, QB)
                c0, d0 = window(q, k0_ref, kb, 1, c0, False)
                c1, d1 = window(q, k1_ref, kb, 1, c1, False)
                return jb - 1, c0, a0 + d0, c1, a1 + d1

            _, c0, a0, c1, a1 = lax.while_loop(cond, body, (i - nb, c0, a0, c1, a1))
        o_ref[pl.ds(q0, QB), :] = jnp.where(lane < HEAD_DIM, a0, a1).astype(o_ref.dtype)

    for i in range(min(SB_WINDOW, nqb)):
        qblock(i, i + 1, True)

    def loop(i, _):
        qblock(i, SB_WINDOW, False)
        return 0

    lax.fori_loop(SB_WINDOW, nqb, loop, 0)


def _stick_breaking(qkv, batch, seq, d):
    npair = d // LANES
    return pl.pallas_call(
        _sb_kernel,
        out_shape=jax.ShapeDtypeStruct((batch * seq, d), bf16),
        grid=(batch, npair),
        in_specs=[pl.BlockSpec((seq, LANES), lambda b, p: (b, p)),
                  pl.BlockSpec((seq, LANES), lambda b, p: (b, npair + p)),
                  pl.BlockSpec((seq, LANES), lambda b, p: (b, 2 * npair + p))],
        out_specs=pl.BlockSpec((seq, LANES), lambda b, p: (b, p)),
        scratch_shapes=[pltpu.VMEM((seq, LANES), bf16), pltpu.VMEM((seq, LANES), bf16)],
        compiler_params=_cparams(("arbitrary", "arbitrary")),
        name="stick_breaking_attn",
    )(qkv, qkv, qkv)


def _dsa_proj_kernel(x_ref, g_ref, sc_ref, sh_ref, w_ref, pos_ref, inv_ref, qg_ref, kg_ref,
                     bd_ref, o_ref, wi_ref, *, d):
    h = _norm_mod(x_ref[...], g_ref[...], sc_ref[0], sh_ref[0])
    p = _dot(h.astype(bf16), w_ref[...])
    lane = lax.broadcasted_iota(jnp.int32, (1, LANES), 1)
    ang = pos_ref[...].astype(f32) * inv_ref[...]
    cos_t = jnp.cos(ang)
    sin_t = jnp.sin(ang)
    upper = (lane % ROPE_DIMS) >= (ROPE_DIMS // 2)
    s_up = jnp.where(upper, sin_t, 0.0)
    s_lo = jnp.where(upper, 0.0, -sin_t)
    half = ROPE_DIMS // 2

    def rope(y):
        return y * cos_t + pltpu.roll(y, half, 1) * s_up + pltpu.roll(y, LANES - half, 1) * s_lo

    def headnorm(y, gain):
        ms = _dot_split(y * y, bd_ref[...], 2) * (1.0 / HEAD_DIM)
        return y * lax.rsqrt(ms + EPS) * gain

    nq = d // LANES
    for c in range(nq):
        y = p[:, c * LANES:(c + 1) * LANES]
        o_ref[:, c * LANES:(c + 1) * LANES] = rope(headnorm(y, qg_ref[...])).astype(bf16)
    c0 = nq
    for c in range(c0, c0 + IDX_HEADS * IDX_DIM // LANES):
        o_ref[:, c * LANES:(c + 1) * LANES] = rope(p[:, c * LANES:(c + 1) * LANES]).astype(bf16)
    c0 += IDX_HEADS * IDX_DIM // LANES
    y = p[:, c0 * LANES:(c0 + 1) * LANES]
    o_ref[:, c0 * LANES:(c0 + 1) * LANES] = rope(headnorm(y, kg_ref[...])).astype(bf16)
    c0 += 1
    o_ref[:, c0 * LANES:(c0 + 1) * LANES] = p[:, c0 * LANES:(c0 + 1) * LANES].astype(bf16)
    c0 += 1
    y = p[:, c0 * LANES:(c0 + 1) * LANES]
    o_ref[:, c0 * LANES:(c0 + 1) * LANES] = jnp.where(lane < IDX_DIM, rope(y), 0.0).astype(bf16)
    wi = pltpu.roll(y, LANES - IDX_DIM, 1) * (IDX_HEADS ** -0.5)
    wi_ref[...] = jnp.where(lane < IDX_HEADS, wi, 0.0)


def _dsa_proj(x2, g, sc, sh, w, pos, inv_lane, qg, kg, bd, seq, tm=512):
    n, d = x2.shape
    nc = w.shape[1]
    per = seq // tm
    return pl.pallas_call(
        functools.partial(_dsa_proj_kernel, d=d),
        out_shape=(jax.ShapeDtypeStruct((n, nc), bf16), jax.ShapeDtypeStruct((n, LANES), f32)),
        grid=(n // tm,),
        in_specs=[pl.BlockSpec((tm, d), lambda i: (i, 0)),
                  pl.BlockSpec((1, d), lambda i: (0, 0)),
                  pl.BlockSpec((1, 1, d), lambda i: (i // per, 0, 0)),
                  pl.BlockSpec((1, 1, d), lambda i: (i // per, 0, 0)),
                  pl.BlockSpec((d, nc), lambda i: (0, 0)),
                  pl.BlockSpec((tm, 1), lambda i: (i, 0)),
                  pl.BlockSpec((1, LANES), lambda i: (0, 0)),
                  pl.BlockSpec((1, LANES), lambda i: (0, 0)),
                  pl.BlockSpec((1, LANES), lambda i: (0, 0)),
                  pl.BlockSpec((LANES, LANES), lambda i: (0, 0))],
        out_specs=(pl.BlockSpec((tm, nc), lambda i: (i, 0)),
                   pl.BlockSpec((tm, LANES), lambda i: (i, 0))),
        compiler_params=_cparams(("arbitrary",)),
        name="dsa_in_proj",
    )(x2, g, sc, sh, w, pos, inv_lane, qg, kg, bd)


def _dsa_kernel(q_ref, k_ref, v_ref, qi_ref, ki_ref, wi_ref, o_ref,
                kk_ref, vv_ref, ki2_ref, keys_ref, bias_ref, m_ref, l_ref, acc_ref, *, topk):
    seq = k_ref.shape[0]
    nchunk = q_ref.shape[1] // LANES
    group = (2 * nchunk) // DSA_KV_HEADS
    i = pl.program_id(1)
    lane = lax.broadcasted_iota(jnp.int32, (1, LANES), 1)
    lo = lane < HEAD_DIM

    @pl.when(i == 0)
    def _():
        k = k_ref[...]
        v = v_ref[...]
        ki = ki_ref[...]
        kr = pltpu.roll(k.astype(f32), HEAD_DIM, 1).astype(bf16)
        vr = pltpu.roll(v.astype(f32), HEAD_DIM, 1).astype(bf16)
        zero = jnp.zeros_like(k)
        kk_ref[0] = jnp.where(lo, k, zero)
        kk_ref[1] = jnp.where(lo, zero, kr)
        kk_ref[2] = jnp.where(lo, kr, zero)
        kk_ref[3] = jnp.where(lo, zero, k)
        vv_ref[0] = jnp.where(lo, v, zero)
        vv_ref[1] = jnp.where(lo, zero, vr)
        vv_ref[2] = jnp.where(lo, vr, zero)
        vv_ref[3] = jnp.where(lo, zero, v)
        ki2_ref[0] = ki
        ki2_ref[1] = pltpu.roll(ki.astype(f32), HEAD_DIM, 1).astype(bf16)

    nkb = i + 1
    qrow = lax.broadcasted_iota(jnp.int32, (DQ, KB), 0) + i * DQ
    kcol = lax.broadcasted_iota(jnp.int32, (DQ, KB), 1)

    wi = wi_ref[...]
    wcols = [wi[:, hd:hd + 1] for hd in range(IDX_HEADS)]

    def score_block(jb, _):
        k0 = pl.multiple_of(jb * KB, KB)
        score = jnp.zeros((DQ, KB), f32)
        for hd in range(IDX_HEADS):
            qc = qi_ref[:, (hd // 2) * LANES:(hd // 2 + 1) * LANES]
            kb = ki2_ref[hd % 2, pl.ds(k0, KB), :]
            rel = jnp.maximum(_dot_nt(qc, kb), 0.0) * (IDX_DIM ** -0.5)
            score = score + rel * wcols[hd]
        score = score + 0.0
        bits = pltpu.bitcast(score, jnp.int32)
        key = bits ^ ((bits >> 31) & 0x7FFFFFFF)
        adm = ((kcol + k0) >> CHUNK_SHIFT) <= (qrow >> CHUNK_SHIFT)
        keys_ref[jb] = jnp.where(adm, key, INT_MIN)
        return 0

    lax.fori_loop(0, nkb, score_block, 0)

    def count(pred_fn):
        def body(jb, part):
            return part + jnp.where(pred_fn(keys_ref[jb]), 1.0, 0.0)
        part = lax.fori_loop(0, nkb, body, jnp.zeros((DQ, KB), f32))
        return jnp.sum(part, axis=1, keepdims=True)

    kf = float(topk)
    t0 = jnp.where(count(lambda key: key >= 0) >= kf, 0, INT_MIN).astype(jnp.int32)

    def bit_step(s, t):
        cand = t + jnp.left_shift(jnp.int32(1), 30 - s)
        return jnp.where(count(lambda key: key >= cand) >= kf, cand, t)

    thr = lax.fori_loop(0, 31, bit_step, t0)
    need = kf - count(lambda key: key > thr)

    r2 = lax.broadcasted_iota(jnp.int32, (KB, KB), 0)
    c2 = lax.broadcasted_iota(jnp.int32, (KB, KB), 1)
    tri = jnp.where(r2 < c2, 1.0, 0.0).astype(bf16)

    def bias_block(jb, carry):
        key = keys_ref[jb]
        eq = key == thr
        eqf = jnp.where(eq, 1.0, 0.0)
        rank = _dot(eqf.astype(bf16), tri) + carry
        sel = (key > thr) | (eq & (rank < need))
        sel = sel & (key > KEY_NEG_INF)
        bias_ref[jb] = jnp.where(sel, 0.0, NEG)
        return carry + jnp.sum(eqf, axis=1, keepdims=True)

    lax.fori_loop(0, nkb, bias_block, jnp.zeros((DQ, 1), f32))

    def scores(c, hf, k0, bias):
        q = q_ref[:, c * LANES:(c + 1) * LANES] * (HEAD_DIM ** -0.5)
        g = (2 * c) // group
        return _dot_nt(q, kk_ref[2 * g + hf, pl.ds(k0, KB), :]) + bias

    m_ref[...] = jnp.full(m_ref.shape, NEG, f32)

    def max_block(jb, _):
        k0 = pl.multiple_of(jb * KB, KB)
        bias = bias_ref[jb]
        for c in range(nchunk):
            for hf in range(2):
                s = scores(c, hf, k0, bias)
                fold = jnp.maximum(s[:, :LANES], s[:, LANES:])
                m_ref[2 * c + hf] = jnp.maximum(m_ref[2 * c + hf], fold)
        return 0

    lax.fori_loop(0, nkb, max_block, 0)
    for h in range(2 * nchunk):
        m = jnp.max(m_ref[h], axis=1, keepdims=True)
        m_ref[h] = jnp.broadcast_to(m, (DQ, LANES))

    acc_ref[...] = jnp.zeros(acc_ref.shape, f32)
    l_ref[...] = jnp.zeros(l_ref.shape, f32)
    r2h = lax.broadcasted_iota(jnp.int32, (2 * KB, LANES), 0) < KB
    c2h = lax.broadcasted_iota(jnp.int32, (2 * KB, LANES), 1) < HEAD_DIM
    ones_half = jnp.where(r2h == c2h, 1.0, 0.0).astype(bf16)

    def attn_block(jb, _):
        k0 = pl.multiple_of(jb * KB, KB)
        bias = bias_ref[jb]
        for c in range(nchunk):
            g = (2 * c) // group
            ps = []
            for hf in range(2):
                m = m_ref[2 * c + hf]
                s = scores(c, hf, k0, bias)
                ps.append(jnp.exp(s - jnp.concatenate([m, m], axis=1)).astype(bf16))
            p2 = jnp.concatenate(ps, axis=1)
            v2 = jnp.concatenate([vv_ref[2 * g, pl.ds(k0, KB), :],
                                  vv_ref[2 * g + 1, pl.ds(k0, KB), :]], axis=0)
            acc_ref[c] += _dot(p2, v2)
            l_ref[c] += _dot(p2, ones_half)
        return 0

    lax.fori_loop(0, nkb, attn_block, 0)

    for c in range(nchunk):
        o_ref[:, c * LANES:(c + 1) * LANES] = (acc_ref[c] / l_ref[c]).astype(o_ref.dtype)


def group_even(d):
    return ((d // HEAD_DIM) // DSA_KV_HEADS) % 2 == 0


def _dsa_attention(proj, wi, batch, seq, d, topk):
    nqb = seq // DQ
    nchunk = d // LANES
    qiw = IDX_HEADS * IDX_DIM
    assert d % qiw == 0 and group_even(d) and DQ == KB
    qicol = d // qiw
    kcol = nchunk + qiw // LANES
    vcol = kcol + 1
    kicol = kcol + 2
    return pl.pallas_call(
        functools.partial(_dsa_kernel, topk=topk),
        out_shape=jax.ShapeDtypeStruct((batch * seq, d), bf16),
        grid=(batch, nqb),
        in_specs=[pl.BlockSpec((DQ, d), lambda b, i: (b * nqb + i, 0)),
                  pl.BlockSpec((seq, LANES), lambda b, i: (b, kcol)),
                  pl.BlockSpec((seq, LANES), lambda b, i: (b, vcol)),
                  pl.BlockSpec((DQ, IDX_HEADS * IDX_DIM), lambda b, i: (b * nqb + i, qicol)),
                  pl.BlockSpec((seq, LANES), lambda b, i: (b, kicol)),
                  pl.BlockSpec((DQ, LANES), lambda b, i: (b * nqb + i, 0))],
        out_specs=pl.BlockSpec((DQ, d), lambda b, i: (b * nqb + i, 0)),
        scratch_shapes=[pltpu.VMEM((4, seq, LANES), bf16),
                        pltpu.VMEM((4, seq, LANES), bf16),
                        pltpu.VMEM((2, seq, LANES), bf16),
                        pltpu.VMEM((seq // KB, DQ, KB), jnp.int32),
                        pltpu.VMEM((seq // KB, DQ, KB), f32),
                        pltpu.VMEM((2 * nchunk, DQ, LANES), f32),
                        pltpu.VMEM((nchunk, DQ, LANES), f32),
                        pltpu.VMEM((nchunk, DQ, LANES), f32)],
        compiler_params=_cparams(("arbitrary", "arbitrary")),
        name="dsa_sparse_attn",
    )(proj, proj, proj, proj, proj, wi)


def _post_kernel(o_ref, wo_ref, x_ref, g1_ref, n2_ref, sc_ref, sh_ref, rw_ref, rb_ref, tri_ref,
                 xo_ref, hf_ref, idx_ref, gate_ref, rank_ref, cnt_ref, run_ref):
    @pl.when(pl.program_id(0) == 0)
    def _():
        run_ref[...] = jnp.zeros(run_ref.shape, f32)

    y = _dot(o_ref[...], wo_ref[...])
    x = x_ref[...] + g1_ref[0] * y
    xo_ref[...] = x
    h = _norm_mod(x, n2_ref[...], sc_ref[0], sh_ref[0])
    tm, d = h.shape
    nchunk = d // LANES
    for c in range(nchunk):
        hf_ref[pl.ds(c, tm, stride=nchunk), :] = h[:, c * LANES:(c + 1) * LANES]
    logits = jnp.dot(h, rw_ref[...], preferred_element_type=f32,
                     precision=lax.Precision.HIGHEST) + rb_ref[...]
    lane = lax.broadcasted_iota(jnp.int32, logits.shape, 1).astype(f32)
    work = logits
    vals, idxs, hits = [], [], []
    for _ in range(TOP_K):
        m = jnp.max(work, axis=1, keepdims=True)
        idx = jnp.min(jnp.where(work == m, lane, float(LANES)), axis=1, keepdims=True)
        hit = lane == idx
        vals.append(m)
        idxs.append(idx)
        hits.append(hit)
        work = jnp.where(hit, -jnp.inf, work)
    es = [jnp.exp(v - vals[0]) for v in vals]
    inv = 1.0 / (es[0] + es[1] + es[2] + es[3])
    multi = jnp.zeros_like(logits)
    for hit in hits:
        multi = multi + jnp.where(hit, 1.0, 0.0)
    before = _dot(tri_ref[...], multi.astype(bf16)) + run_ref[...]
    idx_o = jnp.zeros_like(logits)
    gate_o = jnp.zeros_like(logits)
    rank_o = jnp.zeros_like(logits)
    for k in range(TOP_K):
        slot = lane == float(k)
        rank = jnp.sum(jnp.where(hits[k], before, 0.0), axis=1, keepdims=True)
        idx_o = jnp.where(slot, idxs[k], idx_o)
        gate_o = jnp.where(slot, es[k] * inv, gate_o)
        rank_o = jnp.where(slot, rank, rank_o)
    idx_ref[...] = idx_o
    gate_ref[...] = gate_o
    rank_ref[...] = rank_o
    run_ref[...] += jnp.sum(multi, axis=0, keepdims=True)
    cnt_ref[...] = run_ref[...]


def _post_attention(o, wo, x2, g1, n2, sc2, sh2, rw, rb, tri, seq, tm):
    n, d = x2.shape
    per = seq // tm
    nchunk = d // LANES
    row = lambda i: (i, 0)
    fix = lambda i: (0, 0)
    bat = lambda i: (i // per, 0, 0)
    lane_out = jax.ShapeDtypeStruct((n, LANES), f32)
    return pl.pallas_call(
        _post_kernel,
        out_shape=(jax.ShapeDtypeStruct((n, d), f32), jax.ShapeDtypeStruct((n * nchunk, LANES), f32),
                   lane_out, lane_out, lane_out, jax.ShapeDtypeStruct((1, LANES), f32)),
        grid=(n // tm,),
        in_specs=[pl.BlockSpec((tm, d), row), pl.BlockSpec((d, d), fix),
                  pl.BlockSpec((tm, d), row), pl.BlockSpec((1, 1, d), bat),
                  pl.BlockSpec((1, d), fix), pl.BlockSpec((1, 1, d), bat),
                  pl.BlockSpec((1, 1, d), bat), pl.BlockSpec((d, LANES), fix),
                  pl.BlockSpec((1, LANES), fix), pl.BlockSpec((tm, tm), fix)],
        out_specs=(pl.BlockSpec((tm, d), row), pl.BlockSpec((tm * nchunk, LANES), row),
                   pl.BlockSpec((tm, LANES), row), pl.BlockSpec((tm, LANES), row),
                   pl.BlockSpec((tm, LANES), row), pl.BlockSpec((1, LANES), fix)),
        scratch_shapes=[pltpu.VMEM((1, LANES), f32)],
        compiler_params=_cparams(("arbitrary",)),
        name="attn_out_norm_router",
    )(o, wo, x2, g1, n2, sc2, sh2, rw, rb, tri)


def _deint_kernel(w_ref, p_ref, o_ref):
    ff = o_ref.shape[2] // 2
    wide = 2 * LANES
    for b in range(o_ref.shape[2] // wide):
        x = w_ref[0, :, b * wide:(b + 1) * wide].astype(bf16)
        r = _dot(x, p_ref[...]).astype(bf16)
        o_ref[0, :, b * LANES:(b + 1) * LANES] = r[:, :LANES]
        o_ref[0, :, ff + b * LANES:ff + (b + 1) * LANES] = r[:, LANES:]


def _deinterleave(w):
    ne, d, ff2 = w.shape
    wide = 2 * LANES
    src = jnp.arange(wide)
    perm = jnp.where(src < LANES, 2 * src, 2 * (src - LANES) + 1)
    p = (jnp.arange(wide)[:, None] == perm[None, :]).astype(bf16)
    return pl.pallas_call(
        _deint_kernel,
        out_shape=jax.ShapeDtypeStruct((ne, d, ff2), bf16),
        grid=(ne,),
        in_specs=[pl.BlockSpec((1, d, ff2), lambda e: (e, 0, 0)),
                  pl.BlockSpec((wide, wide), lambda e: (0, 0))],
        out_specs=pl.BlockSpec((1, d, ff2), lambda e: (e, 0, 0)),
        compiler_params=_cparams(("arbitrary",)),
        name="expert_weight_prep",
    )(w, p)


def _dispatch_kernel(zt_ref, pos_ref, hf_ref, xs_ref, zbuf_ref, sem, zsem, *, t_tok, tm, nchunk):
    i = pl.program_id(0)
    ne = zt_ref.shape[0] - 1
    ntile = xs_ref.shape[0] // (tm * nchunk)

    def zero_tile(e):
        return zt_ref[e] if e < ne else zt_ref[ne] + (e - ne)

    def zero_copy(e):
        start = pl.multiple_of(zero_tile(e) * (tm * nchunk), tm * nchunk)
        return pltpu.make_async_copy(zbuf_ref, xs_ref.at[pl.ds(start, tm * nchunk)], zsem)

    def zero_wanted(e):
        return zero_tile(e) >= 0 if e < ne else zero_tile(e) < ntile

    @pl.when(i == 0)
    def _():
        zbuf_ref[...] = jnp.zeros(zbuf_ref.shape, f32)
        for e in range(2 * ne):
            @pl.when(zero_wanted(e))
            def _():
                zero_copy(e).start()
        for e in range(2 * ne):
            @pl.when(zero_wanted(e))
            def _():
                zero_copy(e).wait()

    base = i * t_tok

    def issue(tt, _):
        for u in range(DMA_UNROLL):
            t = tt * DMA_UNROLL + u
            src = hf_ref.at[pl.ds(pl.multiple_of((base + t) * nchunk, nchunk), nchunk)]
            for k in range(TOP_K):
                p = pos_ref[0, 0, t * TOP_K + k]
                dst = xs_ref.at[pl.ds(pl.multiple_of(p * nchunk, nchunk), nchunk)]
                pltpu.make_async_copy(src, dst, sem).start()
        return 0

    lax.fori_loop(0, t_tok // DMA_UNROLL, issue, 0)
    for k in range(TOP_K):
        pltpu.make_async_copy(hf_ref.at[pl.ds(0, t_tok * nchunk)],
                              xs_ref.at[pl.ds(0, t_tok * nchunk)], sem).wait()


def _dispatch(hflat, pos3, ztile, rows_pad, tm, t_tok, nchunk):
    ntile = pos3.shape[0]
    return pl.pallas_call(
        functools.partial(_dispatch_kernel, t_tok=t_tok, tm=tm, nchunk=nchunk),
        out_shape=jax.ShapeDtypeStruct((rows_pad * nchunk, LANES), f32),
        grid_spec=pltpu.PrefetchScalarGridSpec(
            num_scalar_prefetch=1,
            grid=(ntile,),
            in_specs=[pl.BlockSpec((1, 1, t_tok * TOP_K), lambda i, zt: (i, 0, 0),
                                   memory_space=pltpu.SMEM),
                      pl.BlockSpec(memory_space=pl.ANY)],
            out_specs=pl.BlockSpec(memory_space=pl.ANY),
            scratch_shapes=[pltpu.VMEM((tm * nchunk, LANES), f32),
                            pltpu.SemaphoreType.DMA, pltpu.SemaphoreType.DMA]),
        compiler_params=_cparams(("arbitrary",)),
        name="moe_dispatch",
    )(ztile, pos3, hflat)


def _gmm_kernel(te_ref, used_ref, xs_ref, wgu_ref, bgu_ref, wd_ref, bd_ref, y_ref, *, ff, nchunk):
    @pl.when(pl.program_id(0) < used_ref[0])
    def _():
        tm = xs_ref.shape[0] // nchunk
        x = jnp.concatenate([xs_ref[pl.ds(c, tm, stride=nchunk), :] for c in range(nchunk)],
                            axis=1).astype(bf16)
        gu = _dot(x, wgu_ref[0]) + bgu_ref[0]
        g = jnp.minimum(gu[:, :ff], SWIGLU_LIMIT)
        u = jnp.clip(gu[:, ff:], -SWIGLU_LIMIT, SWIGLU_LIMIT)
        act = (u + 1.0) * (g * (1.0 / (1.0 + jnp.exp(-SWIGLU_ALPHA * g))))
        y = _dot(act.astype(bf16), wd_ref[0]) + bd_ref[0]
        for c in range(nchunk):
            y_ref[pl.ds(c, tm, stride=nchunk), :] = y[:, c * LANES:(c + 1) * LANES]

    @pl.when(pl.program_id(0) >= used_ref[0])
    def _():
        y_ref[...] = jnp.zeros(y_ref.shape, f32)


def _grouped_mlp(tile_expert, used, xs, wgu, bgu, wd, bdn, tm, nchunk):
    ne, d, ff2 = wgu.shape
    ntile = tile_expert.shape[0]
    rows = lambda j, te, used: (jnp.minimum(j, used[0] - 1), 0)
    rows_out = lambda j, te, used: (j, 0)
    exp3 = lambda j, te, used: (te[j], 0, 0)
    return pl.pallas_call(
        functools.partial(_gmm_kernel, ff=ff2 // 2, nchunk=nchunk),
        out_shape=jax.ShapeDtypeStruct(xs.shape, f32),
        grid_spec=pltpu.PrefetchScalarGridSpec(
            num_scalar_prefetch=2,
            grid=(ntile,),
            in_specs=[pl.BlockSpec((tm * nchunk, LANES), rows),
                      pl.BlockSpec((1, d, ff2), exp3), pl.BlockSpec((1, 1, ff2), exp3),
                      pl.BlockSpec((1, ff2 // 2, d), exp3), pl.BlockSpec((1, 1, d), exp3)],
            out_specs=pl.BlockSpec((tm * nchunk, LANES), rows_out)),
        compiler_params=_cparams(("arbitrary",)),
        name="moe_grouped_mlp",
    )(tile_expert, used, xs, wgu, bgu, wd, bdn)


def _combine_kernel(pos_ref, y_ref, gate_ref, x_ref, g2_ref, o_ref, buf_ref, sem, *, t_tok, nchunk):
    def issue(tt, _):
        for u in range(DMA_UNROLL):
            t = tt * DMA_UNROLL + u
            for k in range(TOP_K):
                p = pos_ref[0, 0, t * TOP_K + k]
                src = y_ref.at[pl.ds(pl.multiple_of(p * nchunk, nchunk), nchunk)]
                dst = buf_ref.at[pl.ds(pl.multiple_of((k * t_tok + t) * nchunk, nchunk), nchunk)]
                pltpu.make_async_copy(src, dst, sem).start()
        return 0

    lax.fori_loop(0, t_tok // DMA_UNROLL, issue, 0)
    pltpu.make_async_copy(y_ref.at[pl.ds(0, TOP_K * t_tok * nchunk)], buf_ref, sem).wait()
    gates = gate_ref[...]
    g2 = g2_ref[0]
    for c in range(nchunk):
        acc = None
        for k in range(TOP_K):
            rows = buf_ref[pl.ds(k * t_tok * nchunk + c, t_tok, stride=nchunk), :]
            term = gates[:, k:k + 1] * rows
            acc = term if acc is None else acc + term
        sl = slice(c * LANES, (c + 1) * LANES)
        o_ref[:, sl] = x_ref[:, sl] + g2[:, sl] * acc


def _combine(pos3, y, gate, x2, g2, seq, t_tok, nchunk):
    n, d = x2.shape
    per = seq // t_tok
    return pl.pallas_call(
        functools.partial(_combine_kernel, t_tok=t_tok, nchunk=nchunk),
        out_shape=jax.ShapeDtypeStruct((n, d), f32),
        grid=(n // t_tok,),
        in_specs=[pl.BlockSpec((1, 1, t_tok * TOP_K), lambda i: (i, 0, 0), memory_space=pltpu.SMEM),
                  pl.BlockSpec(memory_space=pl.ANY),
                  pl.BlockSpec((t_tok, LANES), lambda i: (i, 0)),
                  pl.BlockSpec((t_tok, d), lambda i: (i, 0)),
                  pl.BlockSpec((1, 1, d), lambda i: (i // per, 0, 0))],
        out_specs=pl.BlockSpec((t_tok, d), lambda i: (i, 0)),
        scratch_shapes=[pltpu.VMEM((TOP_K * t_tok * nchunk, LANES), f32), pltpu.SemaphoreType.DMA],
        compiler_params=_cparams(("arbitrary",)),
        name="moe_combine",
    )(pos3, y, gate, x2, g2)


def _moe_plan(idx, rank, cnt, ne, tm, ntile_max):
    counts = cnt[0, :ne].astype(jnp.int32)
    ntile_e = (counts + tm - 1) // tm
    tile_end = jnp.cumsum(ntile_e)
    tile_start = tile_end - ntile_e
    used = tile_end[-1]
    e_idx = idx[:, :TOP_K].astype(jnp.int32)
    pos = tile_start[e_idx] * tm + rank[:, :TOP_K].astype(jnp.int32)
    tiles = jnp.minimum(jnp.arange(ntile_max, dtype=jnp.int32), used - 1)
    tile_expert = jnp.searchsorted(tile_end, tiles, side="right").astype(jnp.int32)
    ztile = jnp.where(ntile_e > 0, tile_end - 1, -1)
    ztile = jnp.concatenate([ztile, used.reshape(1)]).astype(jnp.int32)
    return pos, tile_expert, used.reshape(1).astype(jnp.int32), ztile


def _lane_tile(v):
    return jnp.tile(v.astype(f32), LANES // v.shape[0]).reshape(1, LANES)


def kernel(x, c, positions, ada_w, ada_b, norm1_g, norm2_g, sb_w_qkv, sb_w_o, dsa_w_in,
           dsa_q_gain, dsa_k_gain, dsa_w_o, router_w, router_b, exp_w_gu, exp_b_gu,
           exp_w_down, exp_b_down):
    batch, seq, d = x.shape
    depth = ada_w.shape[0]
    n = batch * seq
    ne = router_w.shape[-1]
    ff = exp_w_down.shape[2]
    topk = min(TOPK_MAX, seq // 4)
    assert seq % KB == 0 and d % LANES == 0 and ne <= LANES

    mod = _modulation(c, ada_w, ada_b)
    x2 = x.reshape(n, d)

    ncols = dsa_w_in.shape[-1]
    ncols_pad = -(-ncols // LANES) * LANES
    half = ROPE_DIMS // 2
    inv = jnp.exp(-math.log(ROPE_THETA) * (2.0 * jnp.arange(half, dtype=f32) / ROPE_DIMS))
    l64 = jnp.arange(LANES) % HEAD_DIM
    inv_lane = jnp.where(l64 < ROPE_DIMS, inv[l64 % half], 0.0).reshape(1, LANES)
    bd = (jnp.arange(LANES)[:, None] // HEAD_DIM == jnp.arange(LANES)[None, :] // HEAD_DIM)
    bd = bd.astype(bf16)
    pos = positions.reshape(n, 1)

    nchunk = d // LANES
    ntile_max = (n * TOP_K) // MOE_TM + ne
    assert (n * TOP_K) % MOE_TM == 0 and n % MOE_TOK == 0 and seq % MOE_TOK == 0
    tri = (jnp.arange(POST_TM)[:, None] > jnp.arange(POST_TM)[None, :]).astype(bf16)
    wgu_all = _deinterleave(exp_w_gu.reshape(depth * ne, d, 2 * ff))

    for layer in range(depth):
        m6 = [mod[layer][:, k * d:(k + 1) * d].reshape(batch, 1, d) for k in range(6)]
        sh1, sc1, g1, sh2, sc2, g2 = m6
        n1 = norm1_g[layer].reshape(1, d)
        n2 = norm2_g[layer].reshape(1, d)
        j = layer // 2
        if layer % 2 == 0:
            qkv = _proj(x2, n1, sc1, sh1, sb_w_qkv[j].astype(bf16), seq)
            o = _stick_breaking(qkv, batch, seq, d)
            wo = sb_w_o[j].astype(bf16)
        else:
            w = dsa_w_in[j]
            kv0 = d
            qi0 = d + 2 * LANES
            ki0 = qi0 + IDX_HEADS * IDX_DIM
            w = jnp.concatenate([w[:, :kv0], w[:, qi0:ki0], w[:, kv0:qi0], w[:, ki0:]], axis=1)
            w = jnp.pad(w, ((0, 0), (0, ncols_pad - ncols))).astype(bf16)
            proj, wi = _dsa_proj(x2, n1, sc1, sh1, w, pos, inv_lane, _lane_tile(dsa_q_gain[j]),
                                 _lane_tile(dsa_k_gain[j]), bd, seq)
            o = _dsa_attention(proj, wi, batch, seq, d, topk)
            wo = dsa_w_o[j].astype(bf16)
        rw = jnp.pad(router_w[layer], ((0, 0), (0, LANES - ne)))
        rb = jnp.pad(router_b[layer], (0, LANES - ne), constant_values=-jnp.inf).reshape(1, LANES)
        x2, hflat, idx, gate, rank, cnt = _post_attention(o, wo, x2, g1, n2, sc2, sh2, rw, rb,
                                                          tri, seq, POST_TM)
        spos, tile_expert, used, ztile = _moe_plan(idx, rank, cnt, ne, MOE_TM, ntile_max)
        pos3 = spos.reshape(n // MOE_TOK, 1, MOE_TOK * TOP_K)
        xs = _dispatch(hflat, pos3, ztile, ntile_max * MOE_TM, MOE_TM, MOE_TOK, nchunk)
        bgu = exp_b_gu[layer]
        bgu = jnp.concatenate([bgu[..., 0::2], bgu[..., 1::2]], axis=-1).reshape(ne, 1, 2 * ff)
        ys = _grouped_mlp(tile_expert, used, xs, wgu_all[layer * ne:(layer + 1) * ne], bgu,
                          exp_w_down[layer].astype(bf16), exp_b_down[layer].reshape(ne, 1, d),
                          MOE_TM, nchunk)
        x2 = _combine(pos3, ys, gate, x2, g2, seq, MOE_TOK, nchunk)
    return x2.reshape(batch, seq, d)
```

```python
import functools
import math

import jax
import jax.numpy as jnp
from jax import lax
from jax.experimental import pallas as pl
from jax.experimental.pallas import tpu as pltpu

HEAD_DIM = 64
DSA_KV_HEADS = 2
IDX_HEADS = 8
IDX_DIM = 64
CHUNK_SHIFT = 6
TOPK_MAX = 256
TOP_K = 4
ROPE_THETA = 500000.0
ROPE_DIMS = HEAD_DIM // 4
SWIGLU_ALPHA = 1.702
SWIGLU_LIMIT = 7.0
EPS = 1e-6

LANES = 128
QB = 128
KB = 256
DQ = 256
NEG = -1e30
INT_MIN = -(2 ** 31)
KEY_NEG_INF = -2139095041
VMEM_LIMIT = 48 * 1024 * 1024
POST_TM = 512
MOE_TM = 512
MOE_TOK = 256
DMA_UNROLL = 8
SB_WINDOW = 3
SB_CUTOFF = -110.0

f32 = jnp.float32
bf16 = jnp.bfloat16


def _cparams(sem):
    return pltpu.CompilerParams(dimension_semantics=sem, vmem_limit_bytes=VMEM_LIMIT)


def _dot(a, b):
    return jnp.dot(a, b, preferred_element_type=f32)


def _dot_nt(a, b):
    return lax.dot_general(a, b, (((1,), (1,)), ((), ())), preferred_element_type=f32)


def _dot_split(x, m01, passes):
    acc = None
    r = x
    for p in range(passes):
        t = r.astype(bf16)
        d = _dot(t, m01)
        acc = d if acc is None else acc + d
        if p + 1 < passes:
            r = r - t.astype(f32)
    return acc


def _norm_mod(x, g, sc, sh):
    ms = jnp.mean(x * x, axis=-1, keepdims=True)
    return (x * lax.rsqrt(ms + EPS) * g) * (1.0 + sc) + sh


def _mod_kernel(c_ref, w_ref, b_ref, o_ref):
    c = c_ref[...]
    cs = c * (1.0 / (1.0 + jnp.exp(-c)))
    o_ref[0] = jnp.dot(cs, w_ref[0], preferred_element_type=f32,
                       precision=lax.Precision.HIGHEST) + b_ref[0]


def _modulation(c, ada_w, ada_b):
    depth, d, n6 = ada_w.shape
    b = c.shape[0]
    tn = 1024
    return pl.pallas_call(
        _mod_kernel,
        out_shape=jax.ShapeDtypeStruct((depth, b, n6), f32),
        grid=(depth, n6 // tn),
        in_specs=[pl.BlockSpec((b, d), lambda l, j: (0, 0)),
                  pl.BlockSpec((1, d, tn), lambda l, j: (l, 0, j)),
                  pl.BlockSpec((1, 1, tn), lambda l, j: (l, 0, j))],
        out_specs=pl.BlockSpec((1, b, tn), lambda l, j: (l, 0, j)),
        compiler_params=_cparams(("arbitrary", "arbitrary")),
        name="adaln_mod",
    )(c, ada_w, ada_b.reshape(depth, 1, n6))


def _proj_kernel(x_ref, g_ref, sc_ref, sh_ref, w_ref, o_ref):
    h = _norm_mod(x_ref[...], g_ref[...], sc_ref[0], sh_ref[0])
    o_ref[...] = _dot(h.astype(bf16), w_ref[...]).astype(o_ref.dtype)


def _proj(x2, g, sc, sh, w, seq, tm=512):
    n, d = x2.shape
    nc = w.shape[1]
    per = seq // tm
    return pl.pallas_call(
        _proj_kernel,
        out_shape=jax.ShapeDtypeStruct((n, nc), bf16),
        grid=(n // tm,),
        in_specs=[pl.BlockSpec((tm, d), lambda i: (i, 0)),
                  pl.BlockSpec((1, d), lambda i: (0, 0)),
                  pl.BlockSpec((1, 1, d), lambda i: (i // per, 0, 0)),
                  pl.BlockSpec((1, 1, d), lambda i: (i // per, 0, 0)),
                  pl.BlockSpec((d, nc), lambda i: (0, 0))],
        out_specs=pl.BlockSpec((tm, nc), lambda i: (i, 0)),
        compiler_params=_cparams(("arbitrary",)),
        name="sb_qkv_proj",
    )(x2, g, sc, sh, w)


def _sb_kernel(q_ref, k_ref, v_ref, o_ref, k0_ref, k1_ref):
    seq = q_ref.shape[0]
    nqb = seq // QB
    lane = lax.broadcasted_iota(jnp.int32, (1, LANES), 1)
    kk = k_ref[...]
    zero = jnp.zeros_like(kk)
    k0_ref[...] = jnp.where(lane < HEAD_DIM, kk, zero)
    k1_ref[...] = jnp.where(lane >= HEAD_DIM, kk, zero)
    row = lax.broadcasted_iota(jnp.int32, (QB, QB), 0)
    col = lax.broadcasted_iota(jnp.int32, (QB, QB), 1)
    before = col < row
    tri = jnp.where(row > col, 1.0, 0.0).astype(bf16)

    def window(q, kh_ref, k0, nb, c_in, diag):
        z = _dot_nt(q, kh_ref[pl.ds(k0, nb * QB), :])
        l1p = jnp.log(1.0 + jnp.exp(-jnp.abs(z)))
        lm = jnp.minimum(-z, 0.0) - l1p
        ls = jnp.minimum(z, 0.0) - l1p
        lms = [lm[:, s * QB:(s + 1) * QB] for s in range(nb)]
        if diag:
            lms[-1] = jnp.where(before, lms[-1], 0.0)
        c = c_in
        probs = [None] * nb
        for s in reversed(range(nb)):
            tail = _dot_split(lms[s], tri, 3) + c
            a = jnp.exp(ls[:, s * QB:(s + 1) * QB] + tail)
            if diag and s == nb - 1:
                a = jnp.where(before, a, 0.0)
            probs[s] = a.astype(bf16)
            c = c + jnp.sum(lms[s], axis=1, keepdims=True)
        a_all = probs[0] if nb == 1 else jnp.concatenate(probs, axis=1)
        return c, _dot(a_all, v_ref[pl.ds(k0, nb * QB), :])

    def qblock(i, nb, static):
        q0 = i * QB if static else pl.multiple_of(i * QB, QB)
        k0 = q0 - (nb - 1) * QB
        if not static:
            k0 = pl.multiple_of(k0, QB)
        q = q_ref[pl.ds(q0, QB), :] * (HEAD_DIM ** -0.5)
        zc = jnp.zeros((QB, 1), f32)
        c0, a0 = window(q, k0_ref, k0, nb, zc, True)
        c1, a1 = window(q, k1_ref, k0, nb, zc, True)
        if not static:
            def cond(st):
                jb, c0, _, c1, _ = st
                live = jnp.max(jnp.maximum(c0, c1)) > SB_CUTOFF
                return jnp.logical_and(jb >= 0, live)

            def body(st):
                jb, c0, a0, c1, a1 = st
                kb = pl.multiple_of(jb * QB, QB)
                c0, d0 = window(q, k0_ref, kb, 1, c0, False)
                c1, d1 = window(q, k1_ref, kb, 1, c1, False)
                return jb - 1, c0, a0 + d0, c1, a1 + d1

            _, c0, a0, c1, a1 = lax.while_loop(cond, body, (i - nb, c0, a0, c1, a1))
        o_ref[pl.ds(q0, QB), :] = jnp.where(lane < HEAD_DIM, a0, a1).astype(o_ref.dtype)

    for i in range(min(SB_WINDOW, nqb)):
        qblock(i, i + 1, True)

    def loop(i, _):
        qblock(i, SB_WINDOW, False)
        return 0

    lax.fori_loop(SB_WINDOW, nqb, loop, 0)


def _stick_breaking(qkv, batch, seq, d):
    npair = d // LANES
    return pl.pallas_call(
        _sb_kernel,
        out_shape=jax.ShapeDtypeStruct((batch * seq, d), bf16),
        grid=(batch, npair),
        in_specs=[pl.BlockSpec((seq, LANES), lambda b, p: (b, p)),
                  pl.BlockSpec((seq, LANES), lambda b, p: (b, npair + p)),
                  pl.BlockSpec((seq, LANES), lambda b, p: (b, 2 * npair + p))],
        out_specs=pl.BlockSpec((seq, LANES), lambda b, p: (b, p)),
        scratch_shapes=[pltpu.VMEM((seq, LANES), bf16), pltpu.VMEM((seq, LANES), bf16)],
        compiler_params=_cparams(("arbitrary", "arbitrary")),
        name="stick_breaking_attn",
    )(qkv, qkv, qkv)


def _dsa_proj_kernel(x_ref, g_ref, sc_ref, sh_ref, w_ref, pos_ref, inv_ref, qg_ref, kg_ref,
                     bd_ref, o_ref, wi_ref, *, d):
    h = _norm_mod(x_ref[...], g_ref[...], sc_ref[0], sh_ref[0])
    p = _dot(h.astype(bf16), w_ref[...])
    lane = lax.broadcasted_iota(jnp.int32, (1, LANES), 1)
    ang = pos_ref[...].astype(f32) * inv_ref[...]
    cos_t = jnp.cos(ang)
    sin_t = jnp.sin(ang)
    upper = (lane % ROPE_DIMS) >= (ROPE_DIMS // 2)
    s_up = jnp.where(upper, sin_t, 0.0)
    s_lo = jnp.where(upper, 0.0, -sin_t)
    half = ROPE_DIMS // 2

    def rope(y):
        return y * cos_t + pltpu.roll(y, half, 1) * s_up + pltpu.roll(y, LANES - half, 1) * s_lo

    def headnorm(y, gain):
        ms = _dot_split(y * y, bd_ref[...], 2) * (1.0 / HEAD_DIM)
        return y * lax.rsqrt(ms + EPS) * gain

    nq = d // LANES
    for c in range(nq):
        y = p[:, c * LANES:(c + 1) * LANES]
        o_ref[:, c * LANES:(c + 1) * LANES] = rope(headnorm(y, qg_ref[...])).astype(bf16)
    c0 = nq
    for c in range(c0, c0 + IDX_HEADS * IDX_DIM // LANES):
        o_ref[:, c * LANES:(c + 1) * LANES] = rope(p[:, c * LANES:(c + 1) * LANES]).astype(bf16)
    c0 += IDX_HEADS * IDX_DIM // LANES
    y = p[:, c0 * LANES:(c0 + 1) * LANES]
    o_ref[:, c0 * LANES:(c0 + 1) * LANES] = rope(headnorm(y, kg_ref[...])).astype(bf16)
    c0 += 1
    o_ref[:, c0 * LANES:(c0 + 1) * LANES] = p[:, c0 * LANES:(c0 + 1) * LANES].astype(bf16)
    c0 += 1
    y = p[:, c0 * LANES:(c0 + 1) * LANES]
    o_ref[:, c0 * LANES:(c0 + 1) * LANES] = jnp.where(lane < IDX_DIM, rope(y), 0.0).astype(bf16)
    wi = pltpu.roll(y, LANES - IDX_DIM, 1) * (IDX_HEADS ** -0.5)
    wi_ref[...] = jnp.where(lane < IDX_HEADS, wi, 0.0)


def _dsa_proj(x2, g, sc, sh, w, pos, inv_lane, qg, kg, bd, seq, tm=512):
    n, d = x2.shape
    nc = w.shape[1]
    per = seq // tm
    return pl.pallas_call(
        functools.partial(_dsa_proj_kernel, d=d),
        out_shape=(jax.ShapeDtypeStruct((n, nc), bf16), jax.ShapeDtypeStruct((n, LANES), f32)),
        grid=(n // tm,),
        in_specs=[pl.BlockSpec((tm, d), lambda i: (i, 0)),
                  pl.BlockSpec((1, d), lambda i: (0, 0)),
                  pl.BlockSpec((1, 1, d), lambda i: (i // per, 0, 0)),
                  pl.BlockSpec((1, 1, d), lambda i: (i // per, 0, 0)),
                  pl.BlockSpec((d, nc), lambda i: (0, 0)),
                  pl.BlockSpec((tm, 1), lambda i: (i, 0)),
                  pl.BlockSpec((1, LANES), lambda i: (0, 0)),
                  pl.BlockSpec((1, LANES), lambda i: (0, 0)),
                  pl.BlockSpec((1, LANES), lambda i: (0, 0)),
                  pl.BlockSpec((LANES, LANES), lambda i: (0, 0))],
        out_specs=(pl.BlockSpec((tm, nc), lambda i: (i, 0)),
                   pl.BlockSpec((tm, LANES), lambda i: (i, 0))),
        compiler_params=_cparams(("arbitrary",)),
        name="dsa_in_proj",
    )(x2, g, sc, sh, w, pos, inv_lane, qg, kg, bd)


def _dsa_kernel(q_ref, k_ref, v_ref, qi_ref, ki_ref, wi_ref, o_ref,
                kk_ref, vv_ref, ki2_ref, keys_ref, bias_ref, m_ref, acc_ref, thr_ref, need_ref,
                *, topk):
    seq = k_ref.shape[0]
    nchunk = q_ref.shape[1] // LANES
    group = (2 * nchunk) // DSA_KV_HEADS
    i = pl.program_id(1)
    lane = lax.broadcasted_iota(jnp.int32, (1, LANES), 1)
    lo = lane < HEAD_DIM

    @pl.when(i == 0)
    def _():
        k = k_ref[...]
        v = v_ref[...]
        ki = ki_ref[...]
        kr = pltpu.roll(k.astype(f32), HEAD_DIM, 1).astype(bf16)
        vr = pltpu.roll(v.astype(f32), HEAD_DIM, 1).astype(bf16)
        zero = jnp.zeros_like(k)
        kk_ref[0] = jnp.where(lo, k, zero)
        kk_ref[1] = jnp.where(lo, zero, kr)
        kk_ref[2] = jnp.where(lo, kr, zero)
        kk_ref[3] = jnp.where(lo, zero, k)
        lane_full = lax.broadcasted_iota(jnp.int32, k.shape, 1)
        oh = [jnp.where(lane_full < HEAD_DIM, 1.0, 0.0).astype(bf16),
              jnp.where(lane_full < HEAD_DIM, 0.0, 1.0).astype(bf16)]
        vv_ref[0] = jnp.concatenate([jnp.where(lo, v, zero), oh[0]], axis=1)
        vv_ref[1] = jnp.concatenate([jnp.where(lo, zero, vr), oh[1]], axis=1)
        vv_ref[2] = jnp.concatenate([jnp.where(lo, vr, zero), oh[0]], axis=1)
        vv_ref[3] = jnp.concatenate([jnp.where(lo, zero, v), oh[1]], axis=1)
        ki2_ref[0] = ki
        ki2_ref[1] = pltpu.roll(ki.astype(f32), HEAD_DIM, 1).astype(bf16)

    nkb = i + 1
    qrow = lax.broadcasted_iota(jnp.int32, (DQ, KB), 0) + i * DQ
    kcol = lax.broadcasted_iota(jnp.int32, (DQ, KB), 1)

    wi = wi_ref[...]
    wcols = [wi[:, hd:hd + 1] for hd in range(IDX_HEADS)]

    def score_block(jb, _):
        k0 = pl.multiple_of(jb * KB, KB)
        score = jnp.zeros((DQ, KB), f32)
        for hd in range(IDX_HEADS):
            qc = qi_ref[:, (hd // 2) * LANES:(hd // 2 + 1) * LANES]
            kb = ki2_ref[hd % 2, pl.ds(k0, KB), :]
            rel = jnp.maximum(_dot_nt(qc, kb), 0.0) * (IDX_DIM ** -0.5)
            score = score + rel * wcols[hd]
        score = score + 0.0
        bits = pltpu.bitcast(score, jnp.int32)
        key = bits ^ ((bits >> 31) & 0x7FFFFFFF)
        adm = ((kcol + k0) >> CHUNK_SHIFT) <= (qrow >> CHUNK_SHIFT)
        keys_ref[jb] = jnp.where(adm, key, INT_MIN)
        return 0

    lax.fori_loop(0, nkb, score_block, 0)

    kf = float(topk)
    ones_l = jnp.ones((LANES, LANES), bf16)

    def search(nk):
        def count(r, pred):
            part = None
            for jb in range(nk):
                for hl in range(KB // LANES):
                    key = keys_ref[jb, r * QB:(r + 1) * QB, hl * LANES:(hl + 1) * LANES]
                    hit = jnp.where(pred(key), 1.0, 0.0)
                    part = hit if part is None else part + hit
            return _dot(part.astype(bf16), ones_l)

        groups = range(DQ // QB)
        t0 = tuple(jnp.where(count(r, lambda key: key >= 0) >= kf, 0, INT_MIN).astype(jnp.int32)
                   for r in groups)

        def bit_step(s, ts):
            bit = jnp.left_shift(jnp.int32(1), 30 - s)
            out = []
            for r in groups:
                cand = ts[r] + bit
                enough = count(r, lambda key, cand=cand: key >= cand) >= kf
                out.append(jnp.where(enough, cand, ts[r]))
            return tuple(out)

        ts = lax.fori_loop(0, 31, bit_step, t0)
        for r in groups:
            thr_ref[r * QB:(r + 1) * QB, :] = ts[r]
            need_ref[r * QB:(r + 1) * QB, :] = kf - count(r, lambda key, t=ts[r]: key > t)

    for nk in range(1, seq // KB + 1):
        @pl.when(nkb == nk)
        def _(nk=nk):
            search(nk)

    thr = jnp.concatenate([thr_ref[...]] * (KB // LANES), axis=1)
    need = jnp.concatenate([need_ref[...]] * (KB // LANES), axis=1)

    r2 = lax.broadcasted_iota(jnp.int32, (KB, KB), 0)
    c2 = lax.broadcasted_iota(jnp.int32, (KB, KB), 1)
    tri = jnp.where(r2 < c2, 1.0, 0.0).astype(bf16)
    ones_k = jnp.ones((KB, KB), bf16)

    def bias_block(jb, carry):
        key = keys_ref[jb]
        eq = key == thr
        eqb = jnp.where(eq, 1.0, 0.0).astype(bf16)
        rank = _dot(eqb, tri) + carry
        sel = (key > thr) | (eq & (rank < need))
        sel = sel & (key > KEY_NEG_INF)
        bias_ref[jb] = jnp.where(sel, 0.0, NEG)
        return carry + _dot(eqb, ones_k)

    lax.fori_loop(0, nkb, bias_block, jnp.zeros((DQ, KB), f32))

    def scores(c, hf, k0, bias):
        q = q_ref[:, c * LANES:(c + 1) * LANES] * (HEAD_DIM ** -0.5)
        g = (2 * c) // group
        return _dot_nt(q, kk_ref[2 * g + hf, pl.ds(k0, KB), :]) + bias

    m_ref[...] = jnp.full(m_ref.shape, NEG, f32)

    def max_block(jb, _):
        k0 = pl.multiple_of(jb * KB, KB)
        bias = bias_ref[jb]
        for c in range(nchunk):
            for hf in range(2):
                s = scores(c, hf, k0, bias)
                fold = jnp.maximum(s[:, :LANES], s[:, LANES:])
                m_ref[2 * c + hf] = jnp.maximum(m_ref[2 * c + hf], fold)
        return 0

    lax.fori_loop(0, nkb, max_block, 0)
    for h in range(2 * nchunk):
        m = jnp.max(m_ref[h], axis=1, keepdims=True)
        m_ref[h] = jnp.broadcast_to(m, (DQ, LANES))

    acc_ref[...] = jnp.zeros(acc_ref.shape, f32)

    def attn_block(jb, _):
        k0 = pl.multiple_of(jb * KB, KB)
        bias = bias_ref[jb]
        for c in range(nchunk):
            g = (2 * c) // group
            ps = []
            for hf in range(2):
                m = m_ref[2 * c + hf]
                s = scores(c, hf, k0, bias)
                ps.append(jnp.exp(s - jnp.concatenate([m, m], axis=1)).astype(bf16))
            p2 = jnp.concatenate(ps, axis=1)
            v2 = jnp.concatenate([vv_ref[2 * g, pl.ds(k0, KB), :],
                                  vv_ref[2 * g + 1, pl.ds(k0, KB), :]], axis=0)
            acc_ref[c] += _dot(p2, v2)
        return 0

    lax.fori_loop(0, nkb, attn_block, 0)

    for c in range(nchunk):
        r = acc_ref[c]
        o_ref[:, c * LANES:(c + 1) * LANES] = (r[:, :LANES] / r[:, LANES:]).astype(o_ref.dtype)


def group_even(d):
    return ((d // HEAD_DIM) // DSA_KV_HEADS) % 2 == 0


def _dsa_attention(proj, wi, batch, seq, d, topk):
    nqb = seq // DQ
    nchunk = d // LANES
    qiw = IDX_HEADS * IDX_DIM
    assert d % qiw == 0 and group_even(d) and DQ == KB
    qicol = d // qiw
    kcol = nchunk + qiw // LANES
    vcol = kcol + 1
    kicol = kcol + 2
    return pl.pallas_call(
        functools.partial(_dsa_kernel, topk=topk),
        out_shape=jax.ShapeDtypeStruct((batch * seq, d), bf16),
        grid=(batch, nqb),
        in_specs=[pl.BlockSpec((DQ, d), lambda b, i: (b * nqb + i, 0)),
                  pl.BlockSpec((seq, LANES), lambda b, i: (b, kcol)),
                  pl.BlockSpec((seq, LANES), lambda b, i: (b, vcol)),
                  pl.BlockSpec((DQ, IDX_HEADS * IDX_DIM), lambda b, i: (b * nqb + i, qicol)),
                  pl.BlockSpec((seq, LANES), lambda b, i: (b, kicol)),
                  pl.BlockSpec((DQ, LANES), lambda b, i: (b * nqb + i, 0))],
        out_specs=pl.BlockSpec((DQ, d), lambda b, i: (b * nqb + i, 0)),
        scratch_shapes=[pltpu.VMEM((4, seq, LANES), bf16),
                        pltpu.VMEM((4, seq, 2 * LANES), bf16),
                        pltpu.VMEM((2, seq, LANES), bf16),
                        pltpu.VMEM((seq // KB, DQ, KB), jnp.int32),
                        pltpu.VMEM((seq // KB, DQ, KB), f32),
                        pltpu.VMEM((2 * nchunk, DQ, LANES), f32),
                        pltpu.VMEM((nchunk, DQ, 2 * LANES), f32),
                        pltpu.VMEM((DQ, LANES), jnp.int32),
                        pltpu.VMEM((DQ, LANES), f32)],
        compiler_params=_cparams(("arbitrary", "arbitrary")),
        name="dsa_sparse_attn",
    )(proj, proj, proj, proj, proj, wi)


def _post_kernel(o_ref, wo_ref, x_ref, g1_ref, n2_ref, sc_ref, sh_ref, rw_ref, rb_ref, tri_ref,
                 xo_ref, hf_ref, idx_ref, gate_ref, rank_ref, cnt_ref, run_ref):
    @pl.when(pl.program_id(0) == 0)
    def _():
        run_ref[...] = jnp.zeros(run_ref.shape, f32)

    y = _dot(o_ref[...], wo_ref[...])
    x = x_ref[...] + g1_ref[0] * y
    xo_ref[...] = x
    h = _norm_mod(x, n2_ref[...], sc_ref[0], sh_ref[0])
    tm, d = h.shape
    nchunk = d // LANES
    for c in range(nchunk):
        hf_ref[pl.ds(c, tm, stride=nchunk), :] = h[:, c * LANES:(c + 1) * LANES]
    logits = jnp.dot(h, rw_ref[...], preferred_element_type=f32,
                     precision=lax.Precision.HIGHEST) + rb_ref[...]
    lane = lax.broadcasted_iota(jnp.int32, logits.shape, 1).astype(f32)
    work = logits
    vals, idxs, hits = [], [], []
    for _ in range(TOP_K):
        m = jnp.max(work, axis=1, keepdims=True)
        idx = jnp.min(jnp.where(work == m, lane, float(LANES)), axis=1, keepdims=True)
        hit = lane == idx
        vals.append(m)
        idxs.append(idx)
        hits.append(hit)
        work = jnp.where(hit, -jnp.inf, work)
    es = [jnp.exp(v - vals[0]) for v in vals]
    inv = 1.0 / (es[0] + es[1] + es[2] + es[3])
    multi = jnp.zeros_like(logits)
    for hit in hits:
        multi = multi + jnp.where(hit, 1.0, 0.0)
    before = _dot(tri_ref[...], multi.astype(bf16)) + run_ref[...]
    idx_o = jnp.zeros_like(logits)
    gate_o = jnp.zeros_like(logits)
    rank_o = jnp.zeros_like(logits)
    for k in range(TOP_K):
        slot = lane == float(k)
        rank = jnp.sum(jnp.where(hits[k], before, 0.0), axis=1, keepdims=True)
        idx_o = jnp.where(slot, idxs[k], idx_o)
        gate_o = jnp.where(slot, es[k] * inv, gate_o)
        rank_o = jnp.where(slot, rank, rank_o)
    idx_ref[...] = idx_o
    gate_ref[...] = gate_o
    rank_ref[...] = rank_o
    run_ref[...] += jnp.sum(multi, axis=0, keepdims=True)
    cnt_ref[...] = run_ref[...]


def _post_attention(o, wo, x2, g1, n2, sc2, sh2, rw, rb, tri, seq, tm):
    n, d = x2.shape
    per = seq // tm
    nchunk = d // LANES
    row = lambda i: (i, 0)
    fix = lambda i: (0, 0)
    bat = lambda i: (i // per, 0, 0)
    lane_out = jax.ShapeDtypeStruct((n, LANES), f32)
    return pl.pallas_call(
        _post_kernel,
        out_shape=(jax.ShapeDtypeStruct((n, d), f32), jax.ShapeDtypeStruct((n * nchunk, LANES), f32),
                   lane_out, lane_out, lane_out, jax.ShapeDtypeStruct((1, LANES), f32)),
        grid=(n // tm,),
        in_specs=[pl.BlockSpec((tm, d), row), pl.BlockSpec((d, d), fix),
                  pl.BlockSpec((tm, d), row), pl.BlockSpec((1, 1, d), bat),
                  pl.BlockSpec((1, d), fix), pl.BlockSpec((1, 1, d), bat),
                  pl.BlockSpec((1, 1, d), bat), pl.BlockSpec((d, LANES), fix),
                  pl.BlockSpec((1, LANES), fix), pl.BlockSpec((tm, tm), fix)],
        out_specs=(pl.BlockSpec((tm, d), row), pl.BlockSpec((tm * nchunk, LANES), row),
                   pl.BlockSpec((tm, LANES), row), pl.BlockSpec((tm, LANES), row),
                   pl.BlockSpec((tm, LANES), row), pl.BlockSpec((1, LANES), fix)),
        scratch_shapes=[pltpu.VMEM((1, LANES), f32)],
        compiler_params=_cparams(("arbitrary",)),
        name="attn_out_norm_router",
    )(o, wo, x2, g1, n2, sc2, sh2, rw, rb, tri)


def _deint_kernel(w_ref, p_ref, o_ref):
    ff = o_ref.shape[2] // 2
    wide = 2 * LANES
    for b in range(o_ref.shape[2] // wide):
        x = w_ref[0, :, b * wide:(b + 1) * wide].astype(bf16)
        r = _dot(x, p_ref[...]).astype(bf16)
        o_ref[0, :, b * LANES:(b + 1) * LANES] = r[:, :LANES]
        o_ref[0, :, ff + b * LANES:ff + (b + 1) * LANES] = r[:, LANES:]


def _deinterleave(w):
    ne, d, ff2 = w.shape
    wide = 2 * LANES
    src = jnp.arange(wide)
    perm = jnp.where(src < LANES, 2 * src, 2 * (src - LANES) + 1)
    p = (jnp.arange(wide)[:, None] == perm[None, :]).astype(bf16)
    return pl.pallas_call(
        _deint_kernel,
        out_shape=jax.ShapeDtypeStruct((ne, d, ff2), bf16),
        grid=(ne,),
        in_specs=[pl.BlockSpec((1, d, ff2), lambda e: (e, 0, 0)),
                  pl.BlockSpec((wide, wide), lambda e: (0, 0))],
        out_specs=pl.BlockSpec((1, d, ff2), lambda e: (e, 0, 0)),
        compiler_params=_cparams(("arbitrary",)),
        name="expert_weight_prep",
    )(w, p)


def _dispatch_kernel(zt_ref, pos_ref, hf_ref, xs_ref, zbuf_ref, sem, zsem, *, t_tok, tm, nchunk):
    i = pl.program_id(0)
    ne = zt_ref.shape[0] - 1
    ntile = xs_ref.shape[0] // (tm * nchunk)

    def zero_tile(e):
        return zt_ref[e] if e < ne else zt_ref[ne] + (e - ne)

    def zero_copy(e):
        start = pl.multiple_of(zero_tile(e) * (tm * nchunk), tm * nchunk)
        return pltpu.make_async_copy(zbuf_ref, xs_ref.at[pl.ds(start, tm * nchunk)], zsem)

    def zero_wanted(e):
        return zero_tile(e) >= 0 if e < ne else zero_tile(e) < ntile

    @pl.when(i == 0)
    def _():
        zbuf_ref[...] = jnp.zeros(zbuf_ref.shape, f32)
        for e in range(2 * ne):
            @pl.when(zero_wanted(e))
            def _():
                zero_copy(e).start()
        for e in range(2 * ne):
            @pl.when(zero_wanted(e))
            def _():
                zero_copy(e).wait()

    def issue(tt, _):
        for u in range(DMA_UNROLL):
            t = tt * DMA_UNROLL + u
            src = hf_ref.at[pl.ds(pl.multiple_of(t * nchunk, nchunk), nchunk)]
            for k in range(TOP_K):
                p = pos_ref[0, 0, t * TOP_K + k]
                dst = xs_ref.at[pl.ds(pl.multiple_of(p * nchunk, nchunk), nchunk)]
                pltpu.make_async_copy(src, dst, sem).start()
        return 0

    lax.fori_loop(0, t_tok // DMA_UNROLL, issue, 0)
    for k in range(TOP_K):
        pltpu.make_async_copy(hf_ref, xs_ref.at[pl.ds(0, t_tok * nchunk)], sem).wait()


def _dispatch(hflat, pos3, ztile, rows_pad, tm, t_tok, nchunk):
    ntile = pos3.shape[0]
    return pl.pallas_call(
        functools.partial(_dispatch_kernel, t_tok=t_tok, tm=tm, nchunk=nchunk),
        out_shape=jax.ShapeDtypeStruct((rows_pad * nchunk, LANES), f32),
        grid_spec=pltpu.PrefetchScalarGridSpec(
            num_scalar_prefetch=1,
            grid=(ntile,),
            in_specs=[pl.BlockSpec((1, 1, t_tok * TOP_K), lambda i, zt: (i, 0, 0),
                                   memory_space=pltpu.SMEM),
                      pl.BlockSpec((t_tok * nchunk, LANES), lambda i, zt: (i, 0))],
            out_specs=pl.BlockSpec(memory_space=pl.ANY),
            scratch_shapes=[pltpu.VMEM((tm * nchunk, LANES), f32),
                            pltpu.SemaphoreType.DMA, pltpu.SemaphoreType.DMA]),
        compiler_params=_cparams(("arbitrary",)),
        name="moe_dispatch",
    )(ztile, pos3, hflat)


def _gmm_kernel(te_ref, used_ref, xs_ref, wgu_ref, bgu_ref, wd_ref, bd_ref, y_ref, *, ff, nchunk):
    @pl.when(pl.program_id(0) < used_ref[0])
    def _():
        tm = xs_ref.shape[0] // nchunk
        x = jnp.concatenate([xs_ref[pl.ds(c, tm, stride=nchunk), :] for c in range(nchunk)],
                            axis=1).astype(bf16)
        gu = _dot(x, wgu_ref[0]) + bgu_ref[0]
        g = jnp.minimum(gu[:, :ff], SWIGLU_LIMIT)
        u = jnp.clip(gu[:, ff:], -SWIGLU_LIMIT, SWIGLU_LIMIT)
        act = (u + 1.0) * (g * (1.0 / (1.0 + jnp.exp(-SWIGLU_ALPHA * g))))
        y = _dot(act.astype(bf16), wd_ref[0]) + bd_ref[0]
        for c in range(nchunk):
            y_ref[pl.ds(c, tm, stride=nchunk), :] = y[:, c * LANES:(c + 1) * LANES]

    @pl.when(pl.program_id(0) >= used_ref[0])
    def _():
        y_ref[...] = jnp.zeros(y_ref.shape, f32)


def _grouped_mlp(tile_expert, used, xs, wgu, bgu, wd, bdn, tm, nchunk):
    ne, d, ff2 = wgu.shape
    ntile = tile_expert.shape[0]
    rows = lambda j, te, used: (jnp.minimum(j, used[0] - 1), 0)
    rows_out = lambda j, te, used: (j, 0)
    exp3 = lambda j, te, used: (te[j], 0, 0)
    return pl.pallas_call(
        functools.partial(_gmm_kernel, ff=ff2 // 2, nchunk=nchunk),
        out_shape=jax.ShapeDtypeStruct(xs.shape, f32),
        grid_spec=pltpu.PrefetchScalarGridSpec(
            num_scalar_prefetch=2,
            grid=(ntile,),
            in_specs=[pl.BlockSpec((tm * nchunk, LANES), rows),
                      pl.BlockSpec((1, d, ff2), exp3), pl.BlockSpec((1, 1, ff2), exp3),
                      pl.BlockSpec((1, ff2 // 2, d), exp3), pl.BlockSpec((1, 1, d), exp3)],
            out_specs=pl.BlockSpec((tm * nchunk, LANES), rows_out)),
        compiler_params=_cparams(("arbitrary",)),
        name="moe_grouped_mlp",
    )(tile_expert, used, xs, wgu, bgu, wd, bdn)


def _combine_kernel(pos_ref, y_ref, gate_ref, x_ref, g2_ref, o_ref, buf_ref, sem, *, t_tok, nchunk):
    def issue(tt, _):
        for u in range(DMA_UNROLL):
            t = tt * DMA_UNROLL + u
            for k in range(TOP_K):
                p = pos_ref[0, 0, t * TOP_K + k]
                src = y_ref.at[pl.ds(pl.multiple_of(p * nchunk, nchunk), nchunk)]
                dst = buf_ref.at[pl.ds(pl.multiple_of((k * t_tok + t) * nchunk, nchunk), nchunk)]
                pltpu.make_async_copy(src, dst, sem).start()
        return 0

    lax.fori_loop(0, t_tok // DMA_UNROLL, issue, 0)
    pltpu.make_async_copy(y_ref.at[pl.ds(0, TOP_K * t_tok * nchunk)], buf_ref, sem).wait()
    gates = gate_ref[...]
    g2 = g2_ref[0]
    for c in range(nchunk):
        acc = None
        for k in range(TOP_K):
            rows = buf_ref[pl.ds(k * t_tok * nchunk + c, t_tok, stride=nchunk), :]
            term = gates[:, k:k + 1] * rows
            acc = term if acc is None else acc + term
        sl = slice(c * LANES, (c + 1) * LANES)
        o_ref[:, sl] = x_ref[:, sl] + g2[:, sl] * acc


def _combine(pos3, y, gate, x2, g2, seq, t_tok, nchunk):
    n, d = x2.shape
    per = seq // t_tok
    return pl.pallas_call(
        functools.partial(_combine_kernel, t_tok=t_tok, nchunk=nchunk),
        out_shape=jax.ShapeDtypeStruct((n, d), f32),
        grid=(n // t_tok,),
        in_specs=[pl.BlockSpec((1, 1, t_tok * TOP_K), lambda i: (i, 0, 0), memory_space=pltpu.SMEM),
                  pl.BlockSpec(memory_space=pl.ANY),
                  pl.BlockSpec((t_tok, LANES), lambda i: (i, 0)),
                  pl.BlockSpec((t_tok, d), lambda i: (i, 0)),
                  pl.BlockSpec((1, 1, d), lambda i: (i // per, 0, 0))],
        out_specs=pl.BlockSpec((t_tok, d), lambda i: (i, 0)),
        scratch_shapes=[pltpu.VMEM((TOP_K * t_tok * nchunk, LANES), f32), pltpu.SemaphoreType.DMA],
        compiler_params=_cparams(("arbitrary",)),
        name="moe_combine",
    )(pos3, y, gate, x2, g2)


def _moe_plan(idx, rank, cnt, ne, tm, ntile_max):
    counts = cnt[0, :ne].astype(jnp.int32)
    ntile_e = (counts + tm - 1) // tm
    tile_end = jnp.cumsum(ntile_e)
    tile_start = tile_end - ntile_e
    used = tile_end[-1]
    e_idx = idx[:, :TOP_K].astype(jnp.int32)
    pos = tile_start[e_idx] * tm + rank[:, :TOP_K].astype(jnp.int32)
    tiles = jnp.minimum(jnp.arange(ntile_max, dtype=jnp.int32), used - 1)
    tile_expert = jnp.sum(tile_end[None, :] <= tiles[:, None], axis=1).astype(jnp.int32)
    ztile = jnp.where(ntile_e > 0, tile_end - 1, -1)
    ztile = jnp.concatenate([ztile, used.reshape(1)]).astype(jnp.int32)
    return pos, tile_expert, used.reshape(1).astype(jnp.int32), ztile


def _lane_tile(v):
    return jnp.tile(v.astype(f32), LANES // v.shape[0]).reshape(1, LANES)


def kernel(x, c, positions, ada_w, ada_b, norm1_g, norm2_g, sb_w_qkv, sb_w_o, dsa_w_in,
           dsa_q_gain, dsa_k_gain, dsa_w_o, router_w, router_b, exp_w_gu, exp_b_gu,
           exp_w_down, exp_b_down):
    batch, seq, d = x.shape
    depth = ada_w.shape[0]
    n = batch * seq
    ne = router_w.shape[-1]
    ff = exp_w_down.shape[2]
    topk = min(TOPK_MAX, seq // 4)
    assert seq % KB == 0 and d % LANES == 0 and ne <= LANES

    mod = _modulation(c, ada_w, ada_b)
    x2 = x.reshape(n, d)

    ncols = dsa_w_in.shape[-1]
    ncols_pad = -(-ncols // LANES) * LANES
    half = ROPE_DIMS // 2
    inv = jnp.exp(-math.log(ROPE_THETA) * (2.0 * jnp.arange(half, dtype=f32) / ROPE_DIMS))
    l64 = jnp.arange(LANES) % HEAD_DIM
    inv_lane = jnp.where(l64 < ROPE_DIMS, inv[l64 % half], 0.0).reshape(1, LANES)
    bd = (jnp.arange(LANES)[:, None] // HEAD_DIM == jnp.arange(LANES)[None, :] // HEAD_DIM)
    bd = bd.astype(bf16)
    pos = positions.reshape(n, 1)

    nchunk = d // LANES
    ntile_max = (n * TOP_K) // MOE_TM + ne
    assert (n * TOP_K) % MOE_TM == 0 and n % MOE_TOK == 0 and seq % MOE_TOK == 0
    tri = (jnp.arange(POST_TM)[:, None] > jnp.arange(POST_TM)[None, :]).astype(bf16)
    wgu_all = _deinterleave(exp_w_gu.reshape(depth * ne, d, 2 * ff))

    for layer in range(depth):
        m6 = [mod[layer][:, k * d:(k + 1) * d].reshape(batch, 1, d) for k in range(6)]
        sh1, sc1, g1, sh2, sc2, g2 = m6
        n1 = norm1_g[layer].reshape(1, d)
        n2 = norm2_g[layer].reshape(1, d)
        j = layer // 2
        if layer % 2 == 0:
            qkv = _proj(x2, n1, sc1, sh1, sb_w_qkv[j].astype(bf16), seq)
            o = _stick_breaking(qkv, batch, seq, d)
            wo = sb_w_o[j].astype(bf16)
        else:
            w = dsa_w_in[j]
            kv0 = d
            qi0 = d + 2 * LANES
            ki0 = qi0 + IDX_HEADS * IDX_DIM
            w = jnp.concatenate([w[:, :kv0], w[:, qi0:ki0], w[:, kv0:qi0], w[:, ki0:]], axis=1)
            w = jnp.pad(w, ((0, 0), (0, ncols_pad - ncols))).astype(bf16)
            proj, wi = _dsa_proj(x2, n1, sc1, sh1, w, pos, inv_lane, _lane_tile(dsa_q_gain[j]),
                                 _lane_tile(dsa_k_gain[j]), bd, seq)
            o = _dsa_attention(proj, wi, batch, seq, d, topk)
            wo = dsa_w_o[j].astype(bf16)
        rw = jnp.pad(router_w[layer], ((0, 0), (0, LANES - ne)))
        rb = jnp.pad(router_b[layer], (0, LANES - ne), constant_values=-jnp.inf).reshape(1, LANES)
        x2, hflat, idx, gate, rank, cnt = _post_attention(o, wo, x2, g1, n2, sc2, sh2, rw, rb,
                                                          tri, seq, POST_TM)
        spos, tile_expert, used, ztile = _moe_plan(idx, rank, cnt, ne, MOE_TM, ntile_max)
        pos3 = spos.reshape(n // MOE_TOK, 1, MOE_TOK * TOP_K)
        xs = _dispatch(hflat, pos3, ztile, ntile_max * MOE_TM, MOE_TM, MOE_TOK, nchunk)
        bgu = exp_b_gu[layer]
        bgu = jnp.concatenate([bgu[..., 0::2], bgu[..., 1::2]], axis=-1).reshape(ne, 1, 2 * ff)
        ys = _grouped_mlp(tile_expert, used, xs, wgu_all[layer * ne:(layer + 1) * ne], bgu,
                          exp_w_down[layer].astype(bf16), exp_b_down[layer].reshape(ne, 1, d),
                          MOE_TM, nchunk)
        x2 = _combine(pos3, ys, gate, x2, g2, seq, MOE_TOK, nchunk)
    return x2.reshape(batch, seq, d)
```

```python
import functools
import math

import jax
import jax.numpy as jnp
from jax import lax
from jax.experimental import pallas as pl
from jax.experimental.pallas import tpu as pltpu

HEAD_DIM = 64
DSA_KV_HEADS = 2
IDX_HEADS = 8
IDX_DIM = 64
CHUNK_SHIFT = 6
TOPK_MAX = 256
TOP_K = 4
ROPE_THETA = 500000.0
ROPE_DIMS = HEAD_DIM // 4
SWIGLU_ALPHA = 1.702
SWIGLU_LIMIT = 7.0
EPS = 1e-6

LANES = 128
QB = 128
KB = 256
DQ = 256
NEG = -1e30
INT_MIN = -(2 ** 31)
KEY_NEG_INF = -2139095041
NORM_SLACK = 1.02
L_TINY = 1e-30
VMEM_LIMIT = 48 * 1024 * 1024
POST_TM = 512
MOE_TM = 512
MOE_TOK = 256
DMA_UNROLL = 8
SB_WINDOW = 3
SB_CUTOFF = -110.0

f32 = jnp.float32
bf16 = jnp.bfloat16


def _cparams(sem):
    return pltpu.CompilerParams(dimension_semantics=sem, vmem_limit_bytes=VMEM_LIMIT)


def _dot(a, b):
    return jnp.dot(a, b, preferred_element_type=f32)


def _dot_nt(a, b):
    return lax.dot_general(a, b, (((1,), (1,)), ((), ())), preferred_element_type=f32)


def _dot_split(x, m01, passes):
    acc = None
    r = x
    for p in range(passes):
        t = r.astype(bf16)
        d = _dot(t, m01)
        acc = d if acc is None else acc + d
        if p + 1 < passes:
            r = r - t.astype(f32)
    return acc


def _norm_mod(x, g, sc, sh):
    ms = jnp.mean(x * x, axis=-1, keepdims=True)
    return (x * lax.rsqrt(ms + EPS) * g) * (1.0 + sc) + sh


def _mod_kernel(c_ref, w_ref, b_ref, o_ref):
    c = c_ref[...]
    cs = c * (1.0 / (1.0 + jnp.exp(-c)))
    o_ref[0] = jnp.dot(cs, w_ref[0], preferred_element_type=f32,
                       precision=lax.Precision.HIGHEST) + b_ref[0]


def _modulation(c, ada_w, ada_b):
    depth, d, n6 = ada_w.shape
    b = c.shape[0]
    tn = 1024
    return pl.pallas_call(
        _mod_kernel,
        out_shape=jax.ShapeDtypeStruct((depth, b, n6), f32),
        grid=(depth, n6 // tn),
        in_specs=[pl.BlockSpec((b, d), lambda l, j: (0, 0)),
                  pl.BlockSpec((1, d, tn), lambda l, j: (l, 0, j)),
                  pl.BlockSpec((1, 1, tn), lambda l, j: (l, 0, j))],
        out_specs=pl.BlockSpec((1, b, tn), lambda l, j: (l, 0, j)),
        compiler_params=_cparams(("arbitrary", "arbitrary")),
        name="adaln_mod",
    )(c, ada_w, ada_b.reshape(depth, 1, n6))


def _proj_kernel(x_ref, g_ref, sc_ref, sh_ref, w_ref, o_ref):
    h = _norm_mod(x_ref[...], g_ref[...], sc_ref[0], sh_ref[0])
    o_ref[...] = _dot(h.astype(bf16), w_ref[...]).astype(o_ref.dtype)


def _proj(x2, g, sc, sh, w, seq, tm=512):
    n, d = x2.shape
    nc = w.shape[1]
    per = seq // tm
    return pl.pallas_call(
        _proj_kernel,
        out_shape=jax.ShapeDtypeStruct((n, nc), bf16),
        grid=(n // tm,),
        in_specs=[pl.BlockSpec((tm, d), lambda i: (i, 0)),
                  pl.BlockSpec((1, d), lambda i: (0, 0)),
                  pl.BlockSpec((1, 1, d), lambda i: (i // per, 0, 0)),
                  pl.BlockSpec((1, 1, d), lambda i: (i // per, 0, 0)),
                  pl.BlockSpec((d, nc), lambda i: (0, 0))],
        out_specs=pl.BlockSpec((tm, nc), lambda i: (i, 0)),
        compiler_params=_cparams(("arbitrary",)),
        name="sb_qkv_proj",
    )(x2, g, sc, sh, w)


def _sb_kernel(q_ref, k_ref, v_ref, o_ref, k0_ref, k1_ref):
    seq = q_ref.shape[0]
    nqb = seq // QB
    lane = lax.broadcasted_iota(jnp.int32, (1, LANES), 1)
    kk = k_ref[...]
    zero = jnp.zeros_like(kk)
    k0_ref[...] = jnp.where(lane < HEAD_DIM, kk, zero)
    k1_ref[...] = jnp.where(lane >= HEAD_DIM, kk, zero)
    row = lax.broadcasted_iota(jnp.int32, (QB, QB), 0)
    col = lax.broadcasted_iota(jnp.int32, (QB, QB), 1)
    before = col < row
    tri = jnp.where(row > col, 1.0, 0.0).astype(bf16)

    def windows(jobs, nb, diag):
        zs = [_dot_nt(q, kh_ref[pl.ds(k0, nb * QB), :]) for q, kh_ref, k0, _ in jobs]
        lms, lss = [], []
        for z in zs:
            l1p = jnp.log(1.0 + jnp.exp(-jnp.abs(z)))
            lm = jnp.minimum(-z, 0.0) - l1p
            lss.append(lm + z)
            parts = [lm[:, s * QB:(s + 1) * QB] for s in range(nb)]
            if diag:
                parts[-1] = jnp.where(before, parts[-1], 0.0)
            lms.append(parts)
        cums = [[_dot_split(p, tri, 2) for p in parts] for parts in lms]
        outs = []
        for (q, kh_ref, k0, c_in), parts, cum, ls in zip(jobs, lms, cums, lss):
            c = c_in
            probs = [None] * nb
            for s in reversed(range(nb)):
                a = jnp.exp(ls[:, s * QB:(s + 1) * QB] + (cum[s] + c))
                if diag and s == nb - 1:
                    a = jnp.where(before, a, 0.0)
                probs[s] = a.astype(bf16)
                c = c + jnp.sum(parts[s], axis=1, keepdims=True)
            outs.append((c, probs[0] if nb == 1 else jnp.concatenate(probs, axis=1)))
        return [(c, _dot(a_all, v_ref[pl.ds(k0, nb * QB), :]))
                for (c, a_all), (_, _, k0, _) in zip(outs, jobs)]

    def qheads(blocks, nb):
        jobs, meta = [], []
        zc = jnp.zeros((QB, 1), f32)
        for i in blocks:
            q0 = i * QB
            k0 = q0 - (nb - 1) * QB
            if not isinstance(i, int):
                q0 = pl.multiple_of(q0, QB)
                k0 = pl.multiple_of(k0, QB)
            q = q_ref[pl.ds(q0, QB), :] * (HEAD_DIM ** -0.5)
            jobs += [(q, k0_ref, k0, zc), (q, k1_ref, k0, zc)]
            meta.append((q0, q))
        res = windows(jobs, nb, True)
        return [(q0, q) + res[2 * n] + res[2 * n + 1] for n, (q0, q) in enumerate(meta)]

    def qhead(i, nb):
        return qheads([i], nb)[0]

    def qtail(i, nb, st):
        q0, q, c0, a0, c1, a1 = st
        if nb <= SB_WINDOW and not (isinstance(i, int) and i < nb):
            def cond(s):
                jb, c0, _, c1, _ = s
                live = jnp.max(jnp.maximum(c0, c1)) > SB_CUTOFF
                return jnp.logical_and(jb >= 0, live)

            def body(s):
                jb, c0, a0, c1, a1 = s
                kb = pl.multiple_of(jb * QB, QB)
                (c0, d0), (c1, d1) = windows([(q, k0_ref, kb, c0), (q, k1_ref, kb, c1)], 1, False)
                return jb - 1, c0, a0 + d0, c1, a1 + d1

            jb0 = jnp.asarray(i - nb, jnp.int32)
            _, c0, a0, c1, a1 = lax.while_loop(cond, body, (jb0, c0, a0, c1, a1))
        o_ref[pl.ds(q0, QB), :] = jnp.where(lane < HEAD_DIM, a0, a1).astype(o_ref.dtype)

    first = min(SB_WINDOW, nqb)
    npairs = (nqb - first) // 2
    singles = list(range(first + 2 * npairs, nqb))
    for i in range(first):
        if i + 1 == SB_WINDOW and singles:
            continue
        qtail(i, i + 1, qhead(i, i + 1))

    def pair(j, _):
        i = first + 2 * j
        sa, sb = qheads([i, i + 1], SB_WINDOW)
        qtail(i, SB_WINDOW, sa)
        qtail(i + 1, SB_WINDOW, sb)
        return 0

    lax.fori_loop(0, npairs, pair, 0)
    if singles:
        group = ([first - 1] if first == SB_WINDOW else []) + singles
        for i, st in zip(group, qheads(group, SB_WINDOW)):
            qtail(i, SB_WINDOW, st)


def _stick_breaking(qkv, batch, seq, d):
    npair = d // LANES
    return pl.pallas_call(
        _sb_kernel,
        out_shape=jax.ShapeDtypeStruct((batch * seq, d), bf16),
        grid=(batch, npair),
        in_specs=[pl.BlockSpec((seq, LANES), lambda b, p: (b, p)),
                  pl.BlockSpec((seq, LANES), lambda b, p: (b, npair + p)),
                  pl.BlockSpec((seq, LANES), lambda b, p: (b, 2 * npair + p))],
        out_specs=pl.BlockSpec((seq, LANES), lambda b, p: (b, p)),
        scratch_shapes=[pltpu.VMEM((seq, LANES), bf16), pltpu.VMEM((seq, LANES), bf16)],
        compiler_params=_cparams(("arbitrary", "arbitrary")),
        name="stick_breaking_attn",
    )(qkv, qkv, qkv)


def _dsa_proj_kernel(x_ref, g_ref, sc_ref, sh_ref, w_ref, pos_ref, inv_ref, qg_ref, kg_ref,
                     bd_ref, o_ref, wi_ref, *, d):
    h = _norm_mod(x_ref[...], g_ref[...], sc_ref[0], sh_ref[0])
    p = _dot(h.astype(bf16), w_ref[...])
    lane = lax.broadcasted_iota(jnp.int32, (1, LANES), 1)
    ang = pos_ref[...].astype(f32) * inv_ref[...]
    cos_t = jnp.cos(ang)
    sin_t = jnp.sin(ang)
    upper = (lane % ROPE_DIMS) >= (ROPE_DIMS // 2)
    s_up = jnp.where(upper, sin_t, 0.0)
    s_lo = jnp.where(upper, 0.0, -sin_t)
    half = ROPE_DIMS // 2

    def rope(y):
        return y * cos_t + pltpu.roll(y, half, 1) * s_up + pltpu.roll(y, LANES - half, 1) * s_lo

    def headnorm(y, gain):
        ms = _dot_split(y * y, bd_ref[...], 2) * (1.0 / HEAD_DIM)
        return y * lax.rsqrt(ms + EPS) * gain

    nq = d // LANES
    for c in range(nq):
        y = p[:, c * LANES:(c + 1) * LANES]
        o_ref[:, c * LANES:(c + 1) * LANES] = rope(headnorm(y, qg_ref[...])).astype(bf16)
    c0 = nq
    for c in range(c0, c0 + IDX_HEADS * IDX_DIM // LANES):
        o_ref[:, c * LANES:(c + 1) * LANES] = rope(p[:, c * LANES:(c + 1) * LANES]).astype(bf16)
    c0 += IDX_HEADS * IDX_DIM // LANES
    y = p[:, c0 * LANES:(c0 + 1) * LANES]
    o_ref[:, c0 * LANES:(c0 + 1) * LANES] = rope(headnorm(y, kg_ref[...])).astype(bf16)
    c0 += 1
    o_ref[:, c0 * LANES:(c0 + 1) * LANES] = p[:, c0 * LANES:(c0 + 1) * LANES].astype(bf16)
    c0 += 1
    y = p[:, c0 * LANES:(c0 + 1) * LANES]
    o_ref[:, c0 * LANES:(c0 + 1) * LANES] = jnp.where(lane < IDX_DIM, rope(y), 0.0).astype(bf16)
    wi = pltpu.roll(y, LANES - IDX_DIM, 1) * (IDX_HEADS ** -0.5)
    wi_ref[...] = jnp.where(lane < IDX_HEADS, wi, 0.0)


def _dsa_proj(x2, g, sc, sh, w, pos, inv_lane, qg, kg, bd, seq, tm=512):
    n, d = x2.shape
    nc = w.shape[1]
    per = seq // tm
    return pl.pallas_call(
        functools.partial(_dsa_proj_kernel, d=d),
        out_shape=(jax.ShapeDtypeStruct((n, nc), bf16), jax.ShapeDtypeStruct((n, LANES), f32)),
        grid=(n // tm,),
        in_specs=[pl.BlockSpec((tm, d), lambda i: (i, 0)),
                  pl.BlockSpec((1, d), lambda i: (0, 0)),
                  pl.BlockSpec((1, 1, d), lambda i: (i // per, 0, 0)),
                  pl.BlockSpec((1, 1, d), lambda i: (i // per, 0, 0)),
                  pl.BlockSpec((d, nc), lambda i: (0, 0)),
                  pl.BlockSpec((tm, 1), lambda i: (i, 0)),
                  pl.BlockSpec((1, LANES), lambda i: (0, 0)),
                  pl.BlockSpec((1, LANES), lambda i: (0, 0)),
                  pl.BlockSpec((1, LANES), lambda i: (0, 0)),
                  pl.BlockSpec((LANES, LANES), lambda i: (0, 0))],
        out_specs=(pl.BlockSpec((tm, nc), lambda i: (i, 0)),
                   pl.BlockSpec((tm, LANES), lambda i: (i, 0))),
        compiler_params=_cparams(("arbitrary",)),
        name="dsa_in_proj",
    )(x2, g, sc, sh, w, pos, inv_lane, qg, kg, bd)


def _dsa_kernel(q_ref, k_ref, v_ref, qi_ref, ki_ref, wi_ref, o_ref,
                kk_ref, vv_ref, ki2_ref, keys_ref, bias_ref, m_ref, acc_ref, thr_ref, need_ref,
                qs_ref, kmax_ref, *, topk):
    seq = k_ref.shape[0]
    nchunk = q_ref.shape[1] // LANES
    group = (2 * nchunk) // DSA_KV_HEADS
    i = pl.program_id(1)
    lane = lax.broadcasted_iota(jnp.int32, (1, LANES), 1)
    lo = lane < HEAD_DIM

    @pl.when(i == 0)
    def _():
        k = k_ref[...]
        v = v_ref[...]
        ki = ki_ref[...]
        kr = pltpu.roll(k.astype(f32), HEAD_DIM, 1).astype(bf16)
        vr = pltpu.roll(v.astype(f32), HEAD_DIM, 1).astype(bf16)
        zero = jnp.zeros_like(k)
        kk_ref[0] = jnp.where(lo, k, zero)
        kk_ref[1] = jnp.where(lo, zero, kr)
        kk_ref[2] = jnp.where(lo, kr, zero)
        kk_ref[3] = jnp.where(lo, zero, k)
        lane_full = lax.broadcasted_iota(jnp.int32, k.shape, 1)
        oh = [jnp.where(lane_full < HEAD_DIM, 1.0, 0.0).astype(bf16),
              jnp.where(lane_full < HEAD_DIM, 0.0, 1.0).astype(bf16)]
        vv_ref[0] = jnp.concatenate([jnp.where(lo, v, zero), oh[0]], axis=1)
        vv_ref[1] = jnp.concatenate([jnp.where(lo, zero, vr), oh[1]], axis=1)
        vv_ref[2] = jnp.concatenate([jnp.where(lo, vr, zero), oh[0]], axis=1)
        vv_ref[3] = jnp.concatenate([jnp.where(lo, zero, v), oh[1]], axis=1)
        ki2_ref[0] = ki
        ki2_ref[1] = pltpu.roll(ki.astype(f32), HEAD_DIM, 1).astype(bf16)
        kf = k.astype(f32)
        rr = lax.broadcasted_iota(jnp.int32, (LANES, LANES), 0) < HEAD_DIM
        cc = lax.broadcasted_iota(jnp.int32, (LANES, LANES), 1) < HEAD_DIM
        bd = jnp.where(rr == cc, 1.0, 0.0).astype(bf16)
        n2 = jnp.max(_dot((kf * kf).astype(bf16), bd), axis=0, keepdims=True)
        for g in range(DSA_KV_HEADS):
            sel = lo if g == 0 else jnp.logical_not(lo)
            top = jnp.max(jnp.where(sel, n2, 0.0), axis=1, keepdims=True)
            kmax_ref[g] = jnp.broadcast_to(top, (1, LANES))

    nkb = i + 1
    qrow = lax.broadcasted_iota(jnp.int32, (DQ, KB), 0) + i * DQ
    kcol = lax.broadcasted_iota(jnp.int32, (DQ, KB), 1)

    wi = wi_ref[...]
    wcols = [wi[:, hd:hd + 1] for hd in range(IDX_HEADS)]

    def score_block(jb, _):
        k0 = pl.multiple_of(jb * KB, KB)
        score = jnp.zeros((DQ, KB), f32)
        for hd in range(IDX_HEADS):
            qc = qi_ref[:, (hd // 2) * LANES:(hd // 2 + 1) * LANES]
            kb = ki2_ref[hd % 2, pl.ds(k0, KB), :]
            rel = jnp.maximum(_dot_nt(qc, kb), 0.0) * (IDX_DIM ** -0.5)
            score = score + rel * wcols[hd]
        score = score + 0.0
        bits = pltpu.bitcast(score, jnp.int32)
        key = bits ^ ((bits >> 31) & 0x7FFFFFFF)
        adm = ((kcol + k0) >> CHUNK_SHIFT) <= (qrow >> CHUNK_SHIFT)
        keys_ref[jb] = jnp.where(adm, key, INT_MIN)
        return 0

    lax.fori_loop(0, nkb, score_block, 0)

    kf = float(topk)
    ones_l = jnp.ones((LANES, LANES), bf16)

    def search(nk):
        def count(r, pred):
            part = None
            for jb in range(nk):
                for hl in range(KB // LANES):
                    key = keys_ref[jb, r * QB:(r + 1) * QB, hl * LANES:(hl + 1) * LANES]
                    hit = jnp.where(pred(key), 1.0, 0.0)
                    part = hit if part is None else part + hit
            return _dot(part.astype(bf16), ones_l)

        groups = range(DQ // QB)
        t0 = tuple(jnp.where(count(r, lambda key: key >= 0) >= kf, 0, INT_MIN).astype(jnp.int32)
                   for r in groups)

        def bit_step(s, ts):
            bit = jnp.left_shift(jnp.int32(1), 30 - s)
            out = []
            for r in groups:
                cand = ts[r] + bit
                enough = count(r, lambda key, cand=cand: key >= cand) >= kf
                out.append(jnp.where(enough, cand, ts[r]))
            return tuple(out)

        ts = lax.fori_loop(0, 31, bit_step, t0)
        for r in groups:
            thr_ref[r * QB:(r + 1) * QB, :] = ts[r]
            need_ref[r * QB:(r + 1) * QB, :] = kf - count(r, lambda key, t=ts[r]: key > t)

    for nk in range(1, seq // KB + 1):
        @pl.when(nkb == nk)
        def _(nk=nk):
            search(nk)

    thr = jnp.concatenate([thr_ref[...]] * (KB // LANES), axis=1)
    need = jnp.concatenate([need_ref[...]] * (KB // LANES), axis=1)

    r2 = lax.broadcasted_iota(jnp.int32, (KB, KB), 0)
    c2 = lax.broadcasted_iota(jnp.int32, (KB, KB), 1)
    tri = jnp.where(r2 < c2, 1.0, 0.0).astype(bf16)
    ones_k = jnp.ones((KB, KB), bf16)

    def bias_block(jb, carry):
        key = keys_ref[jb]
        eq = key == thr
        eqb = jnp.where(eq, 1.0, 0.0).astype(bf16)
        rank = _dot(eqb, tri) + carry
        sel = (key > thr) | (eq & (rank < need))
        sel = sel & (key > KEY_NEG_INF)
        bias_ref[jb] = jnp.where(sel, 0.0, NEG)
        return carry + _dot(eqb, ones_k)

    lax.fori_loop(0, nkb, bias_block, jnp.zeros((DQ, KB), f32))

    cpg = group // 2
    for g in range(DSA_KV_HEADS):
        qs_ref[g] = jnp.concatenate(
            [q_ref[:, c * LANES:(c + 1) * LANES] for c in range(g * cpg, (g + 1) * cpg)],
            axis=0) * (HEAD_DIM ** -0.5)

    def scores(g, hf, k0, bias):
        return _dot_nt(qs_ref[g], kk_ref[2 * g + hf, pl.ds(k0, KB), :]) + bias

    def sweep_max():
        m_ref[...] = jnp.full(m_ref.shape, NEG, f32)

        def max_block(jb, _):
            k0 = pl.multiple_of(jb * KB, KB)
            bias = jnp.concatenate([bias_ref[jb]] * cpg, axis=0)
            for g in range(DSA_KV_HEADS):
                for hf in range(2):
                    s = scores(g, hf, k0, bias)
                    fold = jnp.maximum(s[:, :LANES], s[:, LANES:])
                    m_ref[2 * g + hf] = jnp.maximum(m_ref[2 * g + hf], fold)
            return 0

        lax.fori_loop(0, nkb, max_block, 0)
        for h in range(2 * DSA_KV_HEADS):
            m = jnp.max(m_ref[h], axis=1, keepdims=True)
            m_ref[h] = jnp.broadcast_to(m, (cpg * DQ, LANES))

    def sweep_exp():
        acc_ref[...] = jnp.zeros(acc_ref.shape, f32)

        def attn_block(jb, _):
            k0 = pl.multiple_of(jb * KB, KB)
            bias = jnp.concatenate([bias_ref[jb]] * cpg, axis=0)
            for g in range(DSA_KV_HEADS):
                ps = []
                for hf in range(2):
                    m = m_ref[2 * g + hf]
                    s = scores(g, hf, k0, bias)
                    ps.append(jnp.exp(s - jnp.concatenate([m, m], axis=1)).astype(bf16))
                p2 = jnp.concatenate(ps, axis=1)
                v2 = jnp.concatenate([vv_ref[2 * g, pl.ds(k0, KB), :],
                                      vv_ref[2 * g + 1, pl.ds(k0, KB), :]], axis=0)
                acc_ref[g] += _dot(p2, v2)
            return 0

        lax.fori_loop(0, nkb, attn_block, 0)

    def write_out():
        for c in range(nchunk):
            r = acc_ref[c // cpg, (c % cpg) * DQ:(c % cpg + 1) * DQ, :]
            o_ref[:, c * LANES:(c + 1) * LANES] = (r[:, :LANES] / r[:, LANES:]).astype(o_ref.dtype)

    for g in range(DSA_KV_HEADS):
        q = qs_ref[g].astype(f32)
        q2 = (q * q).astype(bf16)
        for hf in range(2):
            half = jnp.where((lax.broadcasted_iota(jnp.int32, (LANES, LANES), 0) < HEAD_DIM)
                             == (hf == 0), 1.0, 0.0).astype(bf16)
            qn2 = _dot(q2, half)
            m_ref[2 * g + hf] = NORM_SLACK * jnp.sqrt(qn2 * kmax_ref[g])
    sweep_exp()
    lmin = jnp.min(acc_ref[:, :, LANES:])

    @pl.when(lmin > L_TINY)
    def _():
        write_out()

    @pl.when(jnp.logical_not(lmin > L_TINY))
    def _():
        sweep_max()
        sweep_exp()
        write_out()


def group_even(d):
    return ((d // HEAD_DIM) // DSA_KV_HEADS) % 2 == 0


def _dsa_attention(proj, wi, batch, seq, d, topk):
    nqb = seq // DQ
    nchunk = d // LANES
    cpg = nchunk // DSA_KV_HEADS
    qiw = IDX_HEADS * IDX_DIM
    assert d % qiw == 0 and group_even(d) and DQ == KB
    qicol = d // qiw
    kcol = nchunk + qiw // LANES
    vcol = kcol + 1
    kicol = kcol + 2
    return pl.pallas_call(
        functools.partial(_dsa_kernel, topk=topk),
        out_shape=jax.ShapeDtypeStruct((batch * seq, d), bf16),
        grid=(batch, nqb),
        in_specs=[pl.BlockSpec((DQ, d), lambda b, i: (b * nqb + i, 0)),
                  pl.BlockSpec((seq, LANES), lambda b, i: (b, kcol)),
                  pl.BlockSpec((seq, LANES), lambda b, i: (b, vcol)),
                  pl.BlockSpec((DQ, IDX_HEADS * IDX_DIM), lambda b, i: (b * nqb + i, qicol)),
                  pl.BlockSpec((seq, LANES), lambda b, i: (b, kicol)),
                  pl.BlockSpec((DQ, LANES), lambda b, i: (b * nqb + i, 0))],
        out_specs=pl.BlockSpec((DQ, d), lambda b, i: (b * nqb + i, 0)),
        scratch_shapes=[pltpu.VMEM((4, seq, LANES), bf16),
                        pltpu.VMEM((4, seq, 2 * LANES), bf16),
                        pltpu.VMEM((2, seq, LANES), bf16),
                        pltpu.VMEM((seq // KB, DQ, KB), jnp.int32),
                        pltpu.VMEM((seq // KB, DQ, KB), f32),
                        pltpu.VMEM((2 * DSA_KV_HEADS, cpg * DQ, LANES), f32),
                        pltpu.VMEM((DSA_KV_HEADS, cpg * DQ, 2 * LANES), f32),
                        pltpu.VMEM((DQ, LANES), jnp.int32),
                        pltpu.VMEM((DQ, LANES), f32),
                        pltpu.VMEM((DSA_KV_HEADS, cpg * DQ, LANES), bf16),
                        pltpu.VMEM((DSA_KV_HEADS, 1, LANES), f32)],
        compiler_params=_cparams(("arbitrary", "arbitrary")),
        name="dsa_sparse_attn",
    )(proj, proj, proj, proj, proj, wi)


def _post_kernel(o_ref, wo_ref, x_ref, g1_ref, n2_ref, sc_ref, sh_ref, rw_ref, rb_ref, tri_ref,
                 xo_ref, hf_ref, idx_ref, gate_ref, rank_ref, cnt_ref, run_ref):
    @pl.when(pl.program_id(0) == 0)
    def _():
        run_ref[...] = jnp.zeros(run_ref.shape, f32)

    y = _dot(o_ref[...], wo_ref[...])
    x = x_ref[...] + g1_ref[0] * y
    xo_ref[...] = x
    h = _norm_mod(x, n2_ref[...], sc_ref[0], sh_ref[0])
    tm, d = h.shape
    nchunk = d // LANES
    for c in range(nchunk):
        hf_ref[pl.ds(c, tm, stride=nchunk), :] = h[:, c * LANES:(c + 1) * LANES]
    h_hi = h.astype(bf16)
    h_lo = (h - h_hi.astype(f32)).astype(bf16)
    logits = (_dot(h_hi, rw_ref[0]) + _dot(h_hi, rw_ref[1]) + _dot(h_lo, rw_ref[0])) + rb_ref[...]
    lane = lax.broadcasted_iota(jnp.int32, logits.shape, 1).astype(f32)
    work = logits
    vals, idxs, hits = [], [], []
    for _ in range(TOP_K):
        m = jnp.max(work, axis=1, keepdims=True)
        idx = jnp.min(jnp.where(work == m, lane, float(LANES)), axis=1, keepdims=True)
        hit = lane == idx
        vals.append(m)
        idxs.append(idx)
        hits.append(hit)
        work = jnp.where(hit, -jnp.inf, work)
    es = [jnp.exp(v - vals[0]) for v in vals]
    inv = 1.0 / (es[0] + es[1] + es[2] + es[3])
    multi = jnp.zeros_like(logits)
    for hit in hits:
        multi = multi + jnp.where(hit, 1.0, 0.0)
    before = _dot(tri_ref[...], multi.astype(bf16)) + run_ref[...]
    idx_o = jnp.zeros_like(logits)
    gate_o = jnp.zeros_like(logits)
    rank_o = jnp.zeros_like(logits)
    for k in range(TOP_K):
        slot = lane == float(k)
        rank = jnp.sum(jnp.where(hits[k], before, 0.0), axis=1, keepdims=True)
        idx_o = jnp.where(slot, idxs[k], idx_o)
        gate_o = jnp.where(slot, es[k] * inv, gate_o)
        rank_o = jnp.where(slot, rank, rank_o)
    idx_ref[...] = idx_o
    gate_ref[...] = gate_o
    rank_ref[...] = rank_o
    run_ref[...] += jnp.sum(multi, axis=0, keepdims=True)
    cnt_ref[...] = run_ref[...]


def _post_attention(o, wo, x2, g1, n2, sc2, sh2, rw, rb, tri, seq, tm):
    n, d = x2.shape
    per = seq // tm
    nchunk = d // LANES
    row = lambda i: (i, 0)
    fix = lambda i: (0, 0)
    bat = lambda i: (i // per, 0, 0)
    lane_out = jax.ShapeDtypeStruct((n, LANES), f32)
    return pl.pallas_call(
        _post_kernel,
        out_shape=(jax.ShapeDtypeStruct((n, d), f32), jax.ShapeDtypeStruct((n * nchunk, LANES), f32),
                   lane_out, lane_out, lane_out, jax.ShapeDtypeStruct((1, LANES), f32)),
        grid=(n // tm,),
        in_specs=[pl.BlockSpec((tm, d), row), pl.BlockSpec((d, d), fix),
                  pl.BlockSpec((tm, d), row), pl.BlockSpec((1, 1, d), bat),
                  pl.BlockSpec((1, d), fix), pl.BlockSpec((1, 1, d), bat),
                  pl.BlockSpec((1, 1, d), bat), pl.BlockSpec((2, d, LANES), lambda i: (0, 0, 0)),
                  pl.BlockSpec((1, LANES), fix), pl.BlockSpec((tm, tm), fix)],
        out_specs=(pl.BlockSpec((tm, d), row), pl.BlockSpec((tm * nchunk, LANES), row),
                   pl.BlockSpec((tm, LANES), row), pl.BlockSpec((tm, LANES), row),
                   pl.BlockSpec((tm, LANES), row), pl.BlockSpec((1, LANES), fix)),
        scratch_shapes=[pltpu.VMEM((1, LANES), f32)],
        compiler_params=_cparams(("arbitrary",)),
        name="attn_out_norm_router",
    )(o, wo, x2, g1, n2, sc2, sh2, rw, rb, tri)


def _deint_kernel(w_ref, p_ref, o_ref):
    ff = o_ref.shape[2] // 2
    wide = 2 * LANES
    for b in range(o_ref.shape[2] // wide):
        x = w_ref[0, :, b * wide:(b + 1) * wide].astype(bf16)
        r = _dot(x, p_ref[...]).astype(bf16)
        o_ref[0, :, b * LANES:(b + 1) * LANES] = r[:, :LANES]
        o_ref[0, :, ff + b * LANES:ff + (b + 1) * LANES] = r[:, LANES:]


def _deinterleave(w):
    ne, d, ff2 = w.shape
    wide = 2 * LANES
    src = jnp.arange(wide)
    perm = jnp.where(src < LANES, 2 * src, 2 * (src - LANES) + 1)
    p = (jnp.arange(wide)[:, None] == perm[None, :]).astype(bf16)
    return pl.pallas_call(
        _deint_kernel,
        out_shape=jax.ShapeDtypeStruct((ne, d, ff2), bf16),
        grid=(ne,),
        in_specs=[pl.BlockSpec((1, d, ff2), lambda e: (e, 0, 0)),
                  pl.BlockSpec((wide, wide), lambda e: (0, 0))],
        out_specs=pl.BlockSpec((1, d, ff2), lambda e: (e, 0, 0)),
        compiler_params=_cparams(("arbitrary",)),
        name="expert_weight_prep",
    )(w, p)


def _dispatch_kernel(zt_ref, pos_ref, hf_ref, xs_ref, zbuf_ref, sem, zsem, *, t_tok, tm, nchunk):
    i = pl.program_id(0)
    ne = zt_ref.shape[0] - 1
    ntile = xs_ref.shape[0] // (tm * nchunk)

    def zero_tile(e):
        return zt_ref[e] if e < ne else zt_ref[ne] + (e - ne)

    def zero_copy(e):
        start = pl.multiple_of(zero_tile(e) * (tm * nchunk), tm * nchunk)
        return pltpu.make_async_copy(zbuf_ref, xs_ref.at[pl.ds(start, tm * nchunk)], zsem)

    def zero_wanted(e):
        return zero_tile(e) >= 0 if e < ne else zero_tile(e) < ntile

    @pl.when(i == 0)
    def _():
        zbuf_ref[...] = jnp.zeros(zbuf_ref.shape, f32)
        for e in range(2 * ne):
            @pl.when(zero_wanted(e))
            def _():
                zero_copy(e).start()
        for e in range(2 * ne):
            @pl.when(zero_wanted(e))
            def _():
                zero_copy(e).wait()

    def issue(tt, _):
        for u in range(DMA_UNROLL):
            t = tt * DMA_UNROLL + u
            src = hf_ref.at[pl.ds(pl.multiple_of(t * nchunk, nchunk), nchunk)]
            for k in range(TOP_K):
                p = pos_ref[0, 0, t * TOP_K + k]
                dst = xs_ref.at[pl.ds(pl.multiple_of(p * nchunk, nchunk), nchunk)]
                pltpu.make_async_copy(src, dst, sem).start()
        return 0

    lax.fori_loop(0, t_tok // DMA_UNROLL, issue, 0)
    for k in range(TOP_K):
        pltpu.make_async_copy(hf_ref, xs_ref.at[pl.ds(0, t_tok * nchunk)], sem).wait()


def _dispatch(hflat, pos3, ztile, rows_pad, tm, t_tok, nchunk):
    ntile = pos3.shape[0]
    return pl.pallas_call(
        functools.partial(_dispatch_kernel, t_tok=t_tok, tm=tm, nchunk=nchunk),
        out_shape=jax.ShapeDtypeStruct((rows_pad * nchunk, LANES), f32),
        grid_spec=pltpu.PrefetchScalarGridSpec(
            num_scalar_prefetch=1,
            grid=(ntile,),
            in_specs=[pl.BlockSpec((1, 1, t_tok * TOP_K), lambda i, zt: (i, 0, 0),
                                   memory_space=pltpu.SMEM),
                      pl.BlockSpec((t_tok * nchunk, LANES), lambda i, zt: (i, 0))],
            out_specs=pl.BlockSpec(memory_space=pl.ANY),
            scratch_shapes=[pltpu.VMEM((tm * nchunk, LANES), f32),
                            pltpu.SemaphoreType.DMA, pltpu.SemaphoreType.DMA]),
        compiler_params=_cparams(("arbitrary",)),
        name="moe_dispatch",
    )(ztile, pos3, hflat)


def _gmm_kernel(te_ref, used_ref, xs_ref, wgu_ref, bgu_ref, wd_ref, bd_ref, y_ref, *, ff, nchunk):
    @pl.when(pl.program_id(0) < used_ref[0])
    def _():
        tm = xs_ref.shape[0] // nchunk
        x = jnp.concatenate([xs_ref[pl.ds(c, tm, stride=nchunk), :] for c in range(nchunk)],
                            axis=1).astype(bf16)
        gu = _dot(x, wgu_ref[0]) + bgu_ref[0]
        g = jnp.minimum(gu[:, :ff], SWIGLU_LIMIT)
        u = jnp.clip(gu[:, ff:], -SWIGLU_LIMIT, SWIGLU_LIMIT)
        act = (u + 1.0) * (g * (1.0 / (1.0 + jnp.exp(-SWIGLU_ALPHA * g))))
        y = _dot(act.astype(bf16), wd_ref[0]) + bd_ref[0]
        for c in range(nchunk):
            y_ref[pl.ds(c, tm, stride=nchunk), :] = y[:, c * LANES:(c + 1) * LANES]

    @pl.when(pl.program_id(0) >= used_ref[0])
    def _():
        y_ref[...] = jnp.zeros(y_ref.shape, f32)


def _grouped_mlp(tile_expert, used, xs, wgu, bgu, wd, bdn, tm, nchunk):
    ne, d, ff2 = wgu.shape
    ntile = tile_expert.shape[0]
    rows = lambda j, te, used: (jnp.minimum(j, used[0] - 1), 0)
    rows_out = lambda j, te, used: (j, 0)
    exp3 = lambda j, te, used: (te[j], 0, 0)
    return pl.pallas_call(
        functools.partial(_gmm_kernel, ff=ff2 // 2, nchunk=nchunk),
        out_shape=jax.ShapeDtypeStruct(xs.shape, f32),
        grid_spec=pltpu.PrefetchScalarGridSpec(
            num_scalar_prefetch=2,
            grid=(ntile,),
            in_specs=[pl.BlockSpec((tm * nchunk, LANES), rows),
                      pl.BlockSpec((1, d, ff2), exp3), pl.BlockSpec((1, 1, ff2), exp3),
                      pl.BlockSpec((1, ff2 // 2, d), exp3), pl.BlockSpec((1, 1, d), exp3)],
            out_specs=pl.BlockSpec((tm * nchunk, LANES), rows_out)),
        compiler_params=_cparams(("arbitrary",)),
        name="moe_grouped_mlp",
    )(tile_expert, used, xs, wgu, bgu, wd, bdn)


def _combine_kernel(pos_ref, y_ref, gate_ref, x_ref, g2_ref, o_ref, buf_ref, sem, *, t_tok, nchunk):
    def issue(tt, _):
        for u in range(DMA_UNROLL):
            t = tt * DMA_UNROLL + u
            for k in range(TOP_K):
                p = pos_ref[0, 0, t * TOP_K + k]
                src = y_ref.at[pl.ds(pl.multiple_of(p * nchunk, nchunk), nchunk)]
                dst = buf_ref.at[pl.ds(pl.multiple_of((k * t_tok + t) * nchunk, nchunk), nchunk)]
                pltpu.make_async_copy(src, dst, sem).start()
        return 0

    lax.fori_loop(0, t_tok // DMA_UNROLL, issue, 0)
    pltpu.make_async_copy(y_ref.at[pl.ds(0, TOP_K * t_tok * nchunk)], buf_ref, sem).wait()
    gates = gate_ref[...]
    g2 = g2_ref[0]
    for c in range(nchunk):
        acc = None
        for k in range(TOP_K):
            rows = buf_ref[pl.ds(k * t_tok * nchunk + c, t_tok, stride=nchunk), :]
            term = gates[:, k:k + 1] * rows
            acc = term if acc is None else acc + term
        sl = slice(c * LANES, (c + 1) * LANES)
        o_ref[:, sl] = x_ref[:, sl] + g2[:, sl] * acc


def _combine(pos3, y, gate, x2, g2, seq, t_tok, nchunk):
    n, d = x2.shape
    per = seq // t_tok
    return pl.pallas_call(
        functools.partial(_combine_kernel, t_tok=t_tok, nchunk=nchunk),
        out_shape=jax.ShapeDtypeStruct((n, d), f32),
        grid=(n // t_tok,),
        in_specs=[pl.BlockSpec((1, 1, t_tok * TOP_K), lambda i: (i, 0, 0), memory_space=pltpu.SMEM),
                  pl.BlockSpec(memory_space=pl.ANY),
                  pl.BlockSpec((t_tok, LANES), lambda i: (i, 0)),
                  pl.BlockSpec((t_tok, d), lambda i: (i, 0)),
                  pl.BlockSpec((1, 1, d), lambda i: (i // per, 0, 0))],
        out_specs=pl.BlockSpec((t_tok, d), lambda i: (i, 0)),
        scratch_shapes=[pltpu.VMEM((TOP_K * t_tok * nchunk, LANES), f32), pltpu.SemaphoreType.DMA],
        compiler_params=_cparams(("arbitrary",)),
        name="moe_combine",
    )(pos3, y, gate, x2, g2)


def _moe_plan(idx, rank, cnt, ne, tm, ntile_max):
    counts = cnt[0, :ne].astype(jnp.int32)
    ntile_e = (counts + tm - 1) // tm
    tile_end = jnp.cumsum(ntile_e)
    tile_start = tile_end - ntile_e
    used = tile_end[-1]
    e_idx = idx[:, :TOP_K].astype(jnp.int32)
    onehot = e_idx[:, :, None] == jnp.arange(ne, dtype=jnp.int32)[None, None, :]
    start = jnp.sum(jnp.where(onehot, (tile_start * tm)[None, None, :], 0), axis=-1)
    pos = start + rank[:, :TOP_K].astype(jnp.int32)
    tiles = jnp.minimum(jnp.arange(ntile_max, dtype=jnp.int32), used - 1)
    tile_expert = jnp.sum(tile_end[None, :] <= tiles[:, None], axis=1).astype(jnp.int32)
    ztile = jnp.where(ntile_e > 0, tile_end - 1, -1)
    ztile = jnp.concatenate([ztile, used.reshape(1)]).astype(jnp.int32)
    return pos, tile_expert, used.reshape(1).astype(jnp.int32), ztile


def _lane_tile(v):
    return jnp.tile(v.astype(f32), LANES // v.shape[0]).reshape(1, LANES)


def kernel(x, c, positions, ada_w, ada_b, norm1_g, norm2_g, sb_w_qkv, sb_w_o, dsa_w_in,
           dsa_q_gain, dsa_k_gain, dsa_w_o, router_w, router_b, exp_w_gu, exp_b_gu,
           exp_w_down, exp_b_down):
    batch, seq, d = x.shape
    depth = ada_w.shape[0]
    n = batch * seq
    ne = router_w.shape[-1]
    ff = exp_w_down.shape[2]
    topk = min(TOPK_MAX, seq // 4)
    assert seq % KB == 0 and d % LANES == 0 and ne <= LANES

    mod = _modulation(c, ada_w, ada_b)
    x2 = x.reshape(n, d)

    ncols = dsa_w_in.shape[-1]
    ncols_pad = -(-ncols // LANES) * LANES
    half = ROPE_DIMS // 2
    inv = jnp.exp(-math.log(ROPE_THETA) * (2.0 * jnp.arange(half, dtype=f32) / ROPE_DIMS))
    l64 = jnp.arange(LANES) % HEAD_DIM
    inv_lane = jnp.where(l64 < ROPE_DIMS, inv[l64 % half], 0.0).reshape(1, LANES)
    bd = (jnp.arange(LANES)[:, None] // HEAD_DIM == jnp.arange(LANES)[None, :] // HEAD_DIM)
    bd = bd.astype(bf16)
    pos = positions.reshape(n, 1)

    nchunk = d // LANES
    ntile_max = (n * TOP_K) // MOE_TM + ne
    assert (n * TOP_K) % MOE_TM == 0 and n % MOE_TOK == 0 and seq % MOE_TOK == 0
    tri = (jnp.arange(POST_TM)[:, None] > jnp.arange(POST_TM)[None, :]).astype(bf16)
    wgu_all = _deinterleave(exp_w_gu.reshape(depth * ne, d, 2 * ff))

    for layer in range(depth):
        m6 = [mod[layer][:, k * d:(k + 1) * d].reshape(batch, 1, d) for k in range(6)]
        sh1, sc1, g1, sh2, sc2, g2 = m6
        n1 = norm1_g[layer].reshape(1, d)
        n2 = norm2_g[layer].reshape(1, d)
        j = layer // 2
        if layer % 2 == 0:
            qkv = _proj(x2, n1, sc1, sh1, sb_w_qkv[j].astype(bf16), seq)
            o = _stick_breaking(qkv, batch, seq, d)
            wo = sb_w_o[j].astype(bf16)
        else:
            w = dsa_w_in[j]
            kv0 = d
            qi0 = d + 2 * LANES
            ki0 = qi0 + IDX_HEADS * IDX_DIM
            w = jnp.concatenate([w[:, :kv0], w[:, qi0:ki0], w[:, kv0:qi0], w[:, ki0:]], axis=1)
            w = jnp.pad(w, ((0, 0), (0, ncols_pad - ncols))).astype(bf16)
            proj, wi = _dsa_proj(x2, n1, sc1, sh1, w, pos, inv_lane, _lane_tile(dsa_q_gain[j]),
                                 _lane_tile(dsa_k_gain[j]), bd, seq)
            o = _dsa_attention(proj, wi, batch, seq, d, topk)
            wo = dsa_w_o[j].astype(bf16)
        rw = jnp.pad(router_w[layer], ((0, 0), (0, LANES - ne)))
        rw_hi = rw.astype(bf16)
        rw = jnp.stack([rw_hi, (rw - rw_hi.astype(f32)).astype(bf16)])
        rb = jnp.pad(router_b[layer], (0, LANES - ne), constant_values=-jnp.inf).reshape(1, LANES)
        x2, hflat, idx, gate, rank, cnt = _post_attention(o, wo, x2, g1, n2, sc2, sh2, rw, rb,
                                                          tri, seq, POST_TM)
        spos, tile_expert, used, ztile = _moe_plan(idx, rank, cnt, ne, MOE_TM, ntile_max)
        pos3 = spos.reshape(n // MOE_TOK, 1, MOE_TOK * TOP_K)
        xs = _dispatch(hflat, pos3, ztile, ntile_max * MOE_TM, MOE_TM, MOE_TOK, nchunk)
        bgu = exp_b_gu[layer]
        bgu = jnp.concatenate([bgu[..., 0::2], bgu[..., 1::2]], axis=-1).reshape(ne, 1, 2 * ff)
        ys = _grouped_mlp(tile_expert, used, xs, wgu_all[layer * ne:(layer + 1) * ne], bgu,
                          exp_w_down[layer].astype(bf16), exp_b_down[layer].reshape(ne, 1, d),
                          MOE_TM, nchunk)
        x2 = _combine(pos3, ys, gate, x2, g2, seq, MOE_TOK, nchunk)
    return x2.reshape(batch, seq, d)
```

```python
import functools
import math

import jax
import jax.numpy as jnp
from jax import lax
from jax.experimental import pallas as pl
from jax.experimental.pallas import tpu as pltpu

HEAD_DIM = 64
DSA_KV_HEADS = 2
IDX_HEADS = 8
IDX_DIM = 64
CHUNK_SHIFT = 6
TOPK_MAX = 256
TOP_K = 4
ROPE_THETA = 500000.0
ROPE_DIMS = HEAD_DIM // 4
SWIGLU_ALPHA = 1.702
SWIGLU_LIMIT = 7.0
EPS = 1e-6

LANES = 128
QB = 128
KB = 256
DQ = 256
NEG = -1e30
INT_MIN = -(2 ** 31)
KEY_NEG_INF = -2139095041
NORM_SLACK = 1.02
L_TINY = 1e-30
VMEM_LIMIT = 48 * 1024 * 1024
POST_TM = 512
MOE_TM = 512
MOE_TOK = 256
DMA_UNROLL = 8
SB_GROUP = 4
SB_WINDOW = 3
SB_CUTOFF = -110.0

f32 = jnp.float32
bf16 = jnp.bfloat16


def _cparams(sem):
    return pltpu.CompilerParams(dimension_semantics=sem, vmem_limit_bytes=VMEM_LIMIT)


def _dot(a, b):
    return jnp.dot(a, b, preferred_element_type=f32)


def _dot_nt(a, b):
    return lax.dot_general(a, b, (((1,), (1,)), ((), ())), preferred_element_type=f32)


def _dot_split(x, m01, passes):
    acc = None
    r = x
    for p in range(passes):
        t = r.astype(bf16)
        d = _dot(t, m01)
        acc = d if acc is None else acc + d
        if p + 1 < passes:
            r = r - t.astype(f32)
    return acc


def _norm_mod(x, g, sc, sh):
    ms = jnp.mean(x * x, axis=-1, keepdims=True)
    return (x * lax.rsqrt(ms + EPS) * g) * (1.0 + sc) + sh


def _mod_kernel(c_ref, w_ref, b_ref, o_ref):
    c = c_ref[...]
    cs = c * (1.0 / (1.0 + jnp.exp(-c)))
    o_ref[0] = jnp.dot(cs, w_ref[0], preferred_element_type=f32,
                       precision=lax.Precision.HIGHEST) + b_ref[0]


def _modulation(c, ada_w, ada_b):
    depth, d, n6 = ada_w.shape
    b = c.shape[0]
    tn = 1024
    return pl.pallas_call(
        _mod_kernel,
        out_shape=jax.ShapeDtypeStruct((depth, b, n6), f32),
        grid=(depth, n6 // tn),
        in_specs=[pl.BlockSpec((b, d), lambda l, j: (0, 0)),
                  pl.BlockSpec((1, d, tn), lambda l, j: (l, 0, j)),
                  pl.BlockSpec((1, 1, tn), lambda l, j: (l, 0, j))],
        out_specs=pl.BlockSpec((1, b, tn), lambda l, j: (l, 0, j)),
        compiler_params=_cparams(("arbitrary", "arbitrary")),
        name="adaln_mod",
    )(c, ada_w, ada_b.reshape(depth, 1, n6))


def _proj_kernel(x_ref, g_ref, sc_ref, sh_ref, w_ref, o_ref):
    h = _norm_mod(x_ref[...], g_ref[...], sc_ref[0], sh_ref[0])
    o_ref[...] = _dot(h.astype(bf16), w_ref[...]).astype(o_ref.dtype)


def _proj(x2, g, sc, sh, w, seq, tm=512):
    n, d = x2.shape
    nc = w.shape[1]
    per = seq // tm
    return pl.pallas_call(
        _proj_kernel,
        out_shape=jax.ShapeDtypeStruct((n, nc), bf16),
        grid=(n // tm,),
        in_specs=[pl.BlockSpec((tm, d), lambda i: (i, 0)),
                  pl.BlockSpec((1, d), lambda i: (0, 0)),
                  pl.BlockSpec((1, 1, d), lambda i: (i // per, 0, 0)),
                  pl.BlockSpec((1, 1, d), lambda i: (i // per, 0, 0)),
                  pl.BlockSpec((d, nc), lambda i: (0, 0))],
        out_specs=pl.BlockSpec((tm, nc), lambda i: (i, 0)),
        compiler_params=_cparams(("arbitrary",)),
        name="sb_qkv_proj",
    )(x2, g, sc, sh, w)


def _sb_kernel(q_ref, k_ref, v_ref, o_ref, k0_ref, k1_ref):
    seq = q_ref.shape[0]
    nqb = seq // QB
    lane = lax.broadcasted_iota(jnp.int32, (1, LANES), 1)
    kk = k_ref[...]
    zero = jnp.zeros_like(kk)
    k0_ref[...] = jnp.where(lane < HEAD_DIM, kk, zero)
    k1_ref[...] = jnp.where(lane >= HEAD_DIM, kk, zero)
    row = lax.broadcasted_iota(jnp.int32, (QB, QB), 0)
    col = lax.broadcasted_iota(jnp.int32, (QB, QB), 1)
    before = col < row
    tri = jnp.where(row > col, 1.0, 0.0).astype(bf16)

    def windows(jobs, nb, diag):
        zs = [_dot_nt(q, kh_ref[pl.ds(k0, nb * QB), :]) for q, kh_ref, k0, _ in jobs]
        lms, lss = [], []
        for z in zs:
            l1p = jnp.log(1.0 + jnp.exp(-jnp.abs(z)))
            lm = jnp.minimum(-z, 0.0) - l1p
            lss.append(lm + z)
            parts = [lm[:, s * QB:(s + 1) * QB] for s in range(nb)]
            if diag:
                parts[-1] = jnp.where(before, parts[-1], 0.0)
            lms.append(parts)
        cums = [[_dot_split(p, tri, 2) for p in parts] for parts in lms]
        outs = []
        for (q, kh_ref, k0, c_in), parts, cum, ls in zip(jobs, lms, cums, lss):
            c = c_in
            probs = [None] * nb
            for s in reversed(range(nb)):
                a = jnp.exp(ls[:, s * QB:(s + 1) * QB] + (cum[s] + c))
                if diag and s == nb - 1:
                    a = jnp.where(before, a, 0.0)
                probs[s] = a.astype(bf16)
                c = c + jnp.sum(parts[s], axis=1, keepdims=True)
            outs.append((c, probs[0] if nb == 1 else jnp.concatenate(probs, axis=1)))
        return [(c, _dot(a_all, v_ref[pl.ds(k0, nb * QB), :]))
                for (c, a_all), (_, _, k0, _) in zip(outs, jobs)]

    def qheads(blocks, nb):
        jobs, meta = [], []
        zc = jnp.zeros((QB, 1), f32)
        for i in blocks:
            q0 = i * QB
            k0 = q0 - (nb - 1) * QB
            if not isinstance(i, int):
                q0 = pl.multiple_of(q0, QB)
                k0 = pl.multiple_of(k0, QB)
            q = q_ref[pl.ds(q0, QB), :] * (HEAD_DIM ** -0.5)
            jobs += [(q, k0_ref, k0, zc), (q, k1_ref, k0, zc)]
            meta.append((q0, q))
        res = windows(jobs, nb, True)
        return [(q0, q) + res[2 * n] + res[2 * n + 1] for n, (q0, q) in enumerate(meta)]

    def qhead(i, nb):
        return qheads([i], nb)[0]

    def qtail(i, nb, st):
        q0, q, c0, a0, c1, a1 = st
        if nb <= SB_WINDOW and not (isinstance(i, int) and i < nb):
            def cond(s):
                jb, c0, _, c1, _ = s
                live = jnp.max(jnp.maximum(c0, c1)) > SB_CUTOFF
                return jnp.logical_and(jb >= 0, live)

            def body(s):
                jb, c0, a0, c1, a1 = s
                kb = pl.multiple_of(jb * QB, QB)
                (c0, d0), (c1, d1) = windows([(q, k0_ref, kb, c0), (q, k1_ref, kb, c1)], 1, False)
                return jb - 1, c0, a0 + d0, c1, a1 + d1

            jb0 = jnp.asarray(i - nb, jnp.int32)
            _, c0, a0, c1, a1 = lax.while_loop(cond, body, (jb0, c0, a0, c1, a1))
        o_ref[pl.ds(q0, QB), :] = jnp.where(lane < HEAD_DIM, a0, a1).astype(o_ref.dtype)

    first = min(SB_WINDOW, nqb)
    npairs = (nqb - first) // SB_GROUP
    singles = list(range(first + SB_GROUP * npairs, nqb))
    for i in range(first):
        if i + 1 == SB_WINDOW and singles:
            continue
        qtail(i, i + 1, qhead(i, i + 1))

    def pair(j, _):
        i = first + SB_GROUP * j
        blocks = [i + n for n in range(SB_GROUP)]
        for b, st in zip(blocks, qheads(blocks, SB_WINDOW)):
            qtail(b, SB_WINDOW, st)
        return 0

    lax.fori_loop(0, npairs, pair, 0)
    if singles:
        group = ([first - 1] if first == SB_WINDOW else []) + singles
        for i, st in zip(group, qheads(group, SB_WINDOW)):
            qtail(i, SB_WINDOW, st)


def _stick_breaking(qkv, batch, seq, d):
    npair = d // LANES
    return pl.pallas_call(
        _sb_kernel,
        out_shape=jax.ShapeDtypeStruct((batch * seq, d), bf16),
        grid=(batch, npair),
        in_specs=[pl.BlockSpec((seq, LANES), lambda b, p: (b, p)),
                  pl.BlockSpec((seq, LANES), lambda b, p: (b, npair + p)),
                  pl.BlockSpec((seq, LANES), lambda b, p: (b, 2 * npair + p))],
        out_specs=pl.BlockSpec((seq, LANES), lambda b, p: (b, p)),
        scratch_shapes=[pltpu.VMEM((seq, LANES), bf16), pltpu.VMEM((seq, LANES), bf16)],
        compiler_params=_cparams(("arbitrary", "arbitrary")),
        name="stick_breaking_attn",
    )(qkv, qkv, qkv)


def _dsa_proj_kernel(x_ref, g_ref, sc_ref, sh_ref, w_ref, pos_ref, inv_ref, qg_ref, kg_ref,
                     bd_ref, o_ref, wi_ref, *, d):
    h = _norm_mod(x_ref[...], g_ref[...], sc_ref[0], sh_ref[0])
    p = _dot(h.astype(bf16), w_ref[...])
    lane = lax.broadcasted_iota(jnp.int32, (1, LANES), 1)
    ang = pos_ref[...].astype(f32) * inv_ref[...]
    cos_t = jnp.cos(ang)
    sin_t = jnp.sin(ang)
    upper = (lane % ROPE_DIMS) >= (ROPE_DIMS // 2)
    s_up = jnp.where(upper, sin_t, 0.0)
    s_lo = jnp.where(upper, 0.0, -sin_t)
    half = ROPE_DIMS // 2

    def rope(y):
        return y * cos_t + pltpu.roll(y, half, 1) * s_up + pltpu.roll(y, LANES - half, 1) * s_lo

    def headnorm(y, gain):
        ms = _dot_split(y * y, bd_ref[...], 2) * (1.0 / HEAD_DIM)
        return y * lax.rsqrt(ms + EPS) * gain

    nq = d // LANES
    for c in range(nq):
        y = p[:, c * LANES:(c + 1) * LANES]
        o_ref[:, c * LANES:(c + 1) * LANES] = rope(headnorm(y, qg_ref[...])).astype(bf16)
    c0 = nq
    for c in range(c0, c0 + IDX_HEADS * IDX_DIM // LANES):
        o_ref[:, c * LANES:(c + 1) * LANES] = rope(p[:, c * LANES:(c + 1) * LANES]).astype(bf16)
    c0 += IDX_HEADS * IDX_DIM // LANES
    y = p[:, c0 * LANES:(c0 + 1) * LANES]
    o_ref[:, c0 * LANES:(c0 + 1) * LANES] = rope(headnorm(y, kg_ref[...])).astype(bf16)
    c0 += 1
    o_ref[:, c0 * LANES:(c0 + 1) * LANES] = p[:, c0 * LANES:(c0 + 1) * LANES].astype(bf16)
    c0 += 1
    y = p[:, c0 * LANES:(c0 + 1) * LANES]
    o_ref[:, c0 * LANES:(c0 + 1) * LANES] = jnp.where(lane < IDX_DIM, rope(y), 0.0).astype(bf16)
    wi = pltpu.roll(y, LANES - IDX_DIM, 1) * (IDX_HEADS ** -0.5) * (IDX_DIM ** -0.5)
    wi_ref[...] = jnp.where(lane < IDX_HEADS, wi, 0.0)


def _dsa_proj(x2, g, sc, sh, w, pos, inv_lane, qg, kg, bd, seq, tm=512):
    n, d = x2.shape
    nc = w.shape[1]
    per = seq // tm
    return pl.pallas_call(
        functools.partial(_dsa_proj_kernel, d=d),
        out_shape=(jax.ShapeDtypeStruct((n, nc), bf16), jax.ShapeDtypeStruct((n, LANES), f32)),
        grid=(n // tm,),
        in_specs=[pl.BlockSpec((tm, d), lambda i: (i, 0)),
                  pl.BlockSpec((1, d), lambda i: (0, 0)),
                  pl.BlockSpec((1, 1, d), lambda i: (i // per, 0, 0)),
                  pl.BlockSpec((1, 1, d), lambda i: (i // per, 0, 0)),
                  pl.BlockSpec((d, nc), lambda i: (0, 0)),
                  pl.BlockSpec((tm, 1), lambda i: (i, 0)),
                  pl.BlockSpec((1, LANES), lambda i: (0, 0)),
                  pl.BlockSpec((1, LANES), lambda i: (0, 0)),
                  pl.BlockSpec((1, LANES), lambda i: (0, 0)),
                  pl.BlockSpec((LANES, LANES), lambda i: (0, 0))],
        out_specs=(pl.BlockSpec((tm, nc), lambda i: (i, 0)),
                   pl.BlockSpec((tm, LANES), lambda i: (i, 0))),
        compiler_params=_cparams(("arbitrary",)),
        name="dsa_in_proj",
    )(x2, g, sc, sh, w, pos, inv_lane, qg, kg, bd)


def _dsa_kernel(q_ref, k_ref, v_ref, qi_ref, ki_ref, wi_ref, o_ref,
                kk_ref, vv_ref, ki2_ref, keys_ref, bias_ref, m_ref, acc_ref, thr_ref, need_ref,
                qs_ref, kmax_ref, *, topk):
    seq = k_ref.shape[0]
    nchunk = q_ref.shape[1] // LANES
    group = (2 * nchunk) // DSA_KV_HEADS
    i = pl.program_id(1)
    lane = lax.broadcasted_iota(jnp.int32, (1, LANES), 1)
    lo = lane < HEAD_DIM

    @pl.when(i == 0)
    def _():
        k = k_ref[...]
        v = v_ref[...]
        ki = ki_ref[...]
        kr = pltpu.roll(k.astype(f32), HEAD_DIM, 1).astype(bf16)
        vr = pltpu.roll(v.astype(f32), HEAD_DIM, 1).astype(bf16)
        zero = jnp.zeros_like(k)
        kk_ref[0] = jnp.where(lo, k, zero)
        kk_ref[1] = jnp.where(lo, zero, kr)
        kk_ref[2] = jnp.where(lo, kr, zero)
        kk_ref[3] = jnp.where(lo, zero, k)
        lane_full = lax.broadcasted_iota(jnp.int32, k.shape, 1)
        oh = [jnp.where(lane_full < HEAD_DIM, 1.0, 0.0).astype(bf16),
              jnp.where(lane_full < HEAD_DIM, 0.0, 1.0).astype(bf16)]
        vv_ref[0] = jnp.concatenate([jnp.where(lo, v, zero), oh[0]], axis=1)
        vv_ref[1] = jnp.concatenate([jnp.where(lo, zero, vr), oh[1]], axis=1)
        vv_ref[2] = jnp.concatenate([jnp.where(lo, vr, zero), oh[0]], axis=1)
        vv_ref[3] = jnp.concatenate([jnp.where(lo, zero, v), oh[1]], axis=1)
        ki2_ref[0] = ki
        ki2_ref[1] = pltpu.roll(ki.astype(f32), HEAD_DIM, 1).astype(bf16)
        kf = k.astype(f32)
        rr = lax.broadcasted_iota(jnp.int32, (LANES, LANES), 0) < HEAD_DIM
        cc = lax.broadcasted_iota(jnp.int32, (LANES, LANES), 1) < HEAD_DIM
        bd = jnp.where(rr == cc, 1.0, 0.0).astype(bf16)
        n2 = jnp.max(_dot((kf * kf).astype(bf16), bd), axis=0, keepdims=True)
        for g in range(DSA_KV_HEADS):
            sel = lo if g == 0 else jnp.logical_not(lo)
            top = jnp.max(jnp.where(sel, n2, 0.0), axis=1, keepdims=True)
            kmax_ref[g] = jnp.broadcast_to(top, (1, LANES))

    nkb = i + 1
    qrow = lax.broadcasted_iota(jnp.int32, (DQ, KB), 0) + i * DQ
    kcol = lax.broadcasted_iota(jnp.int32, (DQ, KB), 1)

    wi = wi_ref[...]
    wcols = [wi[:, hd:hd + 1] for hd in range(IDX_HEADS)]

    def score_block(jb, _):
        k0 = pl.multiple_of(jb * KB, KB)
        score = jnp.zeros((DQ, KB), f32)
        for hd in range(IDX_HEADS):
            qc = qi_ref[:, (hd // 2) * LANES:(hd // 2 + 1) * LANES]
            kb = ki2_ref[hd % 2, pl.ds(k0, KB), :]
            score = score + jnp.maximum(_dot_nt(qc, kb), 0.0) * wcols[hd]
        score = score + 0.0
        bits = pltpu.bitcast(score, jnp.int32)
        key = bits ^ ((bits >> 31) & 0x7FFFFFFF)
        adm = ((kcol + k0) >> CHUNK_SHIFT) <= (qrow >> CHUNK_SHIFT)
        keys_ref[jb] = jnp.where(adm, key, INT_MIN)
        return 0

    lax.fori_loop(0, nkb, score_block, 0)

    kf = float(topk)
    ones_l = jnp.ones((LANES, LANES), bf16)

    def search(nk):
        def count(r, pred):
            part = None
            for jb in range(nk):
                for hl in range(KB // LANES):
                    key = keys_ref[jb, r * QB:(r + 1) * QB, hl * LANES:(hl + 1) * LANES]
                    hit = jnp.where(pred(key), 1.0, 0.0)
                    part = hit if part is None else part + hit
            return _dot(part.astype(bf16), ones_l)

        groups = range(DQ // QB)
        t0 = tuple(jnp.where(count(r, lambda key: key >= 0) >= kf, 0, INT_MIN).astype(jnp.int32)
                   for r in groups)

        def bit_step(s, ts):
            bit = jnp.left_shift(jnp.int32(1), 30 - s)
            out = []
            for r in groups:
                cand = ts[r] + bit
                enough = count(r, lambda key, cand=cand: key >= cand) >= kf
                out.append(jnp.where(enough, cand, ts[r]))
            return tuple(out)

        ts = lax.fori_loop(0, 31, bit_step, t0)
        for r in groups:
            thr_ref[r * QB:(r + 1) * QB, :] = ts[r]
            need_ref[r * QB:(r + 1) * QB, :] = kf - count(r, lambda key, t=ts[r]: key > t)

    for nk in range(1, seq // KB + 1):
        @pl.when(nkb == nk)
        def _(nk=nk):
            search(nk)

    thr = jnp.concatenate([thr_ref[...]] * (KB // LANES), axis=1)
    need = jnp.concatenate([need_ref[...]] * (KB // LANES), axis=1)

    r2 = lax.broadcasted_iota(jnp.int32, (KB, KB), 0)
    c2 = lax.broadcasted_iota(jnp.int32, (KB, KB), 1)
    tri = jnp.where(r2 < c2, 1.0, 0.0).astype(bf16)
    ones_k = jnp.ones((KB, KB), bf16)

    def bias_block(jb, carry):
        key = keys_ref[jb]
        eq = key == thr
        eqb = jnp.where(eq, 1.0, 0.0).astype(bf16)
        rank = _dot(eqb, tri) + carry
        sel = (key > thr) | (eq & (rank < need))
        sel = sel & (key > KEY_NEG_INF)
        bias_ref[jb] = jnp.where(sel, 0.0, NEG)
        return carry + _dot(eqb, ones_k)

    lax.fori_loop(0, nkb, bias_block, jnp.zeros((DQ, KB), f32))

    cpg = group // 2
    for g in range(DSA_KV_HEADS):
        qs_ref[g] = jnp.concatenate(
            [q_ref[:, c * LANES:(c + 1) * LANES] for c in range(g * cpg, (g + 1) * cpg)],
            axis=0) * (HEAD_DIM ** -0.5)

    def scores(g, hf, k0, bias):
        return _dot_nt(qs_ref[g], kk_ref[2 * g + hf, pl.ds(k0, KB), :]) + bias

    def sweep_max():
        m_ref[...] = jnp.full(m_ref.shape, NEG, f32)

        def max_block(jb, _):
            k0 = pl.multiple_of(jb * KB, KB)
            bias = jnp.concatenate([bias_ref[jb]] * cpg, axis=0)
            for g in range(DSA_KV_HEADS):
                for hf in range(2):
                    s = scores(g, hf, k0, bias)
                    fold = jnp.maximum(s[:, :LANES], s[:, LANES:])
                    m_ref[2 * g + hf] = jnp.maximum(m_ref[2 * g + hf], fold)
            return 0

        lax.fori_loop(0, nkb, max_block, 0)
        for h in range(2 * DSA_KV_HEADS):
            m = jnp.max(m_ref[h], axis=1, keepdims=True)
            m_ref[h] = jnp.broadcast_to(m, (cpg * DQ, LANES))

    def sweep_exp():
        acc_ref[...] = jnp.zeros(acc_ref.shape, f32)

        def attn_block(jb, _):
            k0 = pl.multiple_of(jb * KB, KB)
            bias = jnp.concatenate([bias_ref[jb]] * cpg, axis=0)
            for g in range(DSA_KV_HEADS):
                ps = []
                for hf in range(2):
                    m = m_ref[2 * g + hf]
                    s = scores(g, hf, k0, bias)
                    ps.append(jnp.exp(s - jnp.concatenate([m, m], axis=1)).astype(bf16))
                p2 = jnp.concatenate(ps, axis=1)
                v2 = jnp.concatenate([vv_ref[2 * g, pl.ds(k0, KB), :],
                                      vv_ref[2 * g + 1, pl.ds(k0, KB), :]], axis=0)
                acc_ref[g] += _dot(p2, v2)
            return 0

        lax.fori_loop(0, nkb, attn_block, 0)

    def write_out():
        for c in range(nchunk):
            r = acc_ref[c // cpg, (c % cpg) * DQ:(c % cpg + 1) * DQ, :]
            o_ref[:, c * LANES:(c + 1) * LANES] = (r[:, :LANES] / r[:, LANES:]).astype(o_ref.dtype)

    for g in range(DSA_KV_HEADS):
        q = qs_ref[g].astype(f32)
        q2 = (q * q).astype(bf16)
        for hf in range(2):
            half = jnp.where((lax.broadcasted_iota(jnp.int32, (LANES, LANES), 0) < HEAD_DIM)
                             == (hf == 0), 1.0, 0.0).astype(bf16)
            qn2 = _dot(q2, half)
            m_ref[2 * g + hf] = NORM_SLACK * jnp.sqrt(qn2 * kmax_ref[g])
    sweep_exp()
    lmin = jnp.min(acc_ref[:, :, LANES:])

    @pl.when(lmin > L_TINY)
    def _():
        write_out()

    @pl.when(jnp.logical_not(lmin > L_TINY))
    def _():
        sweep_max()
        sweep_exp()
        write_out()


def group_even(d):
    return ((d // HEAD_DIM) // DSA_KV_HEADS) % 2 == 0


def _dsa_attention(proj, wi, batch, seq, d, topk):
    nqb = seq // DQ
    nchunk = d // LANES
    cpg = nchunk // DSA_KV_HEADS
    qiw = IDX_HEADS * IDX_DIM
    assert d % qiw == 0 and group_even(d) and DQ == KB
    qicol = d // qiw
    kcol = nchunk + qiw // LANES
    vcol = kcol + 1
    kicol = kcol + 2
    return pl.pallas_call(
        functools.partial(_dsa_kernel, topk=topk),
        out_shape=jax.ShapeDtypeStruct((batch * seq, d), bf16),
        grid=(batch, nqb),
        in_specs=[pl.BlockSpec((DQ, d), lambda b, i: (b * nqb + i, 0)),
                  pl.BlockSpec((seq, LANES), lambda b, i: (b, kcol)),
                  pl.BlockSpec((seq, LANES), lambda b, i: (b, vcol)),
                  pl.BlockSpec((DQ, IDX_HEADS * IDX_DIM), lambda b, i: (b * nqb + i, qicol)),
                  pl.BlockSpec((seq, LANES), lambda b, i: (b, kicol)),
                  pl.BlockSpec((DQ, LANES), lambda b, i: (b * nqb + i, 0))],
        out_specs=pl.BlockSpec((DQ, d), lambda b, i: (b * nqb + i, 0)),
        scratch_shapes=[pltpu.VMEM((4, seq, LANES), bf16),
                        pltpu.VMEM((4, seq, 2 * LANES), bf16),
                        pltpu.VMEM((2, seq, LANES), bf16),
                        pltpu.VMEM((seq // KB, DQ, KB), jnp.int32),
                        pltpu.VMEM((seq // KB, DQ, KB), f32),
                        pltpu.VMEM((2 * DSA_KV_HEADS, cpg * DQ, LANES), f32),
                        pltpu.VMEM((DSA_KV_HEADS, cpg * DQ, 2 * LANES), f32),
                        pltpu.VMEM((DQ, LANES), jnp.int32),
                        pltpu.VMEM((DQ, LANES), f32),
                        pltpu.VMEM((DSA_KV_HEADS, cpg * DQ, LANES), bf16),
                        pltpu.VMEM((DSA_KV_HEADS, 1, LANES), f32)],
        compiler_params=_cparams(("arbitrary", "arbitrary")),
        name="dsa_sparse_attn",
    )(proj, proj, proj, proj, proj, wi)


def _post_kernel(o_ref, wo_ref, x_ref, g1_ref, n2_ref, sc_ref, sh_ref, rw_ref, rb_ref, tri_ref,
                 xo_ref, hf_ref, idx_ref, gate_ref, rank_ref, cnt_ref, run_ref):
    @pl.when(pl.program_id(0) == 0)
    def _():
        run_ref[...] = jnp.zeros(run_ref.shape, f32)

    y = _dot(o_ref[...], wo_ref[...])
    x = x_ref[...] + g1_ref[0] * y
    xo_ref[...] = x
    h = _norm_mod(x, n2_ref[...], sc_ref[0], sh_ref[0])
    tm, d = h.shape
    nchunk = d // LANES
    for c in range(nchunk):
        hf_ref[pl.ds(c, tm, stride=nchunk), :] = h[:, c * LANES:(c + 1) * LANES]
    h_hi = h.astype(bf16)
    h_lo = (h - h_hi.astype(f32)).astype(bf16)
    logits = (_dot(h_hi, rw_ref[0]) + _dot(h_hi, rw_ref[1]) + _dot(h_lo, rw_ref[0])) + rb_ref[...]
    lane = lax.broadcasted_iota(jnp.int32, logits.shape, 1).astype(f32)
    work = logits
    vals, idxs, hits = [], [], []
    for _ in range(TOP_K):
        m = jnp.max(work, axis=1, keepdims=True)
        idx = jnp.min(jnp.where(work == m, lane, float(LANES)), axis=1, keepdims=True)
        hit = lane == idx
        vals.append(m)
        idxs.append(idx)
        hits.append(hit)
        work = jnp.where(hit, -jnp.inf, work)
    es = [jnp.exp(v - vals[0]) for v in vals]
    inv = 1.0 / (es[0] + es[1] + es[2] + es[3])
    multi = jnp.zeros_like(logits)
    for hit in hits:
        multi = multi + jnp.where(hit, 1.0, 0.0)
    before = _dot(tri_ref[...], multi.astype(bf16)) + run_ref[...]
    idx_o = jnp.zeros_like(logits)
    gate_o = jnp.zeros_like(logits)
    rank_o = jnp.zeros_like(logits)
    for k in range(TOP_K):
        slot = lane == float(k)
        rank = jnp.sum(jnp.where(hits[k], before, 0.0), axis=1, keepdims=True)
        idx_o = jnp.where(slot, idxs[k], idx_o)
        gate_o = jnp.where(slot, es[k] * inv, gate_o)
        rank_o = jnp.where(slot, rank, rank_o)
    idx_ref[...] = idx_o
    gate_ref[...] = gate_o
    rank_ref[...] = rank_o
    run_ref[...] += jnp.sum(multi, axis=0, keepdims=True)
    cnt_ref[...] = run_ref[...]


def _post_attention(o, wo, x2, g1, n2, sc2, sh2, rw, rb, tri, seq, tm):
    n, d = x2.shape
    per = seq // tm
    nchunk = d // LANES
    row = lambda i: (i, 0)
    fix = lambda i: (0, 0)
    bat = lambda i: (i // per, 0, 0)
    lane_out = jax.ShapeDtypeStruct((n, LANES), f32)
    return pl.pallas_call(
        _post_kernel,
        out_shape=(jax.ShapeDtypeStruct((n, d), f32), jax.ShapeDtypeStruct((n * nchunk, LANES), f32),
                   lane_out, lane_out, lane_out, jax.ShapeDtypeStruct((1, LANES), f32)),
        grid=(n // tm,),
        in_specs=[pl.BlockSpec((tm, d), row), pl.BlockSpec((d, d), fix),
                  pl.BlockSpec((tm, d), row), pl.BlockSpec((1, 1, d), bat),
                  pl.BlockSpec((1, d), fix), pl.BlockSpec((1, 1, d), bat),
                  pl.BlockSpec((1, 1, d), bat), pl.BlockSpec((2, d, LANES), lambda i: (0, 0, 0)),
                  pl.BlockSpec((1, LANES), fix), pl.BlockSpec((tm, tm), fix)],
        out_specs=(pl.BlockSpec((tm, d), row), pl.BlockSpec((tm * nchunk, LANES), row),
                   pl.BlockSpec((tm, LANES), row), pl.BlockSpec((tm, LANES), row),
                   pl.BlockSpec((tm, LANES), row), pl.BlockSpec((1, LANES), fix)),
        scratch_shapes=[pltpu.VMEM((1, LANES), f32)],
        compiler_params=_cparams(("arbitrary",)),
        name="attn_out_norm_router",
    )(o, wo, x2, g1, n2, sc2, sh2, rw, rb, tri)


def _deint_kernel(w_ref, wd_ref, p_ref, o_ref, od_ref):
    ff = o_ref.shape[2] // 2
    wide = 2 * LANES
    for b in range(o_ref.shape[2] // wide):
        x = w_ref[0, :, b * wide:(b + 1) * wide].astype(bf16)
        r = _dot(x, p_ref[...]).astype(bf16)
        o_ref[0, :, b * LANES:(b + 1) * LANES] = r[:, :LANES]
        o_ref[0, :, ff + b * LANES:ff + (b + 1) * LANES] = r[:, LANES:]
    od_ref[...] = wd_ref[...].astype(bf16)


def _expert_weight_prep(w, wd):
    ne, d, ff2 = w.shape
    wide = 2 * LANES
    src = jnp.arange(wide)
    perm = jnp.where(src < LANES, 2 * src, 2 * (src - LANES) + 1)
    p = (jnp.arange(wide)[:, None] == perm[None, :]).astype(bf16)
    blk = lambda e: (e, 0, 0)
    return pl.pallas_call(
        _deint_kernel,
        out_shape=(jax.ShapeDtypeStruct((ne, d, ff2), bf16),
                   jax.ShapeDtypeStruct(wd.shape, bf16)),
        grid=(ne,),
        in_specs=[pl.BlockSpec((1, d, ff2), blk), pl.BlockSpec((1,) + wd.shape[1:], blk),
                  pl.BlockSpec((wide, wide), lambda e: (0, 0))],
        out_specs=(pl.BlockSpec((1, d, ff2), blk), pl.BlockSpec((1,) + wd.shape[1:], blk)),
        compiler_params=_cparams(("arbitrary",)),
        name="expert_weight_prep",
    )(w, wd, p)


def _dispatch_kernel(zt_ref, pos_ref, hf_ref, xs_ref, zbuf_ref, sem, zsem, *, t_tok, tm, nchunk):
    i = pl.program_id(0)
    ne = zt_ref.shape[0] - 1
    ntile = xs_ref.shape[0] // (tm * nchunk)

    def zero_tile(e):
        return zt_ref[e] if e < ne else zt_ref[ne] + (e - ne)

    def zero_copy(e):
        start = pl.multiple_of(zero_tile(e) * (tm * nchunk), tm * nchunk)
        return pltpu.make_async_copy(zbuf_ref, xs_ref.at[pl.ds(start, tm * nchunk)], zsem)

    def zero_wanted(e):
        return zero_tile(e) >= 0 if e < ne else zero_tile(e) < ntile

    @pl.when(i == 0)
    def _():
        zbuf_ref[...] = jnp.zeros(zbuf_ref.shape, f32)
        for e in range(2 * ne):
            @pl.when(zero_wanted(e))
            def _():
                zero_copy(e).start()
        for e in range(2 * ne):
            @pl.when(zero_wanted(e))
            def _():
                zero_copy(e).wait()

    def issue(tt, _):
        for u in range(DMA_UNROLL):
            t = tt * DMA_UNROLL + u
            src = hf_ref.at[pl.ds(pl.multiple_of(t * nchunk, nchunk), nchunk)]
            for k in range(TOP_K):
                p = pos_ref[0, 0, t * TOP_K + k]
                dst = xs_ref.at[pl.ds(pl.multiple_of(p * nchunk, nchunk), nchunk)]
                pltpu.make_async_copy(src, dst, sem).start()
        return 0

    lax.fori_loop(0, t_tok // DMA_UNROLL, issue, 0)
    for k in range(TOP_K):
        pltpu.make_async_copy(hf_ref, xs_ref.at[pl.ds(0, t_tok * nchunk)], sem).wait()


def _dispatch(hflat, pos3, ztile, rows_pad, tm, t_tok, nchunk):
    ntile = pos3.shape[0]
    return pl.pallas_call(
        functools.partial(_dispatch_kernel, t_tok=t_tok, tm=tm, nchunk=nchunk),
        out_shape=jax.ShapeDtypeStruct((rows_pad * nchunk, LANES), f32),
        grid_spec=pltpu.PrefetchScalarGridSpec(
            num_scalar_prefetch=1,
            grid=(ntile,),
            in_specs=[pl.BlockSpec((1, 1, t_tok * TOP_K), lambda i, zt: (i, 0, 0),
                                   memory_space=pltpu.SMEM),
                      pl.BlockSpec((t_tok * nchunk, LANES), lambda i, zt: (i, 0))],
            out_specs=pl.BlockSpec(memory_space=pl.ANY),
            scratch_shapes=[pltpu.VMEM((tm * nchunk, LANES), f32),
                            pltpu.SemaphoreType.DMA, pltpu.SemaphoreType.DMA]),
        compiler_params=_cparams(("arbitrary",)),
        name="moe_dispatch",
    )(ztile, pos3, hflat)


def _gmm_kernel(te_ref, used_ref, xs_ref, wgu_ref, bgu_ref, wd_ref, bd_ref, y_ref, *, ff, nchunk):
    @pl.when(pl.program_id(0) < used_ref[0])
    def _():
        tm = xs_ref.shape[0] // nchunk
        x = jnp.concatenate([xs_ref[pl.ds(c, tm, stride=nchunk), :] for c in range(nchunk)],
                            axis=1).astype(bf16)
        gu = _dot(x, wgu_ref[0]) + bgu_ref[0]
        g = jnp.minimum(gu[:, :ff], SWIGLU_LIMIT)
        u = jnp.clip(gu[:, ff:], -SWIGLU_LIMIT, SWIGLU_LIMIT)
        act = (u + 1.0) * (g * (1.0 / (1.0 + jnp.exp(-SWIGLU_ALPHA * g))))
        y = _dot(act.astype(bf16), wd_ref[0]) + bd_ref[0]
        for c in range(nchunk):
            y_ref[pl.ds(c, tm, stride=nchunk), :] = y[:, c * LANES:(c + 1) * LANES]

    @pl.when(pl.program_id(0) >= used_ref[0])
    def _():
        y_ref[...] = jnp.zeros(y_ref.shape, f32)


def _grouped_mlp(tile_expert, used, xs, wgu, bgu, wd, bdn, eoff, tm, nchunk):
    _, d, ff2 = wgu.shape
    ntile = tile_expert.shape[0]
    rows = lambda j, te, used: (jnp.minimum(j, used[0] - 1), 0)
    rows_out = lambda j, te, used: (j, 0)
    exp3 = lambda j, te, used: (te[j], 0, 0)
    wexp3 = lambda j, te, used: (te[j] + eoff, 0, 0)
    return pl.pallas_call(
        functools.partial(_gmm_kernel, ff=ff2 // 2, nchunk=nchunk),
        out_shape=jax.ShapeDtypeStruct(xs.shape, f32),
        grid_spec=pltpu.PrefetchScalarGridSpec(
            num_scalar_prefetch=2,
            grid=(ntile,),
            in_specs=[pl.BlockSpec((tm * nchunk, LANES), rows),
                      pl.BlockSpec((1, d, ff2), wexp3), pl.BlockSpec((1, 1, ff2), exp3),
                      pl.BlockSpec((1, ff2 // 2, d), wexp3), pl.BlockSpec((1, 1, d), exp3)],
            out_specs=pl.BlockSpec((tm * nchunk, LANES), rows_out)),
        compiler_params=_cparams(("arbitrary",)),
        name="moe_grouped_mlp",
    )(tile_expert, used, xs, wgu, bgu, wd, bdn)


def _combine_kernel(pos_ref, nxt_ref, y_ref, gate_ref, x_ref, g2_ref, o_ref, buf_ref, sem,
                    *, t_tok, nchunk):
    i = pl.program_id(0)

    def gather(p_ref, slot):
        def issue(tt, _):
            for u in range(DMA_UNROLL):
                t = tt * DMA_UNROLL + u
                for k in range(TOP_K):
                    p = p_ref[0, 0, t * TOP_K + k]
                    src = y_ref.at[pl.ds(pl.multiple_of(p * nchunk, nchunk), nchunk)]
                    row = pl.multiple_of((k * t_tok + t) * nchunk, nchunk)
                    pltpu.make_async_copy(src, buf_ref.at[slot, pl.ds(row, nchunk)],
                                          sem.at[slot]).start()
            return 0

        lax.fori_loop(0, t_tok // DMA_UNROLL, issue, 0)

    @pl.when(i == 0)
    def _():
        gather(pos_ref, 0)

    @pl.when(i + 1 < pl.num_programs(0))
    def _():
        gather(nxt_ref, (i + 1) % 2)

    slot = i % 2
    pltpu.make_async_copy(y_ref.at[pl.ds(0, TOP_K * t_tok * nchunk)], buf_ref.at[slot],
                          sem.at[slot]).wait()
    gates = gate_ref[...]
    g2 = g2_ref[0]
    for c in range(nchunk):
        acc = None
        for k in range(TOP_K):
            rows = buf_ref[slot, pl.ds(k * t_tok * nchunk + c, t_tok, stride=nchunk), :]
            term = gates[:, k:k + 1] * rows
            acc = term if acc is None else acc + term
        sl = slice(c * LANES, (c + 1) * LANES)
        o_ref[:, sl] = x_ref[:, sl] + g2[:, sl] * acc


def _combine(pos3, y, gate, x2, g2, seq, t_tok, nchunk):
    n, d = x2.shape
    per = seq // t_tok
    return pl.pallas_call(
        functools.partial(_combine_kernel, t_tok=t_tok, nchunk=nchunk),
        out_shape=jax.ShapeDtypeStruct((n, d), f32),
        grid=(n // t_tok,),
        in_specs=[pl.BlockSpec((1, 1, t_tok * TOP_K), lambda i: (i, 0, 0), memory_space=pltpu.SMEM),
                  pl.BlockSpec((1, 1, t_tok * TOP_K),
                               lambda i: (jnp.minimum(i + 1, n // t_tok - 1), 0, 0),
                               memory_space=pltpu.SMEM),
                  pl.BlockSpec(memory_space=pl.ANY),
                  pl.BlockSpec((t_tok, LANES), lambda i: (i, 0)),
                  pl.BlockSpec((t_tok, d), lambda i: (i, 0)),
                  pl.BlockSpec((1, 1, d), lambda i: (i // per, 0, 0))],
        out_specs=pl.BlockSpec((t_tok, d), lambda i: (i, 0)),
        scratch_shapes=[pltpu.VMEM((2, TOP_K * t_tok * nchunk, LANES), f32),
                        pltpu.SemaphoreType.DMA((2,))],
        compiler_params=_cparams(("arbitrary",)),
        name="moe_combine",
    )(pos3, pos3, y, gate, x2, g2)


def _moe_plan(idx, rank, cnt, ne, tm, ntile_max):
    counts = cnt[0, :ne].astype(jnp.int32)
    ntile_e = (counts + tm - 1) // tm
    tile_end = jnp.cumsum(ntile_e)
    tile_start = tile_end - ntile_e
    used = tile_end[-1]
    e_idx = idx[:, :TOP_K].astype(jnp.int32)
    onehot = e_idx[:, :, None] == jnp.arange(ne, dtype=jnp.int32)[None, None, :]
    start = jnp.sum(jnp.where(onehot, (tile_start * tm)[None, None, :], 0), axis=-1)
    pos = start + rank[:, :TOP_K].astype(jnp.int32)
    tiles = jnp.minimum(jnp.arange(ntile_max, dtype=jnp.int32), used - 1)
    tile_expert = jnp.sum(tile_end[None, :] <= tiles[:, None], axis=1).astype(jnp.int32)
    ztile = jnp.where(ntile_e > 0, tile_end - 1, -1)
    ztile = jnp.concatenate([ztile, used.reshape(1)]).astype(jnp.int32)
    return pos, tile_expert, used.reshape(1).astype(jnp.int32), ztile


def _lane_tile(v):
    return jnp.tile(v.astype(f32), LANES // v.shape[0]).reshape(1, LANES)


def kernel(x, c, positions, ada_w, ada_b, norm1_g, norm2_g, sb_w_qkv, sb_w_o, dsa_w_in,
           dsa_q_gain, dsa_k_gain, dsa_w_o, router_w, router_b, exp_w_gu, exp_b_gu,
           exp_w_down, exp_b_down):
    batch, seq, d = x.shape
    depth = ada_w.shape[0]
    n = batch * seq
    ne = router_w.shape[-1]
    ff = exp_w_down.shape[2]
    topk = min(TOPK_MAX, seq // 4)
    assert seq % KB == 0 and d % LANES == 0 and ne <= LANES

    mod = _modulation(c, ada_w, ada_b)
    x2 = x.reshape(n, d)

    ncols = dsa_w_in.shape[-1]
    ncols_pad = -(-ncols // LANES) * LANES
    half = ROPE_DIMS // 2
    inv = jnp.exp(-math.log(ROPE_THETA) * (2.0 * jnp.arange(half, dtype=f32) / ROPE_DIMS))
    l64 = jnp.arange(LANES) % HEAD_DIM
    inv_lane = jnp.where(l64 < ROPE_DIMS, inv[l64 % half], 0.0).reshape(1, LANES)
    bd = (jnp.arange(LANES)[:, None] // HEAD_DIM == jnp.arange(LANES)[None, :] // HEAD_DIM)
    bd = bd.astype(bf16)
    pos = positions.reshape(n, 1)

    nchunk = d // LANES
    ntile_max = (n * TOP_K) // MOE_TM + ne
    assert (n * TOP_K) % MOE_TM == 0 and n % MOE_TOK == 0 and seq % MOE_TOK == 0
    tri = (jnp.arange(POST_TM)[:, None] > jnp.arange(POST_TM)[None, :]).astype(bf16)
    wgu_all, wd_all = _expert_weight_prep(exp_w_gu.reshape(depth * ne, d, 2 * ff),
                                          exp_w_down.reshape(depth * ne, ff, d))

    for layer in range(depth):
        m6 = [mod[layer][:, k * d:(k + 1) * d].reshape(batch, 1, d) for k in range(6)]
        sh1, sc1, g1, sh2, sc2, g2 = m6
        n1 = norm1_g[layer].reshape(1, d)
        n2 = norm2_g[layer].reshape(1, d)
        j = layer // 2
        if layer % 2 == 0:
            qkv = _proj(x2, n1, sc1, sh1, sb_w_qkv[j].astype(bf16), seq)
            o = _stick_breaking(qkv, batch, seq, d)
            wo = sb_w_o[j].astype(bf16)
        else:
            w = dsa_w_in[j]
            kv0 = d
            qi0 = d + 2 * LANES
            ki0 = qi0 + IDX_HEADS * IDX_DIM
            w = jnp.concatenate([w[:, :kv0], w[:, qi0:ki0], w[:, kv0:qi0], w[:, ki0:]], axis=1)
            w = jnp.pad(w, ((0, 0), (0, ncols_pad - ncols))).astype(bf16)
            proj, wi = _dsa_proj(x2, n1, sc1, sh1, w, pos, inv_lane, _lane_tile(dsa_q_gain[j]),
                                 _lane_tile(dsa_k_gain[j]), bd, seq)
            o = _dsa_attention(proj, wi, batch, seq, d, topk)
            wo = dsa_w_o[j].astype(bf16)
        rw = jnp.pad(router_w[layer], ((0, 0), (0, LANES - ne)))
        rw_hi = rw.astype(bf16)
        rw = jnp.stack([rw_hi, (rw - rw_hi.astype(f32)).astype(bf16)])
        rb = jnp.pad(router_b[layer], (0, LANES - ne), constant_values=-jnp.inf).reshape(1, LANES)
        x2, hflat, idx, gate, rank, cnt = _post_attention(o, wo, x2, g1, n2, sc2, sh2, rw, rb,
                                                          tri, seq, POST_TM)
        spos, tile_expert, used, ztile = _moe_plan(idx, rank, cnt, ne, MOE_TM, ntile_max)
        pos3 = spos.reshape(n // MOE_TOK, 1, MOE_TOK * TOP_K)
        xs = _dispatch(hflat, pos3, ztile, ntile_max * MOE_TM, MOE_TM, MOE_TOK, nchunk)
        bgu = exp_b_gu[layer]
        bgu = jnp.concatenate([bgu[..., 0::2], bgu[..., 1::2]], axis=-1).reshape(ne, 1, 2 * ff)
        ys = _grouped_mlp(tile_expert, used, xs, wgu_all, bgu, wd_all,
                          exp_b_down[layer].reshape(ne, 1, d), layer * ne, MOE_TM, nchunk)
        x2 = _combine(pos3, ys, gate, x2, g2, seq, MOE_TOK, nchunk)
    return x2.reshape(batch, seq, d)
```

```python
import functools
import math

import jax
import jax.numpy as jnp
from jax import lax
from jax.experimental import pallas as pl
from jax.experimental.pallas import tpu as pltpu

HEAD_DIM = 64
DSA_KV_HEADS = 2
IDX_HEADS = 8
IDX_DIM = 64
CHUNK_SHIFT = 6
TOPK_MAX = 256
TOP_K = 4
ROPE_THETA = 500000.0
ROPE_DIMS = HEAD_DIM // 4
SWIGLU_ALPHA = 1.702
SWIGLU_LIMIT = 7.0
EPS = 1e-6

LANES = 128
QB = 128
KB = 256
DQ = 256
NEG = -1e30
INT_MIN = -(2 ** 31)
KEY_NEG_INF = -2139095041
NORM_SLACK = 1.02
L_TINY = 1e-30
VMEM_LIMIT = 48 * 1024 * 1024
POST_TM = 512
MOE_TM = 512
MOE_TOK = 256
DMA_UNROLL = 8
SB_GROUP = 6
SB_WINDOW = 3
SB_CUTOFF = -110.0

f32 = jnp.float32
bf16 = jnp.bfloat16


def _cparams(sem):
    return pltpu.CompilerParams(dimension_semantics=sem, vmem_limit_bytes=VMEM_LIMIT)


def _dot(a, b):
    return jnp.dot(a, b, preferred_element_type=f32)


def _dot_nt(a, b):
    return lax.dot_general(a, b, (((1,), (1,)), ((), ())), preferred_element_type=f32)


def _dot_split(x, m01, passes):
    acc = None
    r = x
    for p in range(passes):
        t = r.astype(bf16)
        d = _dot(t, m01)
        acc = d if acc is None else acc + d
        if p + 1 < passes:
            r = r - t.astype(f32)
    return acc


def _norm_mod(x, g, sc, sh):
    ms = jnp.mean(x * x, axis=-1, keepdims=True)
    return (x * lax.rsqrt(ms + EPS) * g) * (1.0 + sc) + sh


def _mod_kernel(c_ref, w_ref, b_ref, o_ref):
    c = c_ref[...]
    cs = c * (1.0 / (1.0 + jnp.exp(-c)))
    o_ref[0] = jnp.dot(cs, w_ref[0], preferred_element_type=f32,
                       precision=lax.Precision.HIGHEST) + b_ref[0]


def _modulation(c, ada_w, ada_b):
    depth, d, n6 = ada_w.shape
    b = c.shape[0]
    tn = 1024
    return pl.pallas_call(
        _mod_kernel,
        out_shape=jax.ShapeDtypeStruct((depth, b, n6), f32),
        grid=(depth, n6 // tn),
        in_specs=[pl.BlockSpec((b, d), lambda l, j: (0, 0)),
                  pl.BlockSpec((1, d, tn), lambda l, j: (l, 0, j)),
                  pl.BlockSpec((1, 1, tn), lambda l, j: (l, 0, j))],
        out_specs=pl.BlockSpec((1, b, tn), lambda l, j: (l, 0, j)),
        compiler_params=_cparams(("arbitrary", "arbitrary")),
        name="adaln_mod",
    )(c, ada_w, ada_b.reshape(depth, 1, n6))


def _proj_kernel(x_ref, g_ref, sc_ref, sh_ref, w_ref, o_ref):
    h = _norm_mod(x_ref[...], g_ref[...], sc_ref[0], sh_ref[0])
    o_ref[...] = _dot(h.astype(bf16), w_ref[...]).astype(o_ref.dtype)


def _proj(x2, g, sc, sh, w, seq, tm=512):
    n, d = x2.shape
    nc = w.shape[1]
    per = seq // tm
    return pl.pallas_call(
        _proj_kernel,
        out_shape=jax.ShapeDtypeStruct((n, nc), bf16),
        grid=(n // tm,),
        in_specs=[pl.BlockSpec((tm, d), lambda i: (i, 0)),
                  pl.BlockSpec((1, d), lambda i: (0, 0)),
                  pl.BlockSpec((1, 1, d), lambda i: (i // per, 0, 0)),
                  pl.BlockSpec((1, 1, d), lambda i: (i // per, 0, 0)),
                  pl.BlockSpec((d, nc), lambda i: (0, 0))],
        out_specs=pl.BlockSpec((tm, nc), lambda i: (i, 0)),
        compiler_params=_cparams(("arbitrary",)),
        name="sb_qkv_proj",
    )(x2, g, sc, sh, w)


def _sb_kernel(q_ref, k_ref, v_ref, o_ref, k0_ref, k1_ref):
    seq = q_ref.shape[0]
    nqb = seq // QB
    lane = lax.broadcasted_iota(jnp.int32, (1, LANES), 1)
    kk = k_ref[...]
    zero = jnp.zeros_like(kk)
    k0_ref[...] = jnp.where(lane < HEAD_DIM, kk, zero)
    k1_ref[...] = jnp.where(lane >= HEAD_DIM, kk, zero)
    row = lax.broadcasted_iota(jnp.int32, (QB, QB), 0)
    col = lax.broadcasted_iota(jnp.int32, (QB, QB), 1)
    before = col < row
    tri = jnp.where(row > col, 1.0, 0.0).astype(bf16)

    def windows(jobs, nb, diag):
        zs = [_dot_nt(q, kh_ref[pl.ds(k0, nb * QB), :]) for q, kh_ref, k0, _ in jobs]
        lms, lss = [], []
        for z in zs:
            l1p = jnp.log(1.0 + jnp.exp(-jnp.abs(z)))
            lm = jnp.minimum(-z, 0.0) - l1p
            lss.append(lm + z)
            parts = [lm[:, s * QB:(s + 1) * QB] for s in range(nb)]
            if diag:
                parts[-1] = jnp.where(before, parts[-1], 0.0)
            lms.append(parts)
        cums = [[_dot_split(p, tri, 2) for p in parts] for parts in lms]
        outs = []
        for (q, kh_ref, k0, c_in), parts, cum, ls in zip(jobs, lms, cums, lss):
            c = c_in
            probs = [None] * nb
            for s in reversed(range(nb)):
                a = jnp.exp(ls[:, s * QB:(s + 1) * QB] + (cum[s] + c))
                if diag and s == nb - 1:
                    a = jnp.where(before, a, 0.0)
                probs[s] = a.astype(bf16)
                c = c + jnp.sum(parts[s], axis=1, keepdims=True)
            outs.append((c, probs[0] if nb == 1 else jnp.concatenate(probs, axis=1)))
        return [(c, _dot(a_all, v_ref[pl.ds(k0, nb * QB), :]))
                for (c, a_all), (_, _, k0, _) in zip(outs, jobs)]

    def qheads(blocks, nb):
        jobs, meta = [], []
        zc = jnp.zeros((QB, 1), f32)
        for i in blocks:
            q0 = i * QB
            k0 = q0 - (nb - 1) * QB
            if not isinstance(i, int):
                q0 = pl.multiple_of(q0, QB)
                k0 = pl.multiple_of(k0, QB)
            q = q_ref[pl.ds(q0, QB), :] * (HEAD_DIM ** -0.5)
            jobs += [(q, k0_ref, k0, zc), (q, k1_ref, k0, zc)]
            meta.append((q0, q))
        res = windows(jobs, nb, True)
        return [(q0, q) + res[2 * n] + res[2 * n + 1] for n, (q0, q) in enumerate(meta)]

    def qhead(i, nb):
        return qheads([i], nb)[0]

    def qtail(i, nb, st):
        q0, q, c0, a0, c1, a1 = st
        if nb <= SB_WINDOW and not (isinstance(i, int) and i < nb):
            def cond(s):
                jb, c0, _, c1, _ = s
                live = jnp.max(jnp.maximum(c0, c1)) > SB_CUTOFF
                return jnp.logical_and(jb >= 0, live)

            def body(s):
                jb, c0, a0, c1, a1 = s
                kb = pl.multiple_of(jb * QB, QB)
                (c0, d0), (c1, d1) = windows([(q, k0_ref, kb, c0), (q, k1_ref, kb, c1)], 1, False)
                return jb - 1, c0, a0 + d0, c1, a1 + d1

            jb0 = jnp.asarray(i - nb, jnp.int32)
            _, c0, a0, c1, a1 = lax.while_loop(cond, body, (jb0, c0, a0, c1, a1))
        o_ref[pl.ds(q0, QB), :] = jnp.where(lane < HEAD_DIM, a0, a1).astype(o_ref.dtype)

    first = min(SB_WINDOW, nqb)
    npairs = (nqb - first) // SB_GROUP
    singles = list(range(first + SB_GROUP * npairs, nqb))
    for i in range(first):
        if i + 1 == SB_WINDOW and singles:
            continue
        qtail(i, i + 1, qhead(i, i + 1))

    def pair(j, _):
        i = first + SB_GROUP * j
        blocks = [i + n for n in range(SB_GROUP)]
        for b, st in zip(blocks, qheads(blocks, SB_WINDOW)):
            qtail(b, SB_WINDOW, st)
        return 0

    lax.fori_loop(0, npairs, pair, 0)
    if singles:
        group = ([first - 1] if first == SB_WINDOW else []) + singles
        for i, st in zip(group, qheads(group, SB_WINDOW)):
            qtail(i, SB_WINDOW, st)


def _stick_breaking(qkv, batch, seq, d):
    npair = d // LANES
    return pl.pallas_call(
        _sb_kernel,
        out_shape=jax.ShapeDtypeStruct((batch * seq, d), bf16),
        grid=(batch, npair),
        in_specs=[pl.BlockSpec((seq, LANES), lambda b, p: (b, p)),
                  pl.BlockSpec((seq, LANES), lambda b, p: (b, npair + p)),
                  pl.BlockSpec((seq, LANES), lambda b, p: (b, 2 * npair + p))],
        out_specs=pl.BlockSpec((seq, LANES), lambda b, p: (b, p)),
        scratch_shapes=[pltpu.VMEM((seq, LANES), bf16), pltpu.VMEM((seq, LANES), bf16)],
        compiler_params=_cparams(("arbitrary", "arbitrary")),
        name="stick_breaking_attn",
    )(qkv, qkv, qkv)


def _dsa_proj_kernel(x_ref, g_ref, sc_ref, sh_ref, w_ref, pos_ref, inv_ref, qg_ref, kg_ref,
                     bd_ref, o_ref, wi_ref, *, d):
    h = _norm_mod(x_ref[...], g_ref[...], sc_ref[0], sh_ref[0])
    p = _dot(h.astype(bf16), w_ref[...])
    lane = lax.broadcasted_iota(jnp.int32, (1, LANES), 1)
    ang = pos_ref[...].astype(f32) * inv_ref[...]
    cos_t = jnp.cos(ang)
    sin_t = jnp.sin(ang)
    upper = (lane % ROPE_DIMS) >= (ROPE_DIMS // 2)
    s_up = jnp.where(upper, sin_t, 0.0)
    s_lo = jnp.where(upper, 0.0, -sin_t)
    half = ROPE_DIMS // 2

    def rope(y):
        return y * cos_t + pltpu.roll(y, half, 1) * s_up + pltpu.roll(y, LANES - half, 1) * s_lo

    def headnorm(y, gain):
        ms = _dot_split(y * y, bd_ref[...], 2) * (1.0 / HEAD_DIM)
        return y * lax.rsqrt(ms + EPS) * gain

    nq = d // LANES
    for c in range(nq):
        y = p[:, c * LANES:(c + 1) * LANES]
        o_ref[:, c * LANES:(c + 1) * LANES] = rope(headnorm(y, qg_ref[...])).astype(bf16)
    c0 = nq
    for c in range(c0, c0 + IDX_HEADS * IDX_DIM // LANES):
        o_ref[:, c * LANES:(c + 1) * LANES] = rope(p[:, c * LANES:(c + 1) * LANES]).astype(bf16)
    c0 += IDX_HEADS * IDX_DIM // LANES
    y = p[:, c0 * LANES:(c0 + 1) * LANES]
    o_ref[:, c0 * LANES:(c0 + 1) * LANES] = rope(headnorm(y, kg_ref[...])).astype(bf16)
    c0 += 1
    o_ref[:, c0 * LANES:(c0 + 1) * LANES] = p[:, c0 * LANES:(c0 + 1) * LANES].astype(bf16)
    c0 += 1
    y = p[:, c0 * LANES:(c0 + 1) * LANES]
    o_ref[:, c0 * LANES:(c0 + 1) * LANES] = jnp.where(lane < IDX_DIM, rope(y), 0.0).astype(bf16)
    wi = pltpu.roll(y, LANES - IDX_DIM, 1) * (IDX_HEADS ** -0.5) * (IDX_DIM ** -0.5)
    wi_ref[...] = jnp.where(lane < IDX_HEADS, wi, 0.0)


def _dsa_proj(x2, g, sc, sh, w, pos, inv_lane, qg, kg, bd, seq, tm=512):
    n, d = x2.shape
    nc = w.shape[1]
    per = seq // tm
    return pl.pallas_call(
        functools.partial(_dsa_proj_kernel, d=d),
        out_shape=(jax.ShapeDtypeStruct((n, nc), bf16), jax.ShapeDtypeStruct((n, LANES), f32)),
        grid=(n // tm,),
        in_specs=[pl.BlockSpec((tm, d), lambda i: (i, 0)),
                  pl.BlockSpec((1, d), lambda i: (0, 0)),
                  pl.BlockSpec((1, 1, d), lambda i: (i // per, 0, 0)),
                  pl.BlockSpec((1, 1, d), lambda i: (i // per, 0, 0)),
                  pl.BlockSpec((d, nc), lambda i: (0, 0)),
                  pl.BlockSpec((tm, 1), lambda i: (i, 0)),
                  pl.BlockSpec((1, LANES), lambda i: (0, 0)),
                  pl.BlockSpec((1, LANES), lambda i: (0, 0)),
                  pl.BlockSpec((1, LANES), lambda i: (0, 0)),
                  pl.BlockSpec((LANES, LANES), lambda i: (0, 0))],
        out_specs=(pl.BlockSpec((tm, nc), lambda i: (i, 0)),
                   pl.BlockSpec((tm, LANES), lambda i: (i, 0))),
        compiler_params=_cparams(("arbitrary",)),
        name="dsa_in_proj",
    )(x2, g, sc, sh, w, pos, inv_lane, qg, kg, bd)


def _dsa_kernel(q_ref, k_ref, v_ref, qi_ref, ki_ref, wi_ref, o_ref,
                kk_ref, vv_ref, ki2_ref, keys_ref, bias_ref, m_ref, acc_ref, thr_ref, need_ref,
                qs_ref, kmax_ref, *, topk):
    seq = k_ref.shape[0]
    nchunk = q_ref.shape[1] // LANES
    group = (2 * nchunk) // DSA_KV_HEADS
    i = pl.program_id(1)
    lane = lax.broadcasted_iota(jnp.int32, (1, LANES), 1)
    lo = lane < HEAD_DIM

    @pl.when(i == 0)
    def _():
        k = k_ref[...]
        v = v_ref[...]
        ki = ki_ref[...]
        kr = pltpu.roll(k.astype(f32), HEAD_DIM, 1).astype(bf16)
        vr = pltpu.roll(v.astype(f32), HEAD_DIM, 1).astype(bf16)
        zero = jnp.zeros_like(k)
        kk_ref[0] = jnp.where(lo, k, zero)
        kk_ref[1] = jnp.where(lo, zero, kr)
        kk_ref[2] = jnp.where(lo, kr, zero)
        kk_ref[3] = jnp.where(lo, zero, k)
        lane_full = lax.broadcasted_iota(jnp.int32, k.shape, 1)
        oh = [jnp.where(lane_full < HEAD_DIM, 1.0, 0.0).astype(bf16),
              jnp.where(lane_full < HEAD_DIM, 0.0, 1.0).astype(bf16)]
        vv_ref[0] = jnp.concatenate([jnp.where(lo, v, zero), oh[0]], axis=1)
        vv_ref[1] = jnp.concatenate([jnp.where(lo, zero, vr), oh[1]], axis=1)
        vv_ref[2] = jnp.concatenate([jnp.where(lo, vr, zero), oh[0]], axis=1)
        vv_ref[3] = jnp.concatenate([jnp.where(lo, zero, v), oh[1]], axis=1)
        ki2_ref[0] = ki
        ki2_ref[1] = pltpu.roll(ki.astype(f32), HEAD_DIM, 1).astype(bf16)
        kf = k.astype(f32)
        rr = lax.broadcasted_iota(jnp.int32, (LANES, LANES), 0) < HEAD_DIM
        cc = lax.broadcasted_iota(jnp.int32, (LANES, LANES), 1) < HEAD_DIM
        bd = jnp.where(rr == cc, 1.0, 0.0).astype(bf16)
        n2 = jnp.max(_dot((kf * kf).astype(bf16), bd), axis=0, keepdims=True)
        for g in range(DSA_KV_HEADS):
            sel = lo if g == 0 else jnp.logical_not(lo)
            top = jnp.max(jnp.where(sel, n2, 0.0), axis=1, keepdims=True)
            kmax_ref[g] = jnp.broadcast_to(top, (1, LANES))

    nkb = i + 1
    qrow = lax.broadcasted_iota(jnp.int32, (DQ, KB), 0) + i * DQ
    kcol = lax.broadcasted_iota(jnp.int32, (DQ, KB), 1)

    wi = wi_ref[...]
    wcols = [wi[:, hd:hd + 1] for hd in range(IDX_HEADS)]

    def score_block(jb, _):
        k0 = pl.multiple_of(jb * KB, KB)
        score = jnp.zeros((DQ, KB), f32)
        for hd in range(IDX_HEADS):
            qc = qi_ref[:, (hd // 2) * LANES:(hd // 2 + 1) * LANES]
            kb = ki2_ref[hd % 2, pl.ds(k0, KB), :]
            score = score + jnp.maximum(_dot_nt(qc, kb), 0.0) * wcols[hd]
        score = score + 0.0
        bits = pltpu.bitcast(score, jnp.int32)
        key = bits ^ ((bits >> 31) & 0x7FFFFFFF)
        adm = ((kcol + k0) >> CHUNK_SHIFT) <= (qrow >> CHUNK_SHIFT)
        keys_ref[jb] = jnp.where(adm, key, INT_MIN)
        return 0

    lax.fori_loop(0, nkb, score_block, 0)

    kf = float(topk)
    ones_l = jnp.ones((LANES, LANES), bf16)

    def search(nk):
        def count(r, pred):
            part = None
            for jb in range(nk):
                for hl in range(KB // LANES):
                    key = keys_ref[jb, r * QB:(r + 1) * QB, hl * LANES:(hl + 1) * LANES]
                    hit = jnp.where(pred(key), 1.0, 0.0)
                    part = hit if part is None else part + hit
            return _dot(part.astype(bf16), ones_l)

        groups = range(DQ // QB)
        t0 = tuple(jnp.where(count(r, lambda key: key >= 0) >= kf, 0, INT_MIN).astype(jnp.int32)
                   for r in groups)

        def bit_step(s, ts):
            bit = jnp.left_shift(jnp.int32(1), 30 - s)
            out = []
            for r in groups:
                cand = ts[r] + bit
                enough = count(r, lambda key, cand=cand: key >= cand) >= kf
                out.append(jnp.where(enough, cand, ts[r]))
            return tuple(out)

        ts = lax.fori_loop(0, 31, bit_step, t0)
        for r in groups:
            thr_ref[r * QB:(r + 1) * QB, :] = ts[r]
            need_ref[r * QB:(r + 1) * QB, :] = kf - count(r, lambda key, t=ts[r]: key > t)

    for nk in range(1, seq // KB + 1):
        @pl.when(nkb == nk)
        def _(nk=nk):
            search(nk)

    thr = jnp.concatenate([thr_ref[...]] * (KB // LANES), axis=1)
    need = jnp.concatenate([need_ref[...]] * (KB // LANES), axis=1)

    r2 = lax.broadcasted_iota(jnp.int32, (KB, KB), 0)
    c2 = lax.broadcasted_iota(jnp.int32, (KB, KB), 1)
    tri = jnp.where(r2 < c2, 1.0, 0.0).astype(bf16)
    ones_k = jnp.ones((KB, KB), bf16)

    def bias_block(jb, carry):
        key = keys_ref[jb]
        eq = key == thr
        eqb = jnp.where(eq, 1.0, 0.0).astype(bf16)
        rank = _dot(eqb, tri) + carry
        sel = (key > thr) | (eq & (rank < need))
        sel = sel & (key > KEY_NEG_INF)
        bias_ref[jb] = jnp.where(sel, 0.0, NEG)
        return carry + _dot(eqb, ones_k)

    lax.fori_loop(0, nkb, bias_block, jnp.zeros((DQ, KB), f32))

    cpg = group // 2
    for g in range(DSA_KV_HEADS):
        qs_ref[g] = jnp.concatenate(
            [q_ref[:, c * LANES:(c + 1) * LANES] for c in range(g * cpg, (g + 1) * cpg)],
            axis=0) * (HEAD_DIM ** -0.5)

    def scores(g, hf, k0, bias):
        return _dot_nt(qs_ref[g], kk_ref[2 * g + hf, pl.ds(k0, KB), :]) + bias

    def sweep_max():
        m_ref[...] = jnp.full(m_ref.shape, NEG, f32)

        def max_block(jb, _):
            k0 = pl.multiple_of(jb * KB, KB)
            bias = jnp.concatenate([bias_ref[jb]] * cpg, axis=0)
            for g in range(DSA_KV_HEADS):
                for hf in range(2):
                    s = scores(g, hf, k0, bias)
                    fold = jnp.maximum(s[:, :LANES], s[:, LANES:])
                    m_ref[2 * g + hf] = jnp.maximum(m_ref[2 * g + hf], fold)
            return 0

        lax.fori_loop(0, nkb, max_block, 0)
        for h in range(2 * DSA_KV_HEADS):
            m = jnp.max(m_ref[h], axis=1, keepdims=True)
            m_ref[h] = jnp.broadcast_to(m, (cpg * DQ, LANES))

    def sweep_exp():
        acc_ref[...] = jnp.zeros(acc_ref.shape, f32)

        def attn_block(jb, _):
            k0 = pl.multiple_of(jb * KB, KB)
            bias = jnp.concatenate([bias_ref[jb]] * cpg, axis=0)
            for g in range(DSA_KV_HEADS):
                ps = []
                for hf in range(2):
                    m = m_ref[2 * g + hf]
                    s = scores(g, hf, k0, bias)
                    ps.append(jnp.exp(s - jnp.concatenate([m, m], axis=1)).astype(bf16))
                p2 = jnp.concatenate(ps, axis=1)
                v2 = jnp.concatenate([vv_ref[2 * g, pl.ds(k0, KB), :],
                                      vv_ref[2 * g + 1, pl.ds(k0, KB), :]], axis=0)
                acc_ref[g] += _dot(p2, v2)
            return 0

        lax.fori_loop(0, nkb, attn_block, 0)

    def write_out():
        for c in range(nchunk):
            r = acc_ref[c // cpg, (c % cpg) * DQ:(c % cpg + 1) * DQ, :]
            o_ref[:, c * LANES:(c + 1) * LANES] = (r[:, :LANES] / r[:, LANES:]).astype(o_ref.dtype)

    for g in range(DSA_KV_HEADS):
        q = qs_ref[g].astype(f32)
        q2 = (q * q).astype(bf16)
        for hf in range(2):
            half = jnp.where((lax.broadcasted_iota(jnp.int32, (LANES, LANES), 0) < HEAD_DIM)
                             == (hf == 0), 1.0, 0.0).astype(bf16)
            qn2 = _dot(q2, half)
            m_ref[2 * g + hf] = NORM_SLACK * jnp.sqrt(qn2 * kmax_ref[g])
    sweep_exp()
    lmin = jnp.min(acc_ref[:, :, LANES:])

    @pl.when(lmin > L_TINY)
    def _():
        write_out()

    @pl.when(jnp.logical_not(lmin > L_TINY))
    def _():
        sweep_max()
        sweep_exp()
        write_out()


def group_even(d):
    return ((d // HEAD_DIM) // DSA_KV_HEADS) % 2 == 0


def _dsa_attention(proj, wi, batch, seq, d, topk):
    nqb = seq // DQ
    nchunk = d // LANES
    cpg = nchunk // DSA_KV_HEADS
    qiw = IDX_HEADS * IDX_DIM
    assert d % qiw == 0 and group_even(d) and DQ == KB
    qicol = d // qiw
    kcol = nchunk + qiw // LANES
    vcol = kcol + 1
    kicol = kcol + 2
    return pl.pallas_call(
        functools.partial(_dsa_kernel, topk=topk),
        out_shape=jax.ShapeDtypeStruct((batch * seq, d), bf16),
        grid=(batch, nqb),
        in_specs=[pl.BlockSpec((DQ, d), lambda b, i: (b * nqb + i, 0)),
                  pl.BlockSpec((seq, LANES), lambda b, i: (b, kcol)),
                  pl.BlockSpec((seq, LANES), lambda b, i: (b, vcol)),
                  pl.BlockSpec((DQ, IDX_HEADS * IDX_DIM), lambda b, i: (b * nqb + i, qicol)),
                  pl.BlockSpec((seq, LANES), lambda b, i: (b, kicol)),
                  pl.BlockSpec((DQ, LANES), lambda b, i: (b * nqb + i, 0))],
        out_specs=pl.BlockSpec((DQ, d), lambda b, i: (b * nqb + i, 0)),
        scratch_shapes=[pltpu.VMEM((4, seq, LANES), bf16),
                        pltpu.VMEM((4, seq, 2 * LANES), bf16),
                        pltpu.VMEM((2, seq, LANES), bf16),
                        pltpu.VMEM((seq // KB, DQ, KB), jnp.int32),
                        pltpu.VMEM((seq // KB, DQ, KB), f32),
                        pltpu.VMEM((2 * DSA_KV_HEADS, cpg * DQ, LANES), f32),
                        pltpu.VMEM((DSA_KV_HEADS, cpg * DQ, 2 * LANES), f32),
                        pltpu.VMEM((DQ, LANES), jnp.int32),
                        pltpu.VMEM((DQ, LANES), f32),
                        pltpu.VMEM((DSA_KV_HEADS, cpg * DQ, LANES), bf16),
                        pltpu.VMEM((DSA_KV_HEADS, 1, LANES), f32)],
        compiler_params=_cparams(("arbitrary", "arbitrary")),
        name="dsa_sparse_attn",
    )(proj, proj, proj, proj, proj, wi)


def _post_kernel(o_ref, wo_ref, x_ref, g1_ref, n2_ref, sc_ref, sh_ref, rw_ref, rb_ref, tri_ref,
                 xo_ref, hf_ref, idx_ref, gate_ref, rank_ref, cnt_ref, run_ref):
    half = pl.num_programs(0) // 2

    @pl.when((pl.program_id(0) == 0) | (pl.program_id(0) == half))
    def _():
        run_ref[...] = jnp.zeros(run_ref.shape, f32)

    y = _dot(o_ref[...], wo_ref[...])
    x = x_ref[...] + g1_ref[0] * y
    xo_ref[...] = x
    h = _norm_mod(x, n2_ref[...], sc_ref[0], sh_ref[0])
    tm, d = h.shape
    nchunk = d // LANES
    for c in range(nchunk):
        hf_ref[pl.ds(c, tm, stride=nchunk), :] = h[:, c * LANES:(c + 1) * LANES]
    h_hi = h.astype(bf16)
    h_lo = (h - h_hi.astype(f32)).astype(bf16)
    logits = (_dot(h_hi, rw_ref[0]) + _dot(h_hi, rw_ref[1]) + _dot(h_lo, rw_ref[0])) + rb_ref[...]
    lane = lax.broadcasted_iota(jnp.int32, logits.shape, 1).astype(f32)
    work = logits
    vals, idxs, hits = [], [], []
    for _ in range(TOP_K):
        m = jnp.max(work, axis=1, keepdims=True)
        idx = jnp.min(jnp.where(work == m, lane, float(LANES)), axis=1, keepdims=True)
        hit = lane == idx
        vals.append(m)
        idxs.append(idx)
        hits.append(hit)
        work = jnp.where(hit, -jnp.inf, work)
    es = [jnp.exp(v - vals[0]) for v in vals]
    inv = 1.0 / (es[0] + es[1] + es[2] + es[3])
    multi = jnp.zeros_like(logits)
    for hit in hits:
        multi = multi + jnp.where(hit, 1.0, 0.0)
    before = _dot(tri_ref[...], multi.astype(bf16)) + run_ref[...]
    idx_o = jnp.zeros_like(logits)
    gate_o = jnp.zeros_like(logits)
    rank_o = jnp.zeros_like(logits)
    for k in range(TOP_K):
        slot = lane == float(k)
        rank = jnp.sum(jnp.where(hits[k], before, 0.0), axis=1, keepdims=True)
        idx_o = jnp.where(slot, idxs[k], idx_o)
        gate_o = jnp.where(slot, es[k] * inv, gate_o)
        rank_o = jnp.where(slot, rank, rank_o)
    idx_ref[...] = idx_o
    gate_ref[...] = gate_o
    rank_ref[...] = rank_o
    run_ref[...] += jnp.sum(multi, axis=0, keepdims=True)
    cnt_ref[0] = jnp.broadcast_to(run_ref[...], cnt_ref.shape[1:])


def _post_attention(o, wo, x2, g1, n2, sc2, sh2, rw, rb, tri, seq, tm):
    n, d = x2.shape
    per = seq // tm
    nchunk = d // LANES
    row = lambda i: (i, 0)
    fix = lambda i: (0, 0)
    bat = lambda i: (i // per, 0, 0)
    lane_out = jax.ShapeDtypeStruct((n, LANES), f32)
    return pl.pallas_call(
        _post_kernel,
        out_shape=(jax.ShapeDtypeStruct((n, d), f32), jax.ShapeDtypeStruct((n * nchunk, LANES), f32),
                   lane_out, lane_out, lane_out, jax.ShapeDtypeStruct((2, 8, LANES), f32)),
        grid=(n // tm,),
        in_specs=[pl.BlockSpec((tm, d), row), pl.BlockSpec((d, d), fix),
                  pl.BlockSpec((tm, d), row), pl.BlockSpec((1, 1, d), bat),
                  pl.BlockSpec((1, d), fix), pl.BlockSpec((1, 1, d), bat),
                  pl.BlockSpec((1, 1, d), bat), pl.BlockSpec((2, d, LANES), lambda i: (0, 0, 0)),
                  pl.BlockSpec((1, LANES), fix), pl.BlockSpec((tm, tm), fix)],
        out_specs=(pl.BlockSpec((tm, d), row), pl.BlockSpec((tm * nchunk, LANES), row),
                   pl.BlockSpec((tm, LANES), row), pl.BlockSpec((tm, LANES), row),
                   pl.BlockSpec((tm, LANES), row),
                   pl.BlockSpec((1, 8, LANES), lambda i: (i // (n // tm // 2), 0, 0))),
        scratch_shapes=[pltpu.VMEM((1, LANES), f32)],
        compiler_params=_cparams(("arbitrary",)),
        name="attn_out_norm_router",
    )(o, wo, x2, g1, n2, sc2, sh2, rw, rb, tri)


def _deint_kernel(w_ref, wd_ref, p_ref, o_ref, od_ref):
    ff = o_ref.shape[2] // 2
    wide = 2 * LANES
    for b in range(o_ref.shape[2] // wide):
        x = w_ref[0, :, b * wide:(b + 1) * wide].astype(bf16)
        r = _dot(x, p_ref[...]).astype(bf16)
        o_ref[0, :, b * LANES:(b + 1) * LANES] = r[:, :LANES]
        o_ref[0, :, ff + b * LANES:ff + (b + 1) * LANES] = r[:, LANES:]
    od_ref[...] = wd_ref[...].astype(bf16)


def _expert_weight_prep(w, wd):
    ne, d, ff2 = w.shape
    wide = 2 * LANES
    src = jnp.arange(wide)
    perm = jnp.where(src < LANES, 2 * src, 2 * (src - LANES) + 1)
    p = (jnp.arange(wide)[:, None] == perm[None, :]).astype(bf16)
    blk = lambda e: (e, 0, 0)
    return pl.pallas_call(
        _deint_kernel,
        out_shape=(jax.ShapeDtypeStruct((ne, d, ff2), bf16),
                   jax.ShapeDtypeStruct(wd.shape, bf16)),
        grid=(ne,),
        in_specs=[pl.BlockSpec((1, d, ff2), blk), pl.BlockSpec((1,) + wd.shape[1:], blk),
                  pl.BlockSpec((wide, wide), lambda e: (0, 0))],
        out_specs=(pl.BlockSpec((1, d, ff2), blk), pl.BlockSpec((1,) + wd.shape[1:], blk)),
        compiler_params=_cparams(("arbitrary",)),
        name="expert_weight_prep",
    )(w, wd, p)


def _dispatch_kernel(zt_ref, pos_ref, hf_ref, xs_ref, zbuf_ref, ring_ref, sem, zsem,
                     *, t_tok, tm, nchunk):
    i = pl.program_id(0)
    ne = zt_ref.shape[0] - 1
    ntile = xs_ref.shape[0] // (tm * nchunk)

    def zero_tile(e):
        return zt_ref[e] if e < ne else zt_ref[ne] + (e - ne)

    def zero_copy(e):
        start = pl.multiple_of(zero_tile(e) * (tm * nchunk), tm * nchunk)
        return pltpu.make_async_copy(zbuf_ref, xs_ref.at[pl.ds(start, tm * nchunk)], zsem)

    def zero_wanted(e):
        return zero_tile(e) >= 0 if e < ne else zero_tile(e) < ntile

    @pl.when(i == 0)
    def _():
        zbuf_ref[...] = jnp.zeros(zbuf_ref.shape, f32)
        for e in range(2 * ne):
            @pl.when(zero_wanted(e))
            def _():
                zero_copy(e).start()
        for e in range(2 * ne):
            @pl.when(zero_wanted(e))
            def _():
                zero_copy(e).wait()

    slot = i % 2
    ring_ref[slot] = hf_ref[...]

    def issue(tt, _):
        for u in range(DMA_UNROLL):
            t = tt * DMA_UNROLL + u
            src = ring_ref.at[slot, pl.ds(pl.multiple_of(t * nchunk, nchunk), nchunk)]
            for k in range(TOP_K):
                p = pos_ref[0, 0, t * TOP_K + k]
                dst = xs_ref.at[pl.ds(pl.multiple_of(p * nchunk, nchunk), nchunk)]
                pltpu.make_async_copy(src, dst, sem.at[slot]).start()
        return 0

    lax.fori_loop(0, t_tok // DMA_UNROLL, issue, 0)

    def drain(s):
        for k in range(TOP_K):
            pltpu.make_async_copy(ring_ref.at[s], xs_ref.at[pl.ds(0, t_tok * nchunk)],
                                  sem.at[s]).wait()

    @pl.when(i > 0)
    def _():
        drain(1 - slot)

    @pl.when(i == pl.num_programs(0) - 1)
    def _():
        drain(slot)


def _dispatch(hflat, pos3, ztile, rows_pad, tm, t_tok, nchunk):
    ntile = pos3.shape[0]
    return pl.pallas_call(
        functools.partial(_dispatch_kernel, t_tok=t_tok, tm=tm, nchunk=nchunk),
        out_shape=jax.ShapeDtypeStruct((rows_pad * nchunk, LANES), f32),
        grid_spec=pltpu.PrefetchScalarGridSpec(
            num_scalar_prefetch=1,
            grid=(ntile,),
            in_specs=[pl.BlockSpec((1, 1, t_tok * TOP_K), lambda i, zt: (i, 0, 0),
                                   memory_space=pltpu.SMEM),
                      pl.BlockSpec((t_tok * nchunk, LANES), lambda i, zt: (i, 0))],
            out_specs=pl.BlockSpec(memory_space=pl.ANY),
            scratch_shapes=[pltpu.VMEM((tm * nchunk, LANES), f32),
                            pltpu.VMEM((2, t_tok * nchunk, LANES), f32),
                            pltpu.SemaphoreType.DMA((2,)), pltpu.SemaphoreType.DMA]),
        compiler_params=_cparams(("arbitrary",)),
        name="moe_dispatch",
    )(ztile, pos3, hflat)


def _gmm_kernel(te_ref, used_ref, xs_ref, wgu_ref, bgu_ref, wd_ref, bd_ref, y_ref, *, ff, nchunk):
    @pl.when(pl.program_id(0) < used_ref[0])
    def _():
        tm = xs_ref.shape[0] // nchunk
        x = jnp.concatenate([xs_ref[pl.ds(c, tm, stride=nchunk), :] for c in range(nchunk)],
                            axis=1).astype(bf16)
        gu = _dot(x, wgu_ref[0]) + bgu_ref[0]
        g = jnp.minimum(gu[:, :ff], SWIGLU_LIMIT)
        u = jnp.clip(gu[:, ff:], -SWIGLU_LIMIT, SWIGLU_LIMIT)
        act = (u + 1.0) * (g * (1.0 / (1.0 + jnp.exp(-SWIGLU_ALPHA * g))))
        y = _dot(act.astype(bf16), wd_ref[0]) + bd_ref[0]
        for c in range(nchunk):
            y_ref[pl.ds(c, tm, stride=nchunk), :] = y[:, c * LANES:(c + 1) * LANES]

    @pl.when(pl.program_id(0) >= used_ref[0])
    def _():
        y_ref[...] = jnp.zeros(y_ref.shape, f32)


def _grouped_mlp(tile_expert, used, xs, wgu, bgu, wd, bdn, eoff, tm, nchunk):
    _, d, ff2 = wgu.shape
    ntile = tile_expert.shape[0]
    rows = lambda j, te, used: (jnp.minimum(j, used[0] - 1), 0)
    rows_out = lambda j, te, used: (j, 0)
    exp3 = lambda j, te, used: (te[j], 0, 0)
    wexp3 = lambda j, te, used: (te[j] + eoff, 0, 0)
    return pl.pallas_call(
        functools.partial(_gmm_kernel, ff=ff2 // 2, nchunk=nchunk),
        out_shape=jax.ShapeDtypeStruct(xs.shape, f32),
        grid_spec=pltpu.PrefetchScalarGridSpec(
            num_scalar_prefetch=2,
            grid=(ntile,),
            in_specs=[pl.BlockSpec((tm * nchunk, LANES), rows),
                      pl.BlockSpec((1, d, ff2), wexp3), pl.BlockSpec((1, 1, ff2), exp3),
                      pl.BlockSpec((1, ff2 // 2, d), wexp3), pl.BlockSpec((1, 1, d), exp3)],
            out_specs=pl.BlockSpec((tm * nchunk, LANES), rows_out)),
        compiler_params=_cparams(("arbitrary",)),
        name="moe_grouped_mlp",
    )(tile_expert, used, xs, wgu, bgu, wd, bdn)


def _combine_kernel(pos_ref, nxt_ref, y_ref, gate_ref, x_ref, g2_ref, o_ref, buf_ref, sem,
                    *, t_tok, nchunk):
    i = pl.program_id(0)

    def gather(p_ref, slot):
        def issue(tt, _):
            for u in range(DMA_UNROLL):
                t = tt * DMA_UNROLL + u
                for k in range(TOP_K):
                    p = p_ref[0, 0, t * TOP_K + k]
                    src = y_ref.at[pl.ds(pl.multiple_of(p * nchunk, nchunk), nchunk)]
                    row = pl.multiple_of((k * t_tok + t) * nchunk, nchunk)
                    pltpu.make_async_copy(src, buf_ref.at[slot, pl.ds(row, nchunk)],
                                          sem.at[slot]).start()
            return 0

        lax.fori_loop(0, t_tok // DMA_UNROLL, issue, 0)

    @pl.when(i == 0)
    def _():
        gather(pos_ref, 0)

    @pl.when(i + 1 < pl.num_programs(0))
    def _():
        gather(nxt_ref, (i + 1) % 2)

    slot = i % 2
    pltpu.make_async_copy(y_ref.at[pl.ds(0, TOP_K * t_tok * nchunk)], buf_ref.at[slot],
                          sem.at[slot]).wait()
    gates = gate_ref[...]
    g2 = g2_ref[0]
    for c in range(nchunk):
        acc = None
        for k in range(TOP_K):
            rows = buf_ref[slot, pl.ds(k * t_tok * nchunk + c, t_tok, stride=nchunk), :]
            term = gates[:, k:k + 1] * rows
            acc = term if acc is None else acc + term
        sl = slice(c * LANES, (c + 1) * LANES)
        o_ref[:, sl] = x_ref[:, sl] + g2[:, sl] * acc


def _combine(pos3, y, gate, x2, g2, seq, t_tok, nchunk):
    n, d = x2.shape
    per = seq // t_tok
    return pl.pallas_call(
        functools.partial(_combine_kernel, t_tok=t_tok, nchunk=nchunk),
        out_shape=jax.ShapeDtypeStruct((n, d), f32),
        grid=(n // t_tok,),
        in_specs=[pl.BlockSpec((1, 1, t_tok * TOP_K), lambda i: (i, 0, 0), memory_space=pltpu.SMEM),
                  pl.BlockSpec((1, 1, t_tok * TOP_K),
                               lambda i: (jnp.minimum(i + 1, n // t_tok - 1), 0, 0),
                               memory_space=pltpu.SMEM),
                  pl.BlockSpec(memory_space=pl.ANY),
                  pl.BlockSpec((t_tok, LANES), lambda i: (i, 0)),
                  pl.BlockSpec((t_tok, d), lambda i: (i, 0)),
                  pl.BlockSpec((1, 1, d), lambda i: (i // per, 0, 0))],
        out_specs=pl.BlockSpec((t_tok, d), lambda i: (i, 0)),
        scratch_shapes=[pltpu.VMEM((2, TOP_K * t_tok * nchunk, LANES), f32),
                        pltpu.SemaphoreType.DMA((2,))],
        compiler_params=_cparams(("arbitrary",)),
        name="moe_combine",
    )(pos3, pos3, y, gate, x2, g2)


def _dispatch_step(i, nstep, zt_ref, pos_ref, hf_ref, xs_ref, zbuf_ref, ring_ref, sem, zsem,
                   t_tok, tm, nchunk):
    ne = zt_ref.shape[0] - 1
    ntile = xs_ref.shape[0] // (tm * nchunk)

    def zero_tile(e):
        return zt_ref[e] if e < ne else zt_ref[ne] + (e - ne)

    def zero_copy(e):
        start = pl.multiple_of(zero_tile(e) * (tm * nchunk), tm * nchunk)
        return pltpu.make_async_copy(zbuf_ref, xs_ref.at[pl.ds(start, tm * nchunk)], zsem)

    def zero_wanted(e):
        return zero_tile(e) >= 0 if e < ne else zero_tile(e) < ntile

    @pl.when(i == 0)
    def _():
        zbuf_ref[...] = jnp.zeros(zbuf_ref.shape, f32)
        for e in range(2 * ne):
            @pl.when(zero_wanted(e))
            def _():
                zero_copy(e).start()
        for e in range(2 * ne):
            @pl.when(zero_wanted(e))
            def _():
                zero_copy(e).wait()

    slot = i % 2
    ring_ref[slot] = hf_ref[...]

    def issue(tt, _):
        for u in range(DMA_UNROLL):
            t = tt * DMA_UNROLL + u
            src = ring_ref.at[slot, pl.ds(pl.multiple_of(t * nchunk, nchunk), nchunk)]
            for k in range(TOP_K):
                p = pos_ref[0, 0, t * TOP_K + k]
                dst = xs_ref.at[pl.ds(pl.multiple_of(p * nchunk, nchunk), nchunk)]
                pltpu.make_async_copy(src, dst, sem.at[slot]).start()
        return 0

    lax.fori_loop(0, t_tok // DMA_UNROLL, issue, 0)

    def drain(s):
        for k in range(TOP_K):
            pltpu.make_async_copy(ring_ref.at[s], xs_ref.at[pl.ds(0, t_tok * nchunk)],
                                  sem.at[s]).wait()

    @pl.when(i > 0)
    def _():
        drain(1 - slot)

    @pl.when(i == nstep - 1)
    def _():
        drain(slot)


def _combine_step(i, nstep, pos_ref, nxt_ref, y_ref, gate_ref, x_ref, g2_ref, o_ref, buf_ref, sem,
                  t_tok, nchunk):
    def gather(p_ref, slot):
        def issue(tt, _):
            for u in range(DMA_UNROLL):
                t = tt * DMA_UNROLL + u
                for k in range(TOP_K):
                    p = p_ref[0, 0, t * TOP_K + k]
                    src = y_ref.at[pl.ds(pl.multiple_of(p * nchunk, nchunk), nchunk)]
                    row = pl.multiple_of((k * t_tok + t) * nchunk, nchunk)
                    pltpu.make_async_copy(src, buf_ref.at[slot, pl.ds(row, nchunk)],
                                          sem.at[slot]).start()
            return 0

        lax.fori_loop(0, t_tok // DMA_UNROLL, issue, 0)

    @pl.when(i == 0)
    def _():
        gather(pos_ref, 0)

    @pl.when(i + 1 < nstep)
    def _():
        gather(nxt_ref, (i + 1) % 2)

    slot = i % 2
    pltpu.make_async_copy(y_ref.at[pl.ds(0, TOP_K * t_tok * nchunk)], buf_ref.at[slot],
                          sem.at[slot]).wait()
    gates = gate_ref[...]
    g2 = g2_ref[0]
    for c in range(nchunk):
        acc = None
        for k in range(TOP_K):
            rows = buf_ref[slot, pl.ds(k * t_tok * nchunk + c, t_tok, stride=nchunk), :]
            term = gates[:, k:k + 1] * rows
            acc = term if acc is None else acc + term
        sl = slice(c * LANES, (c + 1) * LANES)
        o_ref[:, sl] = x_ref[:, sl] + g2[:, sl] * acc


def _experts_step(j, used_ref, xs_ref, wgu_ref, bgu_ref, wd_ref, bd_ref, y_ref, ff, nchunk):
    @pl.when(j < used_ref[0])
    def _():
        tm = xs_ref.shape[0] // nchunk
        x = jnp.concatenate([xs_ref[pl.ds(c, tm, stride=nchunk), :] for c in range(nchunk)],
                            axis=1).astype(bf16)
        gu = _dot(x, wgu_ref[0]) + bgu_ref[0]
        g = jnp.minimum(gu[:, :ff], SWIGLU_LIMIT)
        u = jnp.clip(gu[:, ff:], -SWIGLU_LIMIT, SWIGLU_LIMIT)
        act = (u + 1.0) * (g * (1.0 / (1.0 + jnp.exp(-SWIGLU_ALPHA * g))))
        y = _dot(act.astype(bf16), wd_ref[0]) + bd_ref[0]
        for c in range(nchunk):
            y_ref[pl.ds(c, tm, stride=nchunk), :] = y[:, c * LANES:(c + 1) * LANES]

    @pl.when(j >= used_ref[0])
    def _():
        y_ref[...] = jnp.zeros(y_ref.shape, f32)


def _moe_stage_kernel(*refs, has_main, side, nside, t_tok, tm, nchunk, ff):
    it = iter(refs)
    take = lambda k: [next(it) for _ in range(k)]
    j = pl.program_id(0)
    if has_main:
        _, used_ref = take(2)
    if side == "dispatch":
        (zt_ref,) = take(1)
    if has_main:
        main_in = take(5)
    if side == "dispatch":
        side_in = take(2)
    elif side == "combine":
        side_in = take(6)
    if has_main:
        (y_ref,) = take(1)
    (side_out,) = take(1)
    scratch = list(it)

    if side == "dispatch":
        @pl.when(j < nside)
        def _():
            _dispatch_step(j, nside, zt_ref, *side_in, side_out, *scratch, t_tok, tm, nchunk)
    else:
        @pl.when(j < nside)
        def _():
            _combine_step(j, nside, *side_in, side_out, *scratch, t_tok, nchunk)
    if has_main:
        _experts_step(j, used_ref, *main_in, y_ref, ff, nchunk)


def _moe_stage(main, side, kind, *, seq, tm, t_tok, nchunk):
    has_main = main is not None
    nside = side["pos3"].shape[0]
    prefetch, in_specs, operands, out_specs, out_shape, scratch = [], [], [], [], [], []
    smem = pltpu.SMEM
    off = side["off"]
    clamp = lambda j: jnp.minimum(j, nside - 1)
    ff2 = d = None
    if has_main:
        _, d, ff2 = main["wgu"].shape
        eoff = main["eoff"]
        prefetch += [main["te"], main["used"]]
    if kind == "dispatch":
        prefetch += [side["zt"]]
    npre = len(prefetch)

    def spec(shape, fn, **kw):
        return pl.BlockSpec(shape, lambda j, *pre: fn(j, pre), **kw)

    if has_main:
        in_specs += [spec((tm * nchunk, LANES), lambda j, p: (jnp.minimum(j, p[1][0] - 1), 0)),
                     spec((1, d, ff2), lambda j, p: (p[0][j] + eoff, 0, 0)),
                     spec((1, 1, ff2), lambda j, p: (p[0][j], 0, 0)),
                     spec((1, ff2 // 2, d), lambda j, p: (p[0][j] + eoff, 0, 0)),
                     spec((1, 1, d), lambda j, p: (p[0][j], 0, 0))]
        operands += [main["xs"], main["wgu"], main["bgu"], main["wd"], main["bd"]]
    tokens = t_tok * TOP_K
    if kind == "dispatch":
        in_specs += [spec((1, 1, tokens), lambda j, p: (clamp(j), 0, 0), memory_space=smem),
                     spec((t_tok * nchunk, LANES), lambda j, p: (clamp(j) + off, 0))]
        operands += [side["pos3"], side["hflat"]]
    else:
        n, dd = side["x2"].shape
        per = seq // t_tok
        in_specs += [spec((1, 1, tokens), lambda j, p: (clamp(j), 0, 0), memory_space=smem),
                     spec((1, 1, tokens), lambda j, p: (clamp(j + 1), 0, 0), memory_space=smem),
                     pl.BlockSpec(memory_space=pl.ANY),
                     spec((t_tok, LANES), lambda j, p: (clamp(j) + off, 0)),
                     spec((t_tok, dd), lambda j, p: (clamp(j) + off, 0)),
                     spec((1, 1, dd), lambda j, p: ((clamp(j) + off) // per, 0, 0))]
        operands += [side["pos3"], side["pos3"], side["ys"], side["gate"], side["x2"], side["g2"]]
        x_operand = npre + len(operands) - 2
    if has_main:
        out_specs += [spec((tm * nchunk, LANES), lambda j, p: (j, 0))]
        out_shape += [jax.ShapeDtypeStruct(main["xs"].shape, f32)]
    aliases = {}
    if kind == "dispatch":
        out_specs += [pl.BlockSpec(memory_space=pl.ANY)]
        out_shape += [jax.ShapeDtypeStruct((side["rows_pad"] * nchunk, LANES), f32)]
        scratch += [pltpu.VMEM((tm * nchunk, LANES), f32), pltpu.VMEM((2, t_tok * nchunk, LANES), f32),
                    pltpu.SemaphoreType.DMA((2,)), pltpu.SemaphoreType.DMA]
    else:
        out_specs += [spec((t_tok, dd), lambda j, p: (clamp(j) + off, 0))]
        out_shape += [jax.ShapeDtypeStruct((n, dd), f32)]
        scratch += [pltpu.VMEM((2, TOP_K * t_tok * nchunk, LANES), f32),
                    pltpu.SemaphoreType.DMA((2,))]
        aliases = {x_operand: len(out_shape) - 1}
    grid = (main["te"].shape[0],) if has_main else (nside,)
    outs = pl.pallas_call(
        functools.partial(_moe_stage_kernel, has_main=has_main, side=kind, nside=nside,
                          t_tok=t_tok, tm=tm, nchunk=nchunk, ff=None if ff2 is None else ff2 // 2),
        out_shape=tuple(out_shape),
        grid_spec=pltpu.PrefetchScalarGridSpec(
            num_scalar_prefetch=npre, grid=grid, in_specs=in_specs, out_specs=tuple(out_specs),
            scratch_shapes=scratch),
        input_output_aliases=aliases,
        compiler_params=_cparams(("arbitrary",)),
        name=("moe_experts_" if has_main else "moe_") + kind,
    )(*prefetch, *operands)
    return outs


def _moe_plan(idx, rank, cnt, ne, tm, ntile_max):
    counts = cnt[:ne].astype(jnp.int32)
    ntile_e = (counts + tm - 1) // tm
    tile_end = jnp.cumsum(ntile_e)
    tile_start = tile_end - ntile_e
    used = tile_end[-1]
    e_idx = idx[:, :TOP_K].astype(jnp.int32)
    onehot = e_idx[:, :, None] == jnp.arange(ne, dtype=jnp.int32)[None, None, :]
    start = jnp.sum(jnp.where(onehot, (tile_start * tm)[None, None, :], 0), axis=-1)
    pos = start + rank[:, :TOP_K].astype(jnp.int32)
    tiles = jnp.minimum(jnp.arange(ntile_max, dtype=jnp.int32), used - 1)
    tile_expert = jnp.sum(tile_end[None, :] <= tiles[:, None], axis=1).astype(jnp.int32)
    ztile = jnp.where(ntile_e > 0, tile_end - 1, -1)
    ztile = jnp.concatenate([ztile, used.reshape(1)]).astype(jnp.int32)
    return pos, tile_expert, used.reshape(1).astype(jnp.int32), ztile


def _lane_tile(v):
    return jnp.tile(v.astype(f32), LANES // v.shape[0]).reshape(1, LANES)


def kernel(x, c, positions, ada_w, ada_b, norm1_g, norm2_g, sb_w_qkv, sb_w_o, dsa_w_in,
           dsa_q_gain, dsa_k_gain, dsa_w_o, router_w, router_b, exp_w_gu, exp_b_gu,
           exp_w_down, exp_b_down):
    batch, seq, d = x.shape
    depth = ada_w.shape[0]
    n = batch * seq
    ne = router_w.shape[-1]
    ff = exp_w_down.shape[2]
    topk = min(TOPK_MAX, seq // 4)
    assert seq % KB == 0 and d % LANES == 0 and ne <= LANES

    mod = _modulation(c, ada_w, ada_b)
    x2 = x.reshape(n, d)

    ncols = dsa_w_in.shape[-1]
    ncols_pad = -(-ncols // LANES) * LANES
    half = ROPE_DIMS // 2
    inv = jnp.exp(-math.log(ROPE_THETA) * (2.0 * jnp.arange(half, dtype=f32) / ROPE_DIMS))
    l64 = jnp.arange(LANES) % HEAD_DIM
    inv_lane = jnp.where(l64 < ROPE_DIMS, inv[l64 % half], 0.0).reshape(1, LANES)
    bd = (jnp.arange(LANES)[:, None] // HEAD_DIM == jnp.arange(LANES)[None, :] // HEAD_DIM)
    bd = bd.astype(bf16)
    pos = positions.reshape(n, 1)

    nchunk = d // LANES
    half_tok = n // 2
    nside = half_tok // MOE_TOK
    ntile_half = (half_tok * TOP_K) // MOE_TM + ne
    assert (half_tok * TOP_K) % MOE_TM == 0 and half_tok % MOE_TOK == 0 and seq % MOE_TOK == 0
    assert half_tok % POST_TM == 0
    tri = (jnp.arange(POST_TM)[:, None] > jnp.arange(POST_TM)[None, :]).astype(bf16)
    wgu_all, wd_all = _expert_weight_prep(exp_w_gu.reshape(depth * ne, d, 2 * ff),
                                          exp_w_down.reshape(depth * ne, ff, d))

    for layer in range(depth):
        m6 = [mod[layer][:, k * d:(k + 1) * d].reshape(batch, 1, d) for k in range(6)]
        sh1, sc1, g1, sh2, sc2, g2 = m6
        n1 = norm1_g[layer].reshape(1, d)
        n2 = norm2_g[layer].reshape(1, d)
        j = layer // 2
        if layer % 2 == 0:
            qkv = _proj(x2, n1, sc1, sh1, sb_w_qkv[j].astype(bf16), seq)
            o = _stick_breaking(qkv, batch, seq, d)
            wo = sb_w_o[j].astype(bf16)
        else:
            w = dsa_w_in[j]
            kv0 = d
            qi0 = d + 2 * LANES
            ki0 = qi0 + IDX_HEADS * IDX_DIM
            w = jnp.concatenate([w[:, :kv0], w[:, qi0:ki0], w[:, kv0:qi0], w[:, ki0:]], axis=1)
            w = jnp.pad(w, ((0, 0), (0, ncols_pad - ncols))).astype(bf16)
            proj, wi = _dsa_proj(x2, n1, sc1, sh1, w, pos, inv_lane, _lane_tile(dsa_q_gain[j]),
                                 _lane_tile(dsa_k_gain[j]), bd, seq)
            o = _dsa_attention(proj, wi, batch, seq, d, topk)
            wo = dsa_w_o[j].astype(bf16)
        rw = jnp.pad(router_w[layer], ((0, 0), (0, LANES - ne)))
        rw_hi = rw.astype(bf16)
        rw = jnp.stack([rw_hi, (rw - rw_hi.astype(f32)).astype(bf16)])
        rb = jnp.pad(router_b[layer], (0, LANES - ne), constant_values=-jnp.inf).reshape(1, LANES)
        x2, hflat, idx, gate, rank, cnt = _post_attention(o, wo, x2, g1, n2, sc2, sh2, rw, rb,
                                                          tri, seq, POST_TM)
        bgu = exp_b_gu[layer]
        bgu = jnp.concatenate([bgu[..., 0::2], bgu[..., 1::2]], axis=-1).reshape(ne, 1, 2 * ff)
        halves = []
        for hh in range(2):
            tok = slice(hh * half_tok, (hh + 1) * half_tok)
            spos, te, used, zt = _moe_plan(idx[tok], rank[tok], cnt[hh, 0], ne, MOE_TM, ntile_half)
            halves.append(dict(
                pos3=spos.reshape(nside, 1, MOE_TOK * TOP_K), zt=zt, off=hh * nside,
                experts=dict(te=te, used=used, wgu=wgu_all, bgu=bgu, wd=wd_all,
                             bd=exp_b_down[layer].reshape(ne, 1, d), eoff=layer * ne)))
        stage = functools.partial(_moe_stage, seq=seq, tm=MOE_TM, t_tok=MOE_TOK, nchunk=nchunk)

        def disp(h):
            return dict(zt=h["zt"], pos3=h["pos3"], hflat=hflat, off=h["off"],
                        rows_pad=ntile_half * MOE_TM)

        def comb(h, ys, xin):
            return dict(pos3=h["pos3"], ys=ys, gate=gate, x2=xin, g2=g2, off=h["off"])

        ha, hb = halves
        (xs_a,) = stage(None, disp(ha), "dispatch")
        ys_a, xs_b = stage(dict(ha["experts"], xs=xs_a), disp(hb), "dispatch")
        ys_b, x2 = stage(dict(hb["experts"], xs=xs_b), comb(ha, ys_a, x2), "combine")
        (x2,) = stage(None, comb(hb, ys_b, x2), "combine")
    return x2.reshape(batch, seq, d)
```

```python
import functools
import math

import jax
import jax.numpy as jnp
from jax import lax
from jax.experimental import pallas as pl
from jax.experimental.pallas import tpu as pltpu

HEAD_DIM = 64
DSA_KV_HEADS = 2
IDX_HEADS = 8
IDX_DIM = 64
CHUNK_SHIFT = 6
TOPK_MAX = 256
TOP_K = 4
ROPE_THETA = 500000.0
ROPE_DIMS = HEAD_DIM // 4
SWIGLU_ALPHA = 1.702
SWIGLU_LIMIT = 7.0
EPS = 1e-6

LANES = 128
QB = 128
KB = 256
DQ = 256
NEG = -1e30
INT_MIN = -(2 ** 31)
KEY_NEG_INF = -2139095041
NORM_SLACK = 1.02
L_TINY = 1e-30
VMEM_LIMIT = 48 * 1024 * 1024
POST_TM = 512
MOE_TM = 512
MOE_TOK = 256
ROW_DMA_PRIORITY = 1
DMA_UNROLL = 8
SB_GROUP = 6
SB_WINDOW = 3
SB_CUTOFF = -110.0

f32 = jnp.float32
bf16 = jnp.bfloat16


def _cparams(sem):
    return pltpu.CompilerParams(dimension_semantics=sem, vmem_limit_bytes=VMEM_LIMIT)


def _dot(a, b):
    return jnp.dot(a, b, preferred_element_type=f32)


def _dot_nt(a, b):
    return lax.dot_general(a, b, (((1,), (1,)), ((), ())), preferred_element_type=f32)


def _dot_split(x, m01, passes):
    acc = None
    r = x
    for p in range(passes):
        t = r.astype(bf16)
        d = _dot(t, m01)
        acc = d if acc is None else acc + d
        if p + 1 < passes:
            r = r - t.astype(f32)
    return acc


def _norm_mod(x, g, sc, sh):
    ms = jnp.mean(x * x, axis=-1, keepdims=True)
    return (x * lax.rsqrt(ms + EPS) * g) * (1.0 + sc) + sh


def _mod_kernel(c_ref, w_ref, b_ref, o_ref):
    c = c_ref[...]
    cs = c * (1.0 / (1.0 + jnp.exp(-c)))
    o_ref[0] = jnp.dot(cs, w_ref[0], preferred_element_type=f32,
                       precision=lax.Precision.HIGHEST) + b_ref[0]


def _modulation(c, ada_w, ada_b):
    depth, d, n6 = ada_w.shape
    b = c.shape[0]
    tn = 1024
    return pl.pallas_call(
        _mod_kernel,
        out_shape=jax.ShapeDtypeStruct((depth, b, n6), f32),
        grid=(depth, n6 // tn),
        in_specs=[pl.BlockSpec((b, d), lambda l, j: (0, 0)),
                  pl.BlockSpec((1, d, tn), lambda l, j: (l, 0, j)),
                  pl.BlockSpec((1, 1, tn), lambda l, j: (l, 0, j))],
        out_specs=pl.BlockSpec((1, b, tn), lambda l, j: (l, 0, j)),
        compiler_params=_cparams(("arbitrary", "arbitrary")),
        name="adaln_mod",
    )(c, ada_w, ada_b.reshape(depth, 1, n6))


def _proj_kernel(x_ref, g_ref, sc_ref, sh_ref, w_ref, o_ref):
    h = _norm_mod(x_ref[...], g_ref[...], sc_ref[0], sh_ref[0])
    o_ref[...] = _dot(h.astype(bf16), w_ref[...]).astype(o_ref.dtype)


def _proj(x2, g, sc, sh, w, seq, tm=512):
    n, d = x2.shape
    nc = w.shape[1]
    per = seq // tm
    return pl.pallas_call(
        _proj_kernel,
        out_shape=jax.ShapeDtypeStruct((n, nc), bf16),
        grid=(n // tm,),
        in_specs=[pl.BlockSpec((tm, d), lambda i: (i, 0)),
                  pl.BlockSpec((1, d), lambda i: (0, 0)),
                  pl.BlockSpec((1, 1, d), lambda i: (i // per, 0, 0)),
                  pl.BlockSpec((1, 1, d), lambda i: (i // per, 0, 0)),
                  pl.BlockSpec((d, nc), lambda i: (0, 0))],
        out_specs=pl.BlockSpec((tm, nc), lambda i: (i, 0)),
        compiler_params=_cparams(("arbitrary",)),
        name="sb_qkv_proj",
    )(x2, g, sc, sh, w)


def _sb_kernel(q_ref, k_ref, v_ref, o_ref, k0_ref, k1_ref):
    seq = q_ref.shape[0]
    nqb = seq // QB
    lane = lax.broadcasted_iota(jnp.int32, (1, LANES), 1)
    kk = k_ref[...]
    zero = jnp.zeros_like(kk)
    k0_ref[...] = jnp.where(lane < HEAD_DIM, kk, zero)
    k1_ref[...] = jnp.where(lane >= HEAD_DIM, kk, zero)
    row = lax.broadcasted_iota(jnp.int32, (QB, QB), 0)
    col = lax.broadcasted_iota(jnp.int32, (QB, QB), 1)
    before = col < row
    tri = jnp.where(row > col, 1.0, 0.0).astype(bf16)

    def windows(jobs, nb, diag):
        zs = [_dot_nt(q, kh_ref[pl.ds(k0, nb * QB), :]) for q, kh_ref, k0, _ in jobs]
        lms, lss = [], []
        for z in zs:
            l1p = jnp.log(1.0 + jnp.exp(-jnp.abs(z)))
            lm = jnp.minimum(-z, 0.0) - l1p
            lss.append(lm + z)
            parts = [lm[:, s * QB:(s + 1) * QB] for s in range(nb)]
            if diag:
                parts[-1] = jnp.where(before, parts[-1], 0.0)
            lms.append(parts)
        cums = [[_dot_split(p, tri, 2) for p in parts] for parts in lms]
        outs = []
        for (q, kh_ref, k0, c_in), parts, cum, ls in zip(jobs, lms, cums, lss):
            c = c_in
            probs = [None] * nb
            for s in reversed(range(nb)):
                a = jnp.exp(ls[:, s * QB:(s + 1) * QB] + (cum[s] + c))
                if diag and s == nb - 1:
                    a = jnp.where(before, a, 0.0)
                probs[s] = a.astype(bf16)
                c = c + jnp.sum(parts[s], axis=1, keepdims=True)
            outs.append((c, probs[0] if nb == 1 else jnp.concatenate(probs, axis=1)))
        return [(c, _dot(a_all, v_ref[pl.ds(k0, nb * QB), :]))
                for (c, a_all), (_, _, k0, _) in zip(outs, jobs)]

    def qheads(blocks, nb):
        jobs, meta = [], []
        zc = jnp.zeros((QB, 1), f32)
        for i in blocks:
            q0 = i * QB
            k0 = q0 - (nb - 1) * QB
            if not isinstance(i, int):
                q0 = pl.multiple_of(q0, QB)
                k0 = pl.multiple_of(k0, QB)
            q = q_ref[pl.ds(q0, QB), :] * (HEAD_DIM ** -0.5)
            jobs += [(q, k0_ref, k0, zc), (q, k1_ref, k0, zc)]
            meta.append((q0, q))
        res = windows(jobs, nb, True)
        return [(q0, q) + res[2 * n] + res[2 * n + 1] for n, (q0, q) in enumerate(meta)]

    def qhead(i, nb):
        return qheads([i], nb)[0]

    def qtail(i, nb, st):
        q0, q, c0, a0, c1, a1 = st
        if nb <= SB_WINDOW and not (isinstance(i, int) and i < nb):
            def cond(s):
                jb, c0, _, c1, _ = s
                live = jnp.max(jnp.maximum(c0, c1)) > SB_CUTOFF
                return jnp.logical_and(jb >= 0, live)

            def body(s):
                jb, c0, a0, c1, a1 = s
                kb = pl.multiple_of(jb * QB, QB)
                (c0, d0), (c1, d1) = windows([(q, k0_ref, kb, c0), (q, k1_ref, kb, c1)], 1, False)
                return jb - 1, c0, a0 + d0, c1, a1 + d1

            jb0 = jnp.asarray(i - nb, jnp.int32)
            _, c0, a0, c1, a1 = lax.while_loop(cond, body, (jb0, c0, a0, c1, a1))
        o_ref[pl.ds(q0, QB), :] = jnp.where(lane < HEAD_DIM, a0, a1).astype(o_ref.dtype)

    first = min(SB_WINDOW, nqb)
    npairs = (nqb - first) // SB_GROUP
    singles = list(range(first + SB_GROUP * npairs, nqb))
    for i in range(first):
        if i + 1 == SB_WINDOW and singles:
            continue
        qtail(i, i + 1, qhead(i, i + 1))

    def pair(j, _):
        i = first + SB_GROUP * j
        blocks = [i + n for n in range(SB_GROUP)]
        for b, st in zip(blocks, qheads(blocks, SB_WINDOW)):
            qtail(b, SB_WINDOW, st)
        return 0

    lax.fori_loop(0, npairs, pair, 0)
    if singles:
        group = ([first - 1] if first == SB_WINDOW else []) + singles
        for i, st in zip(group, qheads(group, SB_WINDOW)):
            qtail(i, SB_WINDOW, st)


def _stick_breaking(qkv, batch, seq, d):
    npair = d // LANES
    return pl.pallas_call(
        _sb_kernel,
        out_shape=jax.ShapeDtypeStruct((batch * seq, d), bf16),
        grid=(batch, npair),
        in_specs=[pl.BlockSpec((seq, LANES), lambda b, p: (b, p)),
                  pl.BlockSpec((seq, LANES), lambda b, p: (b, npair + p)),
                  pl.BlockSpec((seq, LANES), lambda b, p: (b, 2 * npair + p))],
        out_specs=pl.BlockSpec((seq, LANES), lambda b, p: (b, p)),
        scratch_shapes=[pltpu.VMEM((seq, LANES), bf16), pltpu.VMEM((seq, LANES), bf16)],
        compiler_params=_cparams(("arbitrary", "arbitrary")),
        name="stick_breaking_attn",
    )(qkv, qkv, qkv)


def _dsa_proj_kernel(x_ref, g_ref, sc_ref, sh_ref, w_ref, pos_ref, inv_ref, qg_ref, kg_ref,
                     bd_ref, o_ref, wi_ref, *, d):
    h = _norm_mod(x_ref[...], g_ref[...], sc_ref[0], sh_ref[0])
    p = _dot(h.astype(bf16), w_ref[...])
    lane = lax.broadcasted_iota(jnp.int32, (1, LANES), 1)
    ang = pos_ref[...].astype(f32) * inv_ref[...]
    cos_t = jnp.cos(ang)
    sin_t = jnp.sin(ang)
    upper = (lane % ROPE_DIMS) >= (ROPE_DIMS // 2)
    s_up = jnp.where(upper, sin_t, 0.0)
    s_lo = jnp.where(upper, 0.0, -sin_t)
    half = ROPE_DIMS // 2

    def rope(y):
        return y * cos_t + pltpu.roll(y, half, 1) * s_up + pltpu.roll(y, LANES - half, 1) * s_lo

    def headnorm(y, gain):
        ms = _dot_split(y * y, bd_ref[...], 2) * (1.0 / HEAD_DIM)
        return y * lax.rsqrt(ms + EPS) * gain

    nq = d // LANES
    for c in range(nq):
        y = p[:, c * LANES:(c + 1) * LANES]
        o_ref[:, c * LANES:(c + 1) * LANES] = rope(headnorm(y, qg_ref[...])).astype(bf16)
    c0 = nq
    for c in range(c0, c0 + IDX_HEADS * IDX_DIM // LANES):
        o_ref[:, c * LANES:(c + 1) * LANES] = rope(p[:, c * LANES:(c + 1) * LANES]).astype(bf16)
    c0 += IDX_HEADS * IDX_DIM // LANES
    y = p[:, c0 * LANES:(c0 + 1) * LANES]
    o_ref[:, c0 * LANES:(c0 + 1) * LANES] = rope(headnorm(y, kg_ref[...])).astype(bf16)
    c0 += 1
    o_ref[:, c0 * LANES:(c0 + 1) * LANES] = p[:, c0 * LANES:(c0 + 1) * LANES].astype(bf16)
    c0 += 1
    y = p[:, c0 * LANES:(c0 + 1) * LANES]
    o_ref[:, c0 * LANES:(c0 + 1) * LANES] = jnp.where(lane < IDX_DIM, rope(y), 0.0).astype(bf16)
    wi = pltpu.roll(y, LANES - IDX_DIM, 1) * (IDX_HEADS ** -0.5) * (IDX_DIM ** -0.5)
    wi_ref[...] = jnp.where(lane < IDX_HEADS, wi, 0.0)


def _dsa_proj(x2, g, sc, sh, w, pos, inv_lane, qg, kg, bd, seq, tm=512):
    n, d = x2.shape
    nc = w.shape[1]
    per = seq // tm
    return pl.pallas_call(
        functools.partial(_dsa_proj_kernel, d=d),
        out_shape=(jax.ShapeDtypeStruct((n, nc), bf16), jax.ShapeDtypeStruct((n, LANES), f32)),
        grid=(n // tm,),
        in_specs=[pl.BlockSpec((tm, d), lambda i: (i, 0)),
                  pl.BlockSpec((1, d), lambda i: (0, 0)),
                  pl.BlockSpec((1, 1, d), lambda i: (i // per, 0, 0)),
                  pl.BlockSpec((1, 1, d), lambda i: (i // per, 0, 0)),
                  pl.BlockSpec((d, nc), lambda i: (0, 0)),
                  pl.BlockSpec((tm, 1), lambda i: (i, 0)),
                  pl.BlockSpec((1, LANES), lambda i: (0, 0)),
                  pl.BlockSpec((1, LANES), lambda i: (0, 0)),
                  pl.BlockSpec((1, LANES), lambda i: (0, 0)),
                  pl.BlockSpec((LANES, LANES), lambda i: (0, 0))],
        out_specs=(pl.BlockSpec((tm, nc), lambda i: (i, 0)),
                   pl.BlockSpec((tm, LANES), lambda i: (i, 0))),
        compiler_params=_cparams(("arbitrary",)),
        name="dsa_in_proj",
    )(x2, g, sc, sh, w, pos, inv_lane, qg, kg, bd)


def _dsa_kernel(q_ref, k_ref, v_ref, qi_ref, ki_ref, wi_ref, o_ref,
                kk_ref, vv_ref, ki2_ref, keys_ref, bias_ref, m_ref, acc_ref, thr_ref, need_ref,
                qs_ref, kmax_ref, *, topk):
    seq = k_ref.shape[0]
    nchunk = q_ref.shape[1] // LANES
    group = (2 * nchunk) // DSA_KV_HEADS
    i = pl.program_id(1)
    lane = lax.broadcasted_iota(jnp.int32, (1, LANES), 1)
    lo = lane < HEAD_DIM

    @pl.when(i == 0)
    def _():
        k = k_ref[...]
        v = v_ref[...]
        ki = ki_ref[...]
        kr = pltpu.roll(k.astype(f32), HEAD_DIM, 1).astype(bf16)
        vr = pltpu.roll(v.astype(f32), HEAD_DIM, 1).astype(bf16)
        zero = jnp.zeros_like(k)
        kk_ref[0] = jnp.where(lo, k, zero)
        kk_ref[1] = jnp.where(lo, zero, kr)
        kk_ref[2] = jnp.where(lo, kr, zero)
        kk_ref[3] = jnp.where(lo, zero, k)
        lane_full = lax.broadcasted_iota(jnp.int32, k.shape, 1)
        oh = [jnp.where(lane_full < HEAD_DIM, 1.0, 0.0).astype(bf16),
              jnp.where(lane_full < HEAD_DIM, 0.0, 1.0).astype(bf16)]
        vv_ref[0] = jnp.concatenate([jnp.where(lo, v, zero), oh[0]], axis=1)
        vv_ref[1] = jnp.concatenate([jnp.where(lo, zero, vr), oh[1]], axis=1)
        vv_ref[2] = jnp.concatenate([jnp.where(lo, vr, zero), oh[0]], axis=1)
        vv_ref[3] = jnp.concatenate([jnp.where(lo, zero, v), oh[1]], axis=1)
        ki2_ref[0] = ki
        ki2_ref[1] = pltpu.roll(ki.astype(f32), HEAD_DIM, 1).astype(bf16)
        kf = k.astype(f32)
        rr = lax.broadcasted_iota(jnp.int32, (LANES, LANES), 0) < HEAD_DIM
        cc = lax.broadcasted_iota(jnp.int32, (LANES, LANES), 1) < HEAD_DIM
        bd = jnp.where(rr == cc, 1.0, 0.0).astype(bf16)
        n2 = jnp.max(_dot((kf * kf).astype(bf16), bd), axis=0, keepdims=True)
        for g in range(DSA_KV_HEADS):
            sel = lo if g == 0 else jnp.logical_not(lo)
            top = jnp.max(jnp.where(sel, n2, 0.0), axis=1, keepdims=True)
            kmax_ref[g] = jnp.broadcast_to(top, (1, LANES))

    nkb = i + 1
    qrow = lax.broadcasted_iota(jnp.int32, (DQ, KB), 0) + i * DQ
    kcol = lax.broadcasted_iota(jnp.int32, (DQ, KB), 1)

    wi = wi_ref[...]
    wcols = [wi[:, hd:hd + 1] for hd in range(IDX_HEADS)]

    def score_block(jb, _):
        k0 = pl.multiple_of(jb * KB, KB)
        score = jnp.zeros((DQ, KB), f32)
        for hd in range(IDX_HEADS):
            qc = qi_ref[:, (hd // 2) * LANES:(hd // 2 + 1) * LANES]
            kb = ki2_ref[hd % 2, pl.ds(k0, KB), :]
            score = score + jnp.maximum(_dot_nt(qc, kb), 0.0) * wcols[hd]
        score = score + 0.0
        bits = pltpu.bitcast(score, jnp.int32)
        key = bits ^ ((bits >> 31) & 0x7FFFFFFF)
        adm = ((kcol + k0) >> CHUNK_SHIFT) <= (qrow >> CHUNK_SHIFT)
        keys_ref[jb] = jnp.where(adm, key, INT_MIN)
        return 0

    lax.fori_loop(0, nkb, score_block, 0)

    kf = float(topk)
    ones_l = jnp.ones((LANES, LANES), bf16)

    def search(nk):
        def count(r, pred):
            part = None
            for jb in range(nk):
                for hl in range(KB // LANES):
                    key = keys_ref[jb, r * QB:(r + 1) * QB, hl * LANES:(hl + 1) * LANES]
                    hit = jnp.where(pred(key), 1.0, 0.0)
                    part = hit if part is None else part + hit
            return _dot(part.astype(bf16), ones_l)

        groups = range(DQ // QB)
        t0 = tuple(jnp.where(count(r, lambda key: key >= 0) >= kf, 0, INT_MIN).astype(jnp.int32)
                   for r in groups)

        def bit_step(s, ts):
            bit = jnp.left_shift(jnp.int32(1), 30 - s)
            out = []
            for r in groups:
                cand = ts[r] + bit
                enough = count(r, lambda key, cand=cand: key >= cand) >= kf
                out.append(jnp.where(enough, cand, ts[r]))
            return tuple(out)

        ts = lax.fori_loop(0, 31, bit_step, t0)
        for r in groups:
            thr_ref[r * QB:(r + 1) * QB, :] = ts[r]
            need_ref[r * QB:(r + 1) * QB, :] = kf - count(r, lambda key, t=ts[r]: key > t)

    for nk in range(1, seq // KB + 1):
        @pl.when(nkb == nk)
        def _(nk=nk):
            search(nk)

    thr = jnp.concatenate([thr_ref[...]] * (KB // LANES), axis=1)
    need = jnp.concatenate([need_ref[...]] * (KB // LANES), axis=1)

    r2 = lax.broadcasted_iota(jnp.int32, (KB, KB), 0)
    c2 = lax.broadcasted_iota(jnp.int32, (KB, KB), 1)
    tri = jnp.where(r2 < c2, 1.0, 0.0).astype(bf16)
    ones_k = jnp.ones((KB, KB), bf16)

    def bias_block(jb, carry):
        key = keys_ref[jb]
        eq = key == thr
        eqb = jnp.where(eq, 1.0, 0.0).astype(bf16)
        rank = _dot(eqb, tri) + carry
        sel = (key > thr) | (eq & (rank < need))
        sel = sel & (key > KEY_NEG_INF)
        bias_ref[jb] = jnp.where(sel, 0.0, NEG)
        return carry + _dot(eqb, ones_k)

    lax.fori_loop(0, nkb, bias_block, jnp.zeros((DQ, KB), f32))

    cpg = group // 2
    for g in range(DSA_KV_HEADS):
        qs_ref[g] = jnp.concatenate(
            [q_ref[:, c * LANES:(c + 1) * LANES] for c in range(g * cpg, (g + 1) * cpg)],
            axis=0) * (HEAD_DIM ** -0.5)

    def scores(g, hf, k0, bias):
        return _dot_nt(qs_ref[g], kk_ref[2 * g + hf, pl.ds(k0, KB), :]) + bias

    def sweep_max():
        m_ref[...] = jnp.full(m_ref.shape, NEG, f32)

        def max_block(jb, _):
            k0 = pl.multiple_of(jb * KB, KB)
            bias = jnp.concatenate([bias_ref[jb]] * cpg, axis=0)
            for g in range(DSA_KV_HEADS):
                for hf in range(2):
                    s = scores(g, hf, k0, bias)
                    fold = jnp.maximum(s[:, :LANES], s[:, LANES:])
                    m_ref[2 * g + hf] = jnp.maximum(m_ref[2 * g + hf], fold)
            return 0

        lax.fori_loop(0, nkb, max_block, 0)
        for h in range(2 * DSA_KV_HEADS):
            m = jnp.max(m_ref[h], axis=1, keepdims=True)
            m_ref[h] = jnp.broadcast_to(m, (cpg * DQ, LANES))

    def sweep_exp():
        acc_ref[...] = jnp.zeros(acc_ref.shape, f32)

        def attn_block(jb, _):
            k0 = pl.multiple_of(jb * KB, KB)
            bias = jnp.concatenate([bias_ref[jb]] * cpg, axis=0)
            for g in range(DSA_KV_HEADS):
                ps = []
                for hf in range(2):
                    m = m_ref[2 * g + hf]
                    s = scores(g, hf, k0, bias)
                    ps.append(jnp.exp(s - jnp.concatenate([m, m], axis=1)).astype(bf16))
                p2 = jnp.concatenate(ps, axis=1)
                v2 = jnp.concatenate([vv_ref[2 * g, pl.ds(k0, KB), :],
                                      vv_ref[2 * g + 1, pl.ds(k0, KB), :]], axis=0)
                acc_ref[g] += _dot(p2, v2)
            return 0

        lax.fori_loop(0, nkb, attn_block, 0)

    def write_out():
        for c in range(nchunk):
            r = acc_ref[c // cpg, (c % cpg) * DQ:(c % cpg + 1) * DQ, :]
            o_ref[:, c * LANES:(c + 1) * LANES] = (r[:, :LANES] / r[:, LANES:]).astype(o_ref.dtype)

    for g in range(DSA_KV_HEADS):
        q = qs_ref[g].astype(f32)
        q2 = (q * q).astype(bf16)
        for hf in range(2):
            half = jnp.where((lax.broadcasted_iota(jnp.int32, (LANES, LANES), 0) < HEAD_DIM)
                             == (hf == 0), 1.0, 0.0).astype(bf16)
            qn2 = _dot(q2, half)
            m_ref[2 * g + hf] = NORM_SLACK * jnp.sqrt(qn2 * kmax_ref[g])
    sweep_exp()
    lmin = jnp.min(acc_ref[:, :, LANES:])

    @pl.when(lmin > L_TINY)
    def _():
        write_out()

    @pl.when(jnp.logical_not(lmin > L_TINY))
    def _():
        sweep_max()
        sweep_exp()
        write_out()


def group_even(d):
    return ((d // HEAD_DIM) // DSA_KV_HEADS) % 2 == 0


def _dsa_attention(proj, wi, batch, seq, d, topk):
    nqb = seq // DQ
    nchunk = d // LANES
    cpg = nchunk // DSA_KV_HEADS
    qiw = IDX_HEADS * IDX_DIM
    assert d % qiw == 0 and group_even(d) and DQ == KB
    qicol = d // qiw
    kcol = nchunk + qiw // LANES
    vcol = kcol + 1
    kicol = kcol + 2
    return pl.pallas_call(
        functools.partial(_dsa_kernel, topk=topk),
        out_shape=jax.ShapeDtypeStruct((batch * seq, d), bf16),
        grid=(batch, nqb),
        in_specs=[pl.BlockSpec((DQ, d), lambda b, i: (b * nqb + i, 0)),
                  pl.BlockSpec((seq, LANES), lambda b, i: (b, kcol)),
                  pl.BlockSpec((seq, LANES), lambda b, i: (b, vcol)),
                  pl.BlockSpec((DQ, IDX_HEADS * IDX_DIM), lambda b, i: (b * nqb + i, qicol)),
                  pl.BlockSpec((seq, LANES), lambda b, i: (b, kicol)),
                  pl.BlockSpec((DQ, LANES), lambda b, i: (b * nqb + i, 0))],
        out_specs=pl.BlockSpec((DQ, d), lambda b, i: (b * nqb + i, 0)),
        scratch_shapes=[pltpu.VMEM((4, seq, LANES), bf16),
                        pltpu.VMEM((4, seq, 2 * LANES), bf16),
                        pltpu.VMEM((2, seq, LANES), bf16),
                        pltpu.VMEM((seq // KB, DQ, KB), jnp.int32),
                        pltpu.VMEM((seq // KB, DQ, KB), f32),
                        pltpu.VMEM((2 * DSA_KV_HEADS, cpg * DQ, LANES), f32),
                        pltpu.VMEM((DSA_KV_HEADS, cpg * DQ, 2 * LANES), f32),
                        pltpu.VMEM((DQ, LANES), jnp.int32),
                        pltpu.VMEM((DQ, LANES), f32),
                        pltpu.VMEM((DSA_KV_HEADS, cpg * DQ, LANES), bf16),
                        pltpu.VMEM((DSA_KV_HEADS, 1, LANES), f32)],
        compiler_params=_cparams(("arbitrary", "arbitrary")),
        name="dsa_sparse_attn",
    )(proj, proj, proj, proj, proj, wi)


def _post_kernel(o_ref, wo_ref, x_ref, g1_ref, n2_ref, sc_ref, sh_ref, rw_ref, rb_ref, tri_ref,
                 xo_ref, hf_ref, idx_ref, gate_ref, rank_ref, cnt_ref, run_ref):
    half = pl.num_programs(0) // 2

    @pl.when((pl.program_id(0) == 0) | (pl.program_id(0) == half))
    def _():
        run_ref[...] = jnp.zeros(run_ref.shape, f32)

    y = _dot(o_ref[...], wo_ref[...])
    x = x_ref[...] + g1_ref[0] * y
    xo_ref[...] = x
    h = _norm_mod(x, n2_ref[...], sc_ref[0], sh_ref[0])
    tm, d = h.shape
    nchunk = d // LANES
    for c in range(nchunk):
        hf_ref[pl.ds(c, tm, stride=nchunk), :] = h[:, c * LANES:(c + 1) * LANES]
    h_hi = h.astype(bf16)
    h_lo = (h - h_hi.astype(f32)).astype(bf16)
    logits = (_dot(h_hi, rw_ref[0]) + _dot(h_hi, rw_ref[1]) + _dot(h_lo, rw_ref[0])) + rb_ref[...]
    lane = lax.broadcasted_iota(jnp.int32, logits.shape, 1).astype(f32)
    work = logits
    vals, idxs, hits = [], [], []
    for _ in range(TOP_K):
        m = jnp.max(work, axis=1, keepdims=True)
        idx = jnp.min(jnp.where(work == m, lane, float(LANES)), axis=1, keepdims=True)
        hit = lane == idx
        vals.append(m)
        idxs.append(idx)
        hits.append(hit)
        work = jnp.where(hit, -jnp.inf, work)
    es = [jnp.exp(v - vals[0]) for v in vals]
    inv = 1.0 / (es[0] + es[1] + es[2] + es[3])
    multi = jnp.zeros_like(logits)
    for hit in hits:
        multi = multi + jnp.where(hit, 1.0, 0.0)
    before = _dot(tri_ref[...], multi.astype(bf16)) + run_ref[...]
    idx_o = jnp.zeros_like(logits)
    gate_o = jnp.zeros_like(logits)
    rank_o = jnp.zeros_like(logits)
    for k in range(TOP_K):
        slot = lane == float(k)
        rank = jnp.sum(jnp.where(hits[k], before, 0.0), axis=1, keepdims=True)
        idx_o = jnp.where(slot, idxs[k], idx_o)
        gate_o = jnp.where(slot, es[k] * inv, gate_o)
        rank_o = jnp.where(slot, rank, rank_o)
    idx_ref[...] = idx_o
    gate_ref[...] = gate_o
    rank_ref[...] = rank_o
    run_ref[...] += jnp.sum(multi, axis=0, keepdims=True)
    cnt_ref[0] = jnp.broadcast_to(run_ref[...], cnt_ref.shape[1:])


def _post_attention(o, wo, x2, g1, n2, sc2, sh2, rw, rb, tri, seq, tm):
    n, d = x2.shape
    per = seq // tm
    nchunk = d // LANES
    row = lambda i: (i, 0)
    fix = lambda i: (0, 0)
    bat = lambda i: (i // per, 0, 0)
    lane_out = jax.ShapeDtypeStruct((n, LANES), f32)
    return pl.pallas_call(
        _post_kernel,
        out_shape=(jax.ShapeDtypeStruct((n, d), f32), jax.ShapeDtypeStruct((n * nchunk, LANES), f32),
                   lane_out, lane_out, lane_out, jax.ShapeDtypeStruct((2, 8, LANES), f32)),
        grid=(n // tm,),
        in_specs=[pl.BlockSpec((tm, d), row), pl.BlockSpec((d, d), fix),
                  pl.BlockSpec((tm, d), row), pl.BlockSpec((1, 1, d), bat),
                  pl.BlockSpec((1, d), fix), pl.BlockSpec((1, 1, d), bat),
                  pl.BlockSpec((1, 1, d), bat), pl.BlockSpec((2, d, LANES), lambda i: (0, 0, 0)),
                  pl.BlockSpec((1, LANES), fix), pl.BlockSpec((tm, tm), fix)],
        out_specs=(pl.BlockSpec((tm, d), row), pl.BlockSpec((tm * nchunk, LANES), row),
                   pl.BlockSpec((tm, LANES), row), pl.BlockSpec((tm, LANES), row),
                   pl.BlockSpec((tm, LANES), row),
                   pl.BlockSpec((1, 8, LANES), lambda i: (i // (n // tm // 2), 0, 0))),
        scratch_shapes=[pltpu.VMEM((1, LANES), f32)],
        compiler_params=_cparams(("arbitrary",)),
        name="attn_out_norm_router",
    )(o, wo, x2, g1, n2, sc2, sh2, rw, rb, tri)


def _deint_kernel(w_ref, wd_ref, p_ref, o_ref, od_ref):
    ff = o_ref.shape[2] // 2
    wide = 2 * LANES
    for b in range(o_ref.shape[2] // wide):
        x = w_ref[0, :, b * wide:(b + 1) * wide].astype(bf16)
        r = _dot(x, p_ref[...]).astype(bf16)
        o_ref[0, :, b * LANES:(b + 1) * LANES] = r[:, :LANES]
        o_ref[0, :, ff + b * LANES:ff + (b + 1) * LANES] = r[:, LANES:]
    od_ref[...] = wd_ref[...].astype(bf16)


def _expert_weight_prep(w, wd):
    ne, d, ff2 = w.shape
    wide = 2 * LANES
    src = jnp.arange(wide)
    perm = jnp.where(src < LANES, 2 * src, 2 * (src - LANES) + 1)
    p = (jnp.arange(wide)[:, None] == perm[None, :]).astype(bf16)
    blk = lambda e: (e, 0, 0)
    return pl.pallas_call(
        _deint_kernel,
        out_shape=(jax.ShapeDtypeStruct((ne, d, ff2), bf16),
                   jax.ShapeDtypeStruct(wd.shape, bf16)),
        grid=(ne,),
        in_specs=[pl.BlockSpec((1, d, ff2), blk), pl.BlockSpec((1,) + wd.shape[1:], blk),
                  pl.BlockSpec((wide, wide), lambda e: (0, 0))],
        out_specs=(pl.BlockSpec((1, d, ff2), blk), pl.BlockSpec((1,) + wd.shape[1:], blk)),
        compiler_params=_cparams(("arbitrary",)),
        name="expert_weight_prep",
    )(w, wd, p)


def _dispatch_kernel(zt_ref, pos_ref, hf_ref, xs_ref, zbuf_ref, ring_ref, sem, zsem,
                     *, t_tok, tm, nchunk):
    i = pl.program_id(0)
    ne = zt_ref.shape[0] - 1
    ntile = xs_ref.shape[0] // (tm * nchunk)

    def zero_tile(e):
        return zt_ref[e] if e < ne else zt_ref[ne] + (e - ne)

    def zero_copy(e):
        start = pl.multiple_of(zero_tile(e) * (tm * nchunk), tm * nchunk)
        return pltpu.make_async_copy(zbuf_ref, xs_ref.at[pl.ds(start, tm * nchunk)], zsem)

    def zero_wanted(e):
        return zero_tile(e) >= 0 if e < ne else zero_tile(e) < ntile

    @pl.when(i == 0)
    def _():
        zbuf_ref[...] = jnp.zeros(zbuf_ref.shape, f32)
        for e in range(2 * ne):
            @pl.when(zero_wanted(e))
            def _():
                zero_copy(e).start()
        for e in range(2 * ne):
            @pl.when(zero_wanted(e))
            def _():
                zero_copy(e).wait()

    slot = i % 2
    ring_ref[slot] = hf_ref[...]

    def issue(tt, _):
        for u in range(DMA_UNROLL):
            t = tt * DMA_UNROLL + u
            src = ring_ref.at[slot, pl.ds(pl.multiple_of(t * nchunk, nchunk), nchunk)]
            for k in range(TOP_K):
                p = pos_ref[0, 0, t * TOP_K + k]
                dst = xs_ref.at[pl.ds(pl.multiple_of(p * nchunk, nchunk), nchunk)]
                pltpu.make_async_copy(src, dst, sem.at[slot]).start()
        return 0

    lax.fori_loop(0, t_tok // DMA_UNROLL, issue, 0)

    def drain(s):
        for k in range(TOP_K):
            pltpu.make_async_copy(ring_ref.at[s], xs_ref.at[pl.ds(0, t_tok * nchunk)],
                                  sem.at[s]).wait()

    @pl.when(i > 0)
    def _():
        drain(1 - slot)

    @pl.when(i == pl.num_programs(0) - 1)
    def _():
        drain(slot)


def _dispatch(hflat, pos3, ztile, rows_pad, tm, t_tok, nchunk):
    ntile = pos3.shape[0]
    return pl.pallas_call(
        functools.partial(_dispatch_kernel, t_tok=t_tok, tm=tm, nchunk=nchunk),
        out_shape=jax.ShapeDtypeStruct((rows_pad * nchunk, LANES), f32),
        grid_spec=pltpu.PrefetchScalarGridSpec(
            num_scalar_prefetch=1,
            grid=(ntile,),
            in_specs=[pl.BlockSpec((1, 1, t_tok * TOP_K), lambda i, zt: (i, 0, 0),
                                   memory_space=pltpu.SMEM),
                      pl.BlockSpec((t_tok * nchunk, LANES), lambda i, zt: (i, 0))],
            out_specs=pl.BlockSpec(memory_space=pl.ANY),
            scratch_shapes=[pltpu.VMEM((tm * nchunk, LANES), f32),
                            pltpu.VMEM((2, t_tok * nchunk, LANES), f32),
                            pltpu.SemaphoreType.DMA((2,)), pltpu.SemaphoreType.DMA]),
        compiler_params=_cparams(("arbitrary",)),
        name="moe_dispatch",
    )(ztile, pos3, hflat)


def _gmm_kernel(te_ref, used_ref, xs_ref, wgu_ref, bgu_ref, wd_ref, bd_ref, y_ref, *, ff, nchunk):
    @pl.when(pl.program_id(0) < used_ref[0])
    def _():
        tm = xs_ref.shape[0] // nchunk
        x = jnp.concatenate([xs_ref[pl.ds(c, tm, stride=nchunk), :] for c in range(nchunk)],
                            axis=1).astype(bf16)
        gu = _dot(x, wgu_ref[0]) + bgu_ref[0]
        g = jnp.minimum(gu[:, :ff], SWIGLU_LIMIT)
        u = jnp.clip(gu[:, ff:], -SWIGLU_LIMIT, SWIGLU_LIMIT)
        act = (u + 1.0) * (g * (1.0 / (1.0 + jnp.exp(-SWIGLU_ALPHA * g))))
        y = _dot(act.astype(bf16), wd_ref[0]) + bd_ref[0]
        for c in range(nchunk):
            y_ref[pl.ds(c, tm, stride=nchunk), :] = y[:, c * LANES:(c + 1) * LANES]

    @pl.when(pl.program_id(0) >= used_ref[0])
    def _():
        y_ref[...] = jnp.zeros(y_ref.shape, f32)


def _grouped_mlp(tile_expert, used, xs, wgu, bgu, wd, bdn, eoff, tm, nchunk):
    _, d, ff2 = wgu.shape
    ntile = tile_expert.shape[0]
    rows = lambda j, te, used: (jnp.minimum(j, used[0] - 1), 0)
    rows_out = lambda j, te, used: (j, 0)
    exp3 = lambda j, te, used: (te[j], 0, 0)
    wexp3 = lambda j, te, used: (te[j] + eoff, 0, 0)
    return pl.pallas_call(
        functools.partial(_gmm_kernel, ff=ff2 // 2, nchunk=nchunk),
        out_shape=jax.ShapeDtypeStruct(xs.shape, f32),
        grid_spec=pltpu.PrefetchScalarGridSpec(
            num_scalar_prefetch=2,
            grid=(ntile,),
            in_specs=[pl.BlockSpec((tm * nchunk, LANES), rows),
                      pl.BlockSpec((1, d, ff2), wexp3), pl.BlockSpec((1, 1, ff2), exp3),
                      pl.BlockSpec((1, ff2 // 2, d), wexp3), pl.BlockSpec((1, 1, d), exp3)],
            out_specs=pl.BlockSpec((tm * nchunk, LANES), rows_out)),
        compiler_params=_cparams(("arbitrary",)),
        name="moe_grouped_mlp",
    )(tile_expert, used, xs, wgu, bgu, wd, bdn)


def _combine_kernel(pos_ref, nxt_ref, y_ref, gate_ref, x_ref, g2_ref, o_ref, buf_ref, sem,
                    *, t_tok, nchunk):
    i = pl.program_id(0)

    def gather(p_ref, slot):
        def issue(tt, _):
            for u in range(DMA_UNROLL):
                t = tt * DMA_UNROLL + u
                for k in range(TOP_K):
                    p = p_ref[0, 0, t * TOP_K + k]
                    src = y_ref.at[pl.ds(pl.multiple_of(p * nchunk, nchunk), nchunk)]
                    row = pl.multiple_of((k * t_tok + t) * nchunk, nchunk)
                    pltpu.make_async_copy(src, buf_ref.at[slot, pl.ds(row, nchunk)],
                                          sem.at[slot]).start()
            return 0

        lax.fori_loop(0, t_tok // DMA_UNROLL, issue, 0)

    @pl.when(i == 0)
    def _():
        gather(pos_ref, 0)

    @pl.when(i + 1 < pl.num_programs(0))
    def _():
        gather(nxt_ref, (i + 1) % 2)

    slot = i % 2
    pltpu.make_async_copy(y_ref.at[pl.ds(0, TOP_K * t_tok * nchunk)], buf_ref.at[slot],
                          sem.at[slot]).wait()
    gates = gate_ref[...]
    g2 = g2_ref[0]
    for c in range(nchunk):
        acc = None
        for k in range(TOP_K):
            rows = buf_ref[slot, pl.ds(k * t_tok * nchunk + c, t_tok, stride=nchunk), :]
            term = gates[:, k:k + 1] * rows
            acc = term if acc is None else acc + term
        sl = slice(c * LANES, (c + 1) * LANES)
        o_ref[:, sl] = x_ref[:, sl] + g2[:, sl] * acc


def _combine(pos3, y, gate, x2, g2, seq, t_tok, nchunk):
    n, d = x2.shape
    per = seq // t_tok
    return pl.pallas_call(
        functools.partial(_combine_kernel, t_tok=t_tok, nchunk=nchunk),
        out_shape=jax.ShapeDtypeStruct((n, d), f32),
        grid=(n // t_tok,),
        in_specs=[pl.BlockSpec((1, 1, t_tok * TOP_K), lambda i: (i, 0, 0), memory_space=pltpu.SMEM),
                  pl.BlockSpec((1, 1, t_tok * TOP_K),
                               lambda i: (jnp.minimum(i + 1, n // t_tok - 1), 0, 0),
                               memory_space=pltpu.SMEM),
                  pl.BlockSpec(memory_space=pl.ANY),
                  pl.BlockSpec((t_tok, LANES), lambda i: (i, 0)),
                  pl.BlockSpec((t_tok, d), lambda i: (i, 0)),
                  pl.BlockSpec((1, 1, d), lambda i: (i // per, 0, 0))],
        out_specs=pl.BlockSpec((t_tok, d), lambda i: (i, 0)),
        scratch_shapes=[pltpu.VMEM((2, TOP_K * t_tok * nchunk, LANES), f32),
                        pltpu.SemaphoreType.DMA((2,))],
        compiler_params=_cparams(("arbitrary",)),
        name="moe_combine",
    )(pos3, pos3, y, gate, x2, g2)


def _dispatch_step(i, nstep, zt_ref, pos_ref, hf_ref, xs_ref, zbuf_ref, ring_ref, sem, zsem,
                   t_tok, tm, nchunk):
    ne = zt_ref.shape[0] - 1
    ntile = xs_ref.shape[0] // (tm * nchunk)

    def zero_tile(e):
        return zt_ref[e] if e < ne else zt_ref[ne] + (e - ne)

    def zero_copy(e):
        start = pl.multiple_of(zero_tile(e) * (tm * nchunk), tm * nchunk)
        return pltpu.make_async_copy(zbuf_ref, xs_ref.at[pl.ds(start, tm * nchunk)], zsem)

    def zero_wanted(e):
        return zero_tile(e) >= 0 if e < ne else zero_tile(e) < ntile

    @pl.when(i == 0)
    def _():
        zbuf_ref[...] = jnp.zeros(zbuf_ref.shape, f32)
        for e in range(2 * ne):
            @pl.when(zero_wanted(e))
            def _():
                zero_copy(e).start()
        for e in range(2 * ne):
            @pl.when(zero_wanted(e))
            def _():
                zero_copy(e).wait()

    slot = i % 2
    ring_ref[slot] = hf_ref[...]

    def issue(tt, _):
        for u in range(DMA_UNROLL):
            t = tt * DMA_UNROLL + u
            src = ring_ref.at[slot, pl.ds(pl.multiple_of(t * nchunk, nchunk), nchunk)]
            for k in range(TOP_K):
                p = pos_ref[0, 0, t * TOP_K + k]
                dst = xs_ref.at[pl.ds(pl.multiple_of(p * nchunk, nchunk), nchunk)]
                pltpu.make_async_copy(src, dst, sem.at[slot]).start(priority=ROW_DMA_PRIORITY)
        return 0

    lax.fori_loop(0, t_tok // DMA_UNROLL, issue, 0)

    def drain(s):
        for k in range(TOP_K):
            pltpu.make_async_copy(ring_ref.at[s], xs_ref.at[pl.ds(0, t_tok * nchunk)],
                                  sem.at[s]).wait()

    @pl.when(i > 0)
    def _():
        drain(1 - slot)

    @pl.when(i == nstep - 1)
    def _():
        drain(slot)


def _combine_step(i, nstep, pos_ref, nxt_ref, y_ref, gate_ref, x_ref, g2_ref, o_ref, buf_ref, sem,
                  t_tok, nchunk):
    def gather(p_ref, slot):
        def issue(tt, _):
            for u in range(DMA_UNROLL):
                t = tt * DMA_UNROLL + u
                for k in range(TOP_K):
                    p = p_ref[0, 0, t * TOP_K + k]
                    src = y_ref.at[pl.ds(pl.multiple_of(p * nchunk, nchunk), nchunk)]
                    row = pl.multiple_of((k * t_tok + t) * nchunk, nchunk)
                    pltpu.make_async_copy(src, buf_ref.at[slot, pl.ds(row, nchunk)],
                                          sem.at[slot]).start(priority=ROW_DMA_PRIORITY)
            return 0

        lax.fori_loop(0, t_tok // DMA_UNROLL, issue, 0)

    @pl.when(i == 0)
    def _():
        gather(pos_ref, 0)

    @pl.when(i + 1 < nstep)
    def _():
        gather(nxt_ref, (i + 1) % 2)

    slot = i % 2
    pltpu.make_async_copy(y_ref.at[pl.ds(0, TOP_K * t_tok * nchunk)], buf_ref.at[slot],
                          sem.at[slot]).wait()
    gates = gate_ref[...]
    g2 = g2_ref[0]
    for c in range(nchunk):
        acc = None
        for k in range(TOP_K):
            rows = buf_ref[slot, pl.ds(k * t_tok * nchunk + c, t_tok, stride=nchunk), :]
            term = gates[:, k:k + 1] * rows
            acc = term if acc is None else acc + term
        sl = slice(c * LANES, (c + 1) * LANES)
        o_ref[:, sl] = x_ref[:, sl] + g2[:, sl] * acc


def _experts_step(j, used_ref, xs_ref, wgu_ref, bgu_ref, wd_ref, bd_ref, y_ref, ff, nchunk):
    @pl.when(j < used_ref[0])
    def _():
        tm = xs_ref.shape[0] // nchunk
        x = jnp.concatenate([xs_ref[pl.ds(c, tm, stride=nchunk), :] for c in range(nchunk)],
                            axis=1).astype(bf16)
        gu = _dot(x, wgu_ref[0]) + bgu_ref[0]
        g = jnp.minimum(gu[:, :ff], SWIGLU_LIMIT)
        u = jnp.clip(gu[:, ff:], -SWIGLU_LIMIT, SWIGLU_LIMIT)
        act = (u + 1.0) * (g * (1.0 / (1.0 + jnp.exp(-SWIGLU_ALPHA * g))))
        y = _dot(act.astype(bf16), wd_ref[0]) + bd_ref[0]
        for c in range(nchunk):
            y_ref[pl.ds(c, tm, stride=nchunk), :] = y[:, c * LANES:(c + 1) * LANES]

    @pl.when(j >= used_ref[0])
    def _():
        y_ref[...] = jnp.zeros(y_ref.shape, f32)


def _moe_stage_kernel(*refs, has_main, side, nside, t_tok, tm, nchunk, ff):
    it = iter(refs)
    take = lambda k: [next(it) for _ in range(k)]
    j = pl.program_id(0)
    if has_main:
        _, used_ref = take(2)
    if side == "dispatch":
        (zt_ref,) = take(1)
    if has_main:
        main_in = take(5)
    if side == "dispatch":
        side_in = take(2)
    elif side == "combine":
        side_in = take(6)
    if has_main:
        (y_ref,) = take(1)
    (side_out,) = take(1)
    scratch = list(it)

    if side == "dispatch":
        @pl.when(j < nside)
        def _():
            _dispatch_step(j, nside, zt_ref, *side_in, side_out, *scratch, t_tok, tm, nchunk)
    else:
        @pl.when(j < nside)
        def _():
            _combine_step(j, nside, *side_in, side_out, *scratch, t_tok, nchunk)
    if has_main:
        _experts_step(j, used_ref, *main_in, y_ref, ff, nchunk)


def _moe_stage(main, side, kind, *, seq, tm, t_tok, nchunk):
    has_main = main is not None
    nside = side["pos3"].shape[0]
    prefetch, in_specs, operands, out_specs, out_shape, scratch = [], [], [], [], [], []
    smem = pltpu.SMEM
    off = side["off"]
    clamp = lambda j: jnp.minimum(j, nside - 1)
    ff2 = d = None
    if has_main:
        _, d, ff2 = main["wgu"].shape
        eoff = main["eoff"]
        prefetch += [main["te"], main["used"]]
    if kind == "dispatch":
        prefetch += [side["zt"]]
    npre = len(prefetch)

    def spec(shape, fn, **kw):
        return pl.BlockSpec(shape, lambda j, *pre: fn(j, pre), **kw)

    if has_main:
        in_specs += [spec((tm * nchunk, LANES), lambda j, p: (jnp.minimum(j, p[1][0] - 1), 0)),
                     spec((1, d, ff2), lambda j, p: (p[0][j] + eoff, 0, 0)),
                     spec((1, 1, ff2), lambda j, p: (p[0][j], 0, 0)),
                     spec((1, ff2 // 2, d), lambda j, p: (p[0][j] + eoff, 0, 0)),
                     spec((1, 1, d), lambda j, p: (p[0][j], 0, 0))]
        operands += [main["xs"], main["wgu"], main["bgu"], main["wd"], main["bd"]]
    tokens = t_tok * TOP_K
    if kind == "dispatch":
        in_specs += [spec((1, 1, tokens), lambda j, p: (clamp(j), 0, 0), memory_space=smem),
                     spec((t_tok * nchunk, LANES), lambda j, p: (clamp(j) + off, 0))]
        operands += [side["pos3"], side["hflat"]]
    else:
        n, dd = side["x2"].shape
        per = seq // t_tok
        in_specs += [spec((1, 1, tokens), lambda j, p: (clamp(j), 0, 0), memory_space=smem),
                     spec((1, 1, tokens), lambda j, p: (clamp(j + 1), 0, 0), memory_space=smem),
                     pl.BlockSpec(memory_space=pl.ANY),
                     spec((t_tok, LANES), lambda j, p: (clamp(j) + off, 0)),
                     spec((t_tok, dd), lambda j, p: (clamp(j) + off, 0)),
                     spec((1, 1, dd), lambda j, p: ((clamp(j) + off) // per, 0, 0))]
        operands += [side["pos3"], side["pos3"], side["ys"], side["gate"], side["x2"], side["g2"]]
        x_operand = npre + len(operands) - 2
    if has_main:
        out_specs += [spec((tm * nchunk, LANES), lambda j, p: (j, 0))]
        out_shape += [jax.ShapeDtypeStruct(main["xs"].shape, f32)]
    aliases = {}
    if kind == "dispatch":
        out_specs += [pl.BlockSpec(memory_space=pl.ANY)]
        out_shape += [jax.ShapeDtypeStruct((side["rows_pad"] * nchunk, LANES), f32)]
        scratch += [pltpu.VMEM((tm * nchunk, LANES), f32), pltpu.VMEM((2, t_tok * nchunk, LANES), f32),
                    pltpu.SemaphoreType.DMA((2,)), pltpu.SemaphoreType.DMA]
    else:
        out_specs += [spec((t_tok, dd), lambda j, p: (clamp(j) + off, 0))]
        out_shape += [jax.ShapeDtypeStruct((n, dd), f32)]
        scratch += [pltpu.VMEM((2, TOP_K * t_tok * nchunk, LANES), f32),
                    pltpu.SemaphoreType.DMA((2,))]
        aliases = {x_operand: len(out_shape) - 1}
    grid = (main["te"].shape[0],) if has_main else (nside,)
    outs = pl.pallas_call(
        functools.partial(_moe_stage_kernel, has_main=has_main, side=kind, nside=nside,
                          t_tok=t_tok, tm=tm, nchunk=nchunk, ff=None if ff2 is None else ff2 // 2),
        out_shape=tuple(out_shape),
        grid_spec=pltpu.PrefetchScalarGridSpec(
            num_scalar_prefetch=npre, grid=grid, in_specs=in_specs, out_specs=tuple(out_specs),
            scratch_shapes=scratch),
        input_output_aliases=aliases,
        compiler_params=_cparams(("arbitrary",)),
        name=("moe_experts_" if has_main else "moe_") + kind,
    )(*prefetch, *operands)
    return outs


def _moe_plan(idx, rank, cnt, ne, tm, ntile_max):
    counts = cnt[:ne].astype(jnp.int32)
    ntile_e = (counts + tm - 1) // tm
    tile_end = jnp.cumsum(ntile_e)
    tile_start = tile_end - ntile_e
    used = tile_end[-1]
    e_idx = idx[:, :TOP_K].astype(jnp.int32)
    onehot = e_idx[:, :, None] == jnp.arange(ne, dtype=jnp.int32)[None, None, :]
    start = jnp.sum(jnp.where(onehot, (tile_start * tm)[None, None, :], 0), axis=-1)
    pos = start + rank[:, :TOP_K].astype(jnp.int32)
    tiles = jnp.minimum(jnp.arange(ntile_max, dtype=jnp.int32), used - 1)
    tile_expert = jnp.sum(tile_end[None, :] <= tiles[:, None], axis=1).astype(jnp.int32)
    ztile = jnp.where(ntile_e > 0, tile_end - 1, -1)
    ztile = jnp.concatenate([ztile, used.reshape(1)]).astype(jnp.int32)
    return pos, tile_expert, used.reshape(1).astype(jnp.int32), ztile


def _lane_tile(v):
    return jnp.tile(v.astype(f32), LANES // v.shape[0]).reshape(1, LANES)


def kernel(x, c, positions, ada_w, ada_b, norm1_g, norm2_g, sb_w_qkv, sb_w_o, dsa_w_in,
           dsa_q_gain, dsa_k_gain, dsa_w_o, router_w, router_b, exp_w_gu, exp_b_gu,
           exp_w_down, exp_b_down):
    batch, seq, d = x.shape
    depth = ada_w.shape[0]
    n = batch * seq
    ne = router_w.shape[-1]
    ff = exp_w_down.shape[2]
    topk = min(TOPK_MAX, seq // 4)
    assert seq % KB == 0 and d % LANES == 0 and ne <= LANES

    mod = _modulation(c, ada_w, ada_b)
    x2 = x.reshape(n, d)

    ncols = dsa_w_in.shape[-1]
    ncols_pad = -(-ncols // LANES) * LANES
    half = ROPE_DIMS // 2
    inv = jnp.exp(-math.log(ROPE_THETA) * (2.0 * jnp.arange(half, dtype=f32) / ROPE_DIMS))
    l64 = jnp.arange(LANES) % HEAD_DIM
    inv_lane = jnp.where(l64 < ROPE_DIMS, inv[l64 % half], 0.0).reshape(1, LANES)
    bd = (jnp.arange(LANES)[:, None] // HEAD_DIM == jnp.arange(LANES)[None, :] // HEAD_DIM)
    bd = bd.astype(bf16)
    pos = positions.reshape(n, 1)

    nchunk = d // LANES
    half_tok = n // 2
    nside = half_tok // MOE_TOK
    ntile_half = (half_tok * TOP_K) // MOE_TM + ne
    assert (half_tok * TOP_K) % MOE_TM == 0 and half_tok % MOE_TOK == 0 and seq % MOE_TOK == 0
    assert half_tok % POST_TM == 0
    tri = (jnp.arange(POST_TM)[:, None] > jnp.arange(POST_TM)[None, :]).astype(bf16)
    wgu_all, wd_all = _expert_weight_prep(exp_w_gu.reshape(depth * ne, d, 2 * ff),
                                          exp_w_down.reshape(depth * ne, ff, d))

    for layer in range(depth):
        m6 = [mod[layer][:, k * d:(k + 1) * d].reshape(batch, 1, d) for k in range(6)]
        sh1, sc1, g1, sh2, sc2, g2 = m6
        n1 = norm1_g[layer].reshape(1, d)
        n2 = norm2_g[layer].reshape(1, d)
        j = layer // 2
        if layer % 2 == 0:
            qkv = _proj(x2, n1, sc1, sh1, sb_w_qkv[j].astype(bf16), seq)
            o = _stick_breaking(qkv, batch, seq, d)
            wo = sb_w_o[j].astype(bf16)
        else:
            w = dsa_w_in[j]
            kv0 = d
            qi0 = d + 2 * LANES
            ki0 = qi0 + IDX_HEADS * IDX_DIM
            w = jnp.concatenate([w[:, :kv0], w[:, qi0:ki0], w[:, kv0:qi0], w[:, ki0:]], axis=1)
            w = jnp.pad(w, ((0, 0), (0, ncols_pad - ncols))).astype(bf16)
            proj, wi = _dsa_proj(x2, n1, sc1, sh1, w, pos, inv_lane, _lane_tile(dsa_q_gain[j]),
                                 _lane_tile(dsa_k_gain[j]), bd, seq)
            o = _dsa_attention(proj, wi, batch, seq, d, topk)
            wo = dsa_w_o[j].astype(bf16)
        rw = jnp.pad(router_w[layer], ((0, 0), (0, LANES - ne)))
        rw_hi = rw.astype(bf16)
        rw = jnp.stack([rw_hi, (rw - rw_hi.astype(f32)).astype(bf16)])
        rb = jnp.pad(router_b[layer], (0, LANES - ne), constant_values=-jnp.inf).reshape(1, LANES)
        x2, hflat, idx, gate, rank, cnt = _post_attention(o, wo, x2, g1, n2, sc2, sh2, rw, rb,
                                                          tri, seq, POST_TM)
        bgu = exp_b_gu[layer]
        bgu = jnp.concatenate([bgu[..., 0::2], bgu[..., 1::2]], axis=-1).reshape(ne, 1, 2 * ff)
        halves = []
        for hh in range(2):
            tok = slice(hh * half_tok, (hh + 1) * half_tok)
            spos, te, used, zt = _moe_plan(idx[tok], rank[tok], cnt[hh, 0], ne, MOE_TM, ntile_half)
            halves.append(dict(
                pos3=spos.reshape(nside, 1, MOE_TOK * TOP_K), zt=zt, off=hh * nside,
                experts=dict(te=te, used=used, wgu=wgu_all, bgu=bgu, wd=wd_all,
                             bd=exp_b_down[layer].reshape(ne, 1, d), eoff=layer * ne)))
        stage = functools.partial(_moe_stage, seq=seq, tm=MOE_TM, t_tok=MOE_TOK, nchunk=nchunk)

        def disp(h):
            return dict(zt=h["zt"], pos3=h["pos3"], hflat=hflat, off=h["off"],
                        rows_pad=ntile_half * MOE_TM)

        def comb(h, ys, xin):
            return dict(pos3=h["pos3"], ys=ys, gate=gate, x2=xin, g2=g2, off=h["off"])

        ha, hb = halves
        (xs_a,) = stage(None, disp(ha), "dispatch")
        ys_a, xs_b = stage(dict(ha["experts"], xs=xs_a), disp(hb), "dispatch")
        ys_b, x2 = stage(dict(hb["experts"], xs=xs_b), comb(ha, ys_a, x2), "combine")
        (x2,) = stage(None, comb(hb, ys_b, x2), "combine")
    return x2.reshape(batch, seq, d)
```

```python
import functools
import math

import jax
import jax.numpy as jnp
from jax import lax
from jax.experimental import pallas as pl
from jax.experimental.pallas import tpu as pltpu

HEAD_DIM = 64
DSA_KV_HEADS = 2
IDX_HEADS = 8
IDX_DIM = 64
CHUNK_SHIFT = 6
TOPK_MAX = 256
TOP_K = 4
ROPE_THETA = 500000.0
ROPE_DIMS = HEAD_DIM // 4
SWIGLU_ALPHA = 1.702
SWIGLU_LIMIT = 7.0
EPS = 1e-6

LANES = 128
QB = 128
KB = 256
DQ = 256
NEG = -1e30
INT_MIN = -(2 ** 31)
KEY_NEG_INF = -2139095041
NORM_SLACK = 1.02
L_TINY = 1e-30
VMEM_LIMIT = 48 * 1024 * 1024
POST_TM = 512
MOE_TM = 512
MOE_TOK = 256
DMA_UNROLL = 8
SB_GROUP = 6
SB_WINDOW = 3
SB_CUTOFF = -110.0

f32 = jnp.float32
bf16 = jnp.bfloat16


def _cparams(sem):
    return pltpu.CompilerParams(dimension_semantics=sem, vmem_limit_bytes=VMEM_LIMIT)


def _dot(a, b):
    return jnp.dot(a, b, preferred_element_type=f32)


def _dot_nt(a, b):
    return lax.dot_general(a, b, (((1,), (1,)), ((), ())), preferred_element_type=f32)


def _dot_split(x, m01, passes):
    acc = None
    r = x
    for p in range(passes):
        t = r.astype(bf16)
        d = _dot(t, m01)
        acc = d if acc is None else acc + d
        if p + 1 < passes:
            r = r - t.astype(f32)
    return acc


def _norm_mod(x, g, sc, sh):
    ms = jnp.mean(x * x, axis=-1, keepdims=True)
    return (x * lax.rsqrt(ms + EPS) * g) * (1.0 + sc) + sh


def _mod_kernel(c_ref, w_ref, b_ref, o_ref):
    c = c_ref[...]
    cs = c * (1.0 / (1.0 + jnp.exp(-c)))
    o_ref[0] = jnp.dot(cs, w_ref[0], preferred_element_type=f32,
                       precision=lax.Precision.HIGHEST) + b_ref[0]


def _modulation(c, ada_w, ada_b):
    depth, d, n6 = ada_w.shape
    b = c.shape[0]
    tn = 1024
    return pl.pallas_call(
        _mod_kernel,
        out_shape=jax.ShapeDtypeStruct((depth, b, n6), f32),
        grid=(depth, n6 // tn),
        in_specs=[pl.BlockSpec((b, d), lambda l, j: (0, 0)),
                  pl.BlockSpec((1, d, tn), lambda l, j: (l, 0, j)),
                  pl.BlockSpec((1, 1, tn), lambda l, j: (l, 0, j))],
        out_specs=pl.BlockSpec((1, b, tn), lambda l, j: (l, 0, j)),
        compiler_params=_cparams(("arbitrary", "arbitrary")),
        name="adaln_mod",
    )(c, ada_w, ada_b.reshape(depth, 1, n6))


def _proj_kernel(x_ref, g_ref, sc_ref, sh_ref, w_ref, o_ref):
    h = _norm_mod(x_ref[...], g_ref[...], sc_ref[0], sh_ref[0])
    o_ref[...] = _dot(h.astype(bf16), w_ref[...]).astype(o_ref.dtype)


def _proj(x2, g, sc, sh, w, seq, tm=512):
    n, d = x2.shape
    nc = w.shape[1]
    per = seq // tm
    return pl.pallas_call(
        _proj_kernel,
        out_shape=jax.ShapeDtypeStruct((n, nc), bf16),
        grid=(n // tm,),
        in_specs=[pl.BlockSpec((tm, d), lambda i: (i, 0)),
                  pl.BlockSpec((1, d), lambda i: (0, 0)),
                  pl.BlockSpec((1, 1, d), lambda i: (i // per, 0, 0)),
                  pl.BlockSpec((1, 1, d), lambda i: (i // per, 0, 0)),
                  pl.BlockSpec((d, nc), lambda i: (0, 0))],
        out_specs=pl.BlockSpec((tm, nc), lambda i: (i, 0)),
        compiler_params=_cparams(("arbitrary",)),
        name="sb_qkv_proj",
    )(x2, g, sc, sh, w)


def _sb_kernel(q_ref, k_ref, v_ref, o_ref, k0_ref, k1_ref):
    seq = q_ref.shape[0]
    nqb = seq // QB
    lane = lax.broadcasted_iota(jnp.int32, (1, LANES), 1)
    kk = k_ref[...]
    zero = jnp.zeros_like(kk)
    k0_ref[...] = jnp.where(lane < HEAD_DIM, kk, zero)
    k1_ref[...] = jnp.where(lane >= HEAD_DIM, kk, zero)
    row = lax.broadcasted_iota(jnp.int32, (QB, QB), 0)
    col = lax.broadcasted_iota(jnp.int32, (QB, QB), 1)
    before = col < row
    tri = jnp.where(row > col, 1.0, 0.0).astype(bf16)

    def windows(jobs, nb, diag):
        zs = [_dot_nt(q, kh_ref[pl.ds(k0, nb * QB), :]) for q, kh_ref, k0, _ in jobs]
        lms, lss = [], []
        for z in zs:
            l1p = jnp.log(1.0 + jnp.exp(-jnp.abs(z)))
            lm = jnp.minimum(-z, 0.0) - l1p
            lss.append(lm + z)
            parts = [lm[:, s * QB:(s + 1) * QB] for s in range(nb)]
            if diag:
                parts[-1] = jnp.where(before, parts[-1], 0.0)
            lms.append(parts)
        cums = [[_dot_split(p, tri, 2) for p in parts] for parts in lms]
        outs = []
        for (q, kh_ref, k0, c_in), parts, cum, ls in zip(jobs, lms, cums, lss):
            c = c_in
            probs = [None] * nb
            for s in reversed(range(nb)):
                a = jnp.exp(ls[:, s * QB:(s + 1) * QB] + (cum[s] + c))
                if diag and s == nb - 1:
                    a = jnp.where(before, a, 0.0)
                probs[s] = a.astype(bf16)
                c = c + jnp.sum(parts[s], axis=1, keepdims=True)
            outs.append((c, probs[0] if nb == 1 else jnp.concatenate(probs, axis=1)))
        return [(c, _dot(a_all, v_ref[pl.ds(k0, nb * QB), :]))
                for (c, a_all), (_, _, k0, _) in zip(outs, jobs)]

    def qheads(blocks, nb):
        jobs, meta = [], []
        zc = jnp.zeros((QB, 1), f32)
        for i in blocks:
            q0 = i * QB
            k0 = q0 - (nb - 1) * QB
            if not isinstance(i, int):
                q0 = pl.multiple_of(q0, QB)
                k0 = pl.multiple_of(k0, QB)
            q = q_ref[pl.ds(q0, QB), :] * (HEAD_DIM ** -0.5)
            jobs += [(q, k0_ref, k0, zc), (q, k1_ref, k0, zc)]
            meta.append((q0, q))
        res = windows(jobs, nb, True)
        return [(q0, q) + res[2 * n] + res[2 * n + 1] for n, (q0, q) in enumerate(meta)]

    def qhead(i, nb):
        return qheads([i], nb)[0]

    def qtail(i, nb, st):
        q0, q, c0, a0, c1, a1 = st
        if nb <= SB_WINDOW and not (isinstance(i, int) and i < nb):
            def cond(s):
                jb, c0, _, c1, _ = s
                live = jnp.max(jnp.maximum(c0, c1)) > SB_CUTOFF
                return jnp.logical_and(jb >= 0, live)

            def body(s):
                jb, c0, a0, c1, a1 = s
                kb = pl.multiple_of(jb * QB, QB)
                (c0, d0), (c1, d1) = windows([(q, k0_ref, kb, c0), (q, k1_ref, kb, c1)], 1, False)
                return jb - 1, c0, a0 + d0, c1, a1 + d1

            jb0 = jnp.asarray(i - nb, jnp.int32)
            _, c0, a0, c1, a1 = lax.while_loop(cond, body, (jb0, c0, a0, c1, a1))
        o_ref[pl.ds(q0, QB), :] = jnp.where(lane < HEAD_DIM, a0, a1).astype(o_ref.dtype)

    first = min(SB_WINDOW, nqb)
    npairs = (nqb - first) // SB_GROUP
    singles = list(range(first + SB_GROUP * npairs, nqb))
    for i in range(first):
        if i + 1 == SB_WINDOW and singles:
            continue
        qtail(i, i + 1, qhead(i, i + 1))

    def pair(j, _):
        i = first + SB_GROUP * j
        blocks = [i + n for n in range(SB_GROUP)]
        for b, st in zip(blocks, qheads(blocks, SB_WINDOW)):
            qtail(b, SB_WINDOW, st)
        return 0

    lax.fori_loop(0, npairs, pair, 0)
    if singles:
        group = ([first - 1] if first == SB_WINDOW else []) + singles
        for i, st in zip(group, qheads(group, SB_WINDOW)):
            qtail(i, SB_WINDOW, st)


def _stick_breaking(qkv, batch, seq, d):
    npair = d // LANES
    return pl.pallas_call(
        _sb_kernel,
        out_shape=jax.ShapeDtypeStruct((batch * seq, d), bf16),
        grid=(batch, npair),
        in_specs=[pl.BlockSpec((seq, LANES), lambda b, p: (b, p)),
                  pl.BlockSpec((seq, LANES), lambda b, p: (b, npair + p)),
                  pl.BlockSpec((seq, LANES), lambda b, p: (b, 2 * npair + p))],
        out_specs=pl.BlockSpec((seq, LANES), lambda b, p: (b, p)),
        scratch_shapes=[pltpu.VMEM((seq, LANES), bf16), pltpu.VMEM((seq, LANES), bf16)],
        compiler_params=_cparams(("arbitrary", "arbitrary")),
        name="stick_breaking_attn",
    )(qkv, qkv, qkv)


def _dsa_proj_kernel(x_ref, g_ref, sc_ref, sh_ref, w_ref, pos_ref, inv_ref, qg_ref, kg_ref,
                     bd_ref, o_ref, wi_ref, *, d):
    h = _norm_mod(x_ref[...], g_ref[...], sc_ref[0], sh_ref[0])
    p = _dot(h.astype(bf16), w_ref[...])
    lane = lax.broadcasted_iota(jnp.int32, (1, LANES), 1)
    ang = pos_ref[...].astype(f32) * inv_ref[...]
    cos_t = jnp.cos(ang)
    sin_t = jnp.sin(ang)
    upper = (lane % ROPE_DIMS) >= (ROPE_DIMS // 2)
    s_up = jnp.where(upper, sin_t, 0.0)
    s_lo = jnp.where(upper, 0.0, -sin_t)
    half = ROPE_DIMS // 2

    def rope(y):
        return y * cos_t + pltpu.roll(y, half, 1) * s_up + pltpu.roll(y, LANES - half, 1) * s_lo

    def headnorm(y, gain):
        ms = _dot_split(y * y, bd_ref[...], 2) * (1.0 / HEAD_DIM)
        return y * lax.rsqrt(ms + EPS) * gain

    nq = d // LANES
    for c in range(nq):
        y = p[:, c * LANES:(c + 1) * LANES]
        o_ref[:, c * LANES:(c + 1) * LANES] = rope(headnorm(y, qg_ref[...])).astype(bf16)
    c0 = nq
    for c in range(c0, c0 + IDX_HEADS * IDX_DIM // LANES):
        o_ref[:, c * LANES:(c + 1) * LANES] = rope(p[:, c * LANES:(c + 1) * LANES]).astype(bf16)
    c0 += IDX_HEADS * IDX_DIM // LANES
    y = p[:, c0 * LANES:(c0 + 1) * LANES]
    o_ref[:, c0 * LANES:(c0 + 1) * LANES] = rope(headnorm(y, kg_ref[...])).astype(bf16)
    c0 += 1
    o_ref[:, c0 * LANES:(c0 + 1) * LANES] = p[:, c0 * LANES:(c0 + 1) * LANES].astype(bf16)
    c0 += 1
    y = p[:, c0 * LANES:(c0 + 1) * LANES]
    o_ref[:, c0 * LANES:(c0 + 1) * LANES] = jnp.where(lane < IDX_DIM, rope(y), 0.0).astype(bf16)
    wi = pltpu.roll(y, LANES - IDX_DIM, 1) * (IDX_HEADS ** -0.5) * (IDX_DIM ** -0.5)
    wi_ref[...] = jnp.where(lane < IDX_HEADS, wi, 0.0)


def _dsa_proj(x2, g, sc, sh, w, pos, inv_lane, qg, kg, bd, seq, tm=512):
    n, d = x2.shape
    nc = w.shape[1]
    per = seq // tm
    return pl.pallas_call(
        functools.partial(_dsa_proj_kernel, d=d),
        out_shape=(jax.ShapeDtypeStruct((n, nc), bf16), jax.ShapeDtypeStruct((n, LANES), f32)),
        grid=(n // tm,),
        in_specs=[pl.BlockSpec((tm, d), lambda i: (i, 0)),
                  pl.BlockSpec((1, d), lambda i: (0, 0)),
                  pl.BlockSpec((1, 1, d), lambda i: (i // per, 0, 0)),
                  pl.BlockSpec((1, 1, d), lambda i: (i // per, 0, 0)),
                  pl.BlockSpec((d, nc), lambda i: (0, 0)),
                  pl.BlockSpec((tm, 1), lambda i: (i, 0)),
                  pl.BlockSpec((1, LANES), lambda i: (0, 0)),
                  pl.BlockSpec((1, LANES), lambda i: (0, 0)),
                  pl.BlockSpec((1, LANES), lambda i: (0, 0)),
                  pl.BlockSpec((LANES, LANES), lambda i: (0, 0))],
        out_specs=(pl.BlockSpec((tm, nc), lambda i: (i, 0)),
                   pl.BlockSpec((tm, LANES), lambda i: (i, 0))),
        compiler_params=_cparams(("arbitrary",)),
        name="dsa_in_proj",
    )(x2, g, sc, sh, w, pos, inv_lane, qg, kg, bd)


def _dsa_kernel(q_ref, k_ref, v_ref, qi_ref, ki_ref, wi_ref, o_ref,
                kk_ref, vv_ref, ki2_ref, keys_ref, bias_ref, m_ref, acc_ref, thr_ref, need_ref,
                qs_ref, kmax_ref, *, topk):
    seq = k_ref.shape[0]
    nchunk = q_ref.shape[1] // LANES
    group = (2 * nchunk) // DSA_KV_HEADS
    i = pl.program_id(1)
    lane = lax.broadcasted_iota(jnp.int32, (1, LANES), 1)
    lo = lane < HEAD_DIM

    @pl.when(i == 0)
    def _():
        k = k_ref[...]
        v = v_ref[...]
        ki = ki_ref[...]
        kr = pltpu.roll(k.astype(f32), HEAD_DIM, 1).astype(bf16)
        vr = pltpu.roll(v.astype(f32), HEAD_DIM, 1).astype(bf16)
        zero = jnp.zeros_like(k)
        kk_ref[0] = jnp.where(lo, k, zero)
        kk_ref[1] = jnp.where(lo, zero, kr)
        kk_ref[2] = jnp.where(lo, kr, zero)
        kk_ref[3] = jnp.where(lo, zero, k)
        lane_full = lax.broadcasted_iota(jnp.int32, k.shape, 1)
        oh = [jnp.where(lane_full < HEAD_DIM, 1.0, 0.0).astype(bf16),
              jnp.where(lane_full < HEAD_DIM, 0.0, 1.0).astype(bf16)]
        vv_ref[0] = jnp.concatenate([jnp.where(lo, v, zero), oh[0]], axis=1)
        vv_ref[1] = jnp.concatenate([jnp.where(lo, zero, vr), oh[1]], axis=1)
        vv_ref[2] = jnp.concatenate([jnp.where(lo, vr, zero), oh[0]], axis=1)
        vv_ref[3] = jnp.concatenate([jnp.where(lo, zero, v), oh[1]], axis=1)
        ki2_ref[0] = ki
        ki2_ref[1] = pltpu.roll(ki.astype(f32), HEAD_DIM, 1).astype(bf16)
        kf = k.astype(f32)
        rr = lax.broadcasted_iota(jnp.int32, (LANES, LANES), 0) < HEAD_DIM
        cc = lax.broadcasted_iota(jnp.int32, (LANES, LANES), 1) < HEAD_DIM
        bd = jnp.where(rr == cc, 1.0, 0.0).astype(bf16)
        n2 = jnp.max(_dot((kf * kf).astype(bf16), bd), axis=0, keepdims=True)
        for g in range(DSA_KV_HEADS):
            sel = lo if g == 0 else jnp.logical_not(lo)
            top = jnp.max(jnp.where(sel, n2, 0.0), axis=1, keepdims=True)
            kmax_ref[g] = jnp.broadcast_to(top, (1, LANES))

    nkb = i + 1
    qrow = lax.broadcasted_iota(jnp.int32, (DQ, KB), 0) + i * DQ
    kcol = lax.broadcasted_iota(jnp.int32, (DQ, KB), 1)

    wi = wi_ref[...]
    wcols = [wi[:, hd:hd + 1] for hd in range(IDX_HEADS)]

    def score_block(jb, _):
        k0 = pl.multiple_of(jb * KB, KB)
        score = jnp.zeros((DQ, KB), f32)
        for hd in range(IDX_HEADS):
            qc = qi_ref[:, (hd // 2) * LANES:(hd // 2 + 1) * LANES]
            kb = ki2_ref[hd % 2, pl.ds(k0, KB), :]
            score = score + jnp.maximum(_dot_nt(qc, kb), 0.0) * wcols[hd]
        score = score + 0.0
        bits = pltpu.bitcast(score, jnp.int32)
        key = bits ^ ((bits >> 31) & 0x7FFFFFFF)
        adm = ((kcol + k0) >> CHUNK_SHIFT) <= (qrow >> CHUNK_SHIFT)
        keys_ref[jb] = jnp.where(adm, key, INT_MIN)
        return 0

    lax.fori_loop(0, nkb, score_block, 0)

    kf = float(topk)
    ones_l = jnp.ones((LANES, LANES), bf16)

    def search(nk):
        def count(r, pred):
            part = None
            for jb in range(nk):
                for hl in range(KB // LANES):
                    key = keys_ref[jb, r * QB:(r + 1) * QB, hl * LANES:(hl + 1) * LANES]
                    hit = jnp.where(pred(key), 1.0, 0.0)
                    part = hit if part is None else part + hit
            return _dot(part.astype(bf16), ones_l)

        groups = range(DQ // QB)
        t0 = tuple(jnp.where(count(r, lambda key: key >= 0) >= kf, 0, INT_MIN).astype(jnp.int32)
                   for r in groups)

        def bit_step(s, ts):
            bit = jnp.left_shift(jnp.int32(1), 30 - s)
            out = []
            for r in groups:
                cand = ts[r] + bit
                enough = count(r, lambda key, cand=cand: key >= cand) >= kf
                out.append(jnp.where(enough, cand, ts[r]))
            return tuple(out)

        ts = lax.fori_loop(0, 31, bit_step, t0)
        for r in groups:
            thr_ref[r * QB:(r + 1) * QB, :] = ts[r]
            need_ref[r * QB:(r + 1) * QB, :] = kf - count(r, lambda key, t=ts[r]: key > t)

    for nk in range(1, seq // KB + 1):
        @pl.when(nkb == nk)
        def _(nk=nk):
            search(nk)

    thr = jnp.concatenate([thr_ref[...]] * (KB // LANES), axis=1)
    need = jnp.concatenate([need_ref[...]] * (KB // LANES), axis=1)

    r2 = lax.broadcasted_iota(jnp.int32, (KB, KB), 0)
    c2 = lax.broadcasted_iota(jnp.int32, (KB, KB), 1)
    tri = jnp.where(r2 < c2, 1.0, 0.0).astype(bf16)
    ones_k = jnp.ones((KB, KB), bf16)

    def bias_block(jb, carry):
        key = keys_ref[jb]
        eq = key == thr
        eqb = jnp.where(eq, 1.0, 0.0).astype(bf16)
        rank = _dot(eqb, tri) + carry
        sel = (key > thr) | (eq & (rank < need))
        sel = sel & (key > KEY_NEG_INF)
        bias_ref[jb] = jnp.where(sel, 0.0, NEG)
        return carry + _dot(eqb, ones_k)

    lax.fori_loop(0, nkb, bias_block, jnp.zeros((DQ, KB), f32))

    cpg = group // 2
    for g in range(DSA_KV_HEADS):
        qs_ref[g] = jnp.concatenate(
            [q_ref[:, c * LANES:(c + 1) * LANES] for c in range(g * cpg, (g + 1) * cpg)],
            axis=0) * (HEAD_DIM ** -0.5)

    def scores(g, hf, k0, bias):
        return _dot_nt(qs_ref[g], kk_ref[2 * g + hf, pl.ds(k0, KB), :]) + bias

    def sweep_max():
        m_ref[...] = jnp.full(m_ref.shape, NEG, f32)

        def max_block(jb, _):
            k0 = pl.multiple_of(jb * KB, KB)
            bias = jnp.concatenate([bias_ref[jb]] * cpg, axis=0)
            for g in range(DSA_KV_HEADS):
                for hf in range(2):
                    s = scores(g, hf, k0, bias)
                    fold = jnp.maximum(s[:, :LANES], s[:, LANES:])
                    m_ref[2 * g + hf] = jnp.maximum(m_ref[2 * g + hf], fold)
            return 0

        lax.fori_loop(0, nkb, max_block, 0)
        for h in range(2 * DSA_KV_HEADS):
            m = jnp.max(m_ref[h], axis=1, keepdims=True)
            m_ref[h] = jnp.broadcast_to(m, (cpg * DQ, LANES))

    def sweep_exp():
        acc_ref[...] = jnp.zeros(acc_ref.shape, f32)

        def attn_block(jb, _):
            k0 = pl.multiple_of(jb * KB, KB)
            bias = jnp.concatenate([bias_ref[jb]] * cpg, axis=0)
            for g in range(DSA_KV_HEADS):
                ps = []
                for hf in range(2):
                    m = m_ref[2 * g + hf]
                    s = scores(g, hf, k0, bias)
                    ps.append(jnp.exp(s - jnp.concatenate([m, m], axis=1)).astype(bf16))
                p2 = jnp.concatenate(ps, axis=1)
                v2 = jnp.concatenate([vv_ref[2 * g, pl.ds(k0, KB), :],
                                      vv_ref[2 * g + 1, pl.ds(k0, KB), :]], axis=0)
                acc_ref[g] += _dot(p2, v2)
            return 0

        lax.fori_loop(0, nkb, attn_block, 0)

    def write_out():
        for c in range(nchunk):
            r = acc_ref[c // cpg, (c % cpg) * DQ:(c % cpg + 1) * DQ, :]
            o_ref[:, c * LANES:(c + 1) * LANES] = (r[:, :LANES] / r[:, LANES:]).astype(o_ref.dtype)

    for g in range(DSA_KV_HEADS):
        q = qs_ref[g].astype(f32)
        q2 = (q * q).astype(bf16)
        for hf in range(2):
            half = jnp.where((lax.broadcasted_iota(jnp.int32, (LANES, LANES), 0) < HEAD_DIM)
                             == (hf == 0), 1.0, 0.0).astype(bf16)
            qn2 = _dot(q2, half)
            m_ref[2 * g + hf] = NORM_SLACK * jnp.sqrt(qn2 * kmax_ref[g])
    sweep_exp()
    lmin = jnp.min(acc_ref[:, :, LANES:])

    @pl.when(lmin > L_TINY)
    def _():
        write_out()

    @pl.when(jnp.logical_not(lmin > L_TINY))
    def _():
        sweep_max()
        sweep_exp()
        write_out()


def group_even(d):
    return ((d // HEAD_DIM) // DSA_KV_HEADS) % 2 == 0


def _dsa_attention(proj, wi, batch, seq, d, topk):
    nqb = seq // DQ
    nchunk = d // LANES
    cpg = nchunk // DSA_KV_HEADS
    qiw = IDX_HEADS * IDX_DIM
    assert d % qiw == 0 and group_even(d) and DQ == KB
    qicol = d // qiw
    kcol = nchunk + qiw // LANES
    vcol = kcol + 1
    kicol = kcol + 2
    return pl.pallas_call(
        functools.partial(_dsa_kernel, topk=topk),
        out_shape=jax.ShapeDtypeStruct((batch * seq, d), bf16),
        grid=(batch, nqb),
        in_specs=[pl.BlockSpec((DQ, d), lambda b, i: (b * nqb + i, 0)),
                  pl.BlockSpec((seq, LANES), lambda b, i: (b, kcol)),
                  pl.BlockSpec((seq, LANES), lambda b, i: (b, vcol)),
                  pl.BlockSpec((DQ, IDX_HEADS * IDX_DIM), lambda b, i: (b * nqb + i, qicol)),
                  pl.BlockSpec((seq, LANES), lambda b, i: (b, kicol)),
                  pl.BlockSpec((DQ, LANES), lambda b, i: (b * nqb + i, 0))],
        out_specs=pl.BlockSpec((DQ, d), lambda b, i: (b * nqb + i, 0)),
        scratch_shapes=[pltpu.VMEM((4, seq, LANES), bf16),
                        pltpu.VMEM((4, seq, 2 * LANES), bf16),
                        pltpu.VMEM((2, seq, LANES), bf16),
                        pltpu.VMEM((seq // KB, DQ, KB), jnp.int32),
                        pltpu.VMEM((seq // KB, DQ, KB), f32),
                        pltpu.VMEM((2 * DSA_KV_HEADS, cpg * DQ, LANES), f32),
                        pltpu.VMEM((DSA_KV_HEADS, cpg * DQ, 2 * LANES), f32),
                        pltpu.VMEM((DQ, LANES), jnp.int32),
                        pltpu.VMEM((DQ, LANES), f32),
                        pltpu.VMEM((DSA_KV_HEADS, cpg * DQ, LANES), bf16),
                        pltpu.VMEM((DSA_KV_HEADS, 1, LANES), f32)],
        compiler_params=_cparams(("arbitrary", "arbitrary")),
        name="dsa_sparse_attn",
    )(proj, proj, proj, proj, proj, wi)


def _post_kernel(o_ref, wo_ref, x_ref, g1_ref, n2_ref, sc_ref, sh_ref, rw_ref, rb_ref, tri_ref,
                 xo_ref, hf_ref, idx_ref, gate_ref, rank_ref, cnt_ref, run_ref):
    half = pl.num_programs(0) // 2

    @pl.when((pl.program_id(0) == 0) | (pl.program_id(0) == half))
    def _():
        run_ref[...] = jnp.zeros(run_ref.shape, f32)

    y = _dot(o_ref[...], wo_ref[...])
    x = x_ref[...] + g1_ref[0] * y
    xo_ref[...] = x
    h = _norm_mod(x, n2_ref[...], sc_ref[0], sh_ref[0])
    tm, d = h.shape
    nchunk = d // LANES
    for c in range(nchunk):
        hf_ref[pl.ds(c, tm, stride=nchunk), :] = h[:, c * LANES:(c + 1) * LANES]
    h_hi = h.astype(bf16)
    h_lo = (h - h_hi.astype(f32)).astype(bf16)
    logits = (_dot(h_hi, rw_ref[0]) + _dot(h_hi, rw_ref[1]) + _dot(h_lo, rw_ref[0])) + rb_ref[...]
    lane = lax.broadcasted_iota(jnp.int32, logits.shape, 1).astype(f32)
    work = logits
    vals, idxs, hits = [], [], []
    for _ in range(TOP_K):
        m = jnp.max(work, axis=1, keepdims=True)
        idx = jnp.min(jnp.where(work == m, lane, float(LANES)), axis=1, keepdims=True)
        hit = lane == idx
        vals.append(m)
        idxs.append(idx)
        hits.append(hit)
        work = jnp.where(hit, -jnp.inf, work)
    es = [jnp.exp(v - vals[0]) for v in vals]
    inv = 1.0 / (es[0] + es[1] + es[2] + es[3])
    multi = jnp.zeros_like(logits)
    for hit in hits:
        multi = multi + jnp.where(hit, 1.0, 0.0)
    before = _dot(tri_ref[...], multi.astype(bf16)) + run_ref[...]
    idx_o = jnp.zeros_like(logits)
    gate_o = jnp.zeros_like(logits)
    rank_o = jnp.zeros_like(logits)
    for k in range(TOP_K):
        slot = lane == float(k)
        rank = jnp.sum(jnp.where(hits[k], before, 0.0), axis=1, keepdims=True)
        idx_o = jnp.where(slot, idxs[k], idx_o)
        gate_o = jnp.where(slot, es[k] * inv, gate_o)
        rank_o = jnp.where(slot, rank, rank_o)
    idx_ref[...] = idx_o
    gate_ref[...] = gate_o
    rank_ref[...] = rank_o
    run_ref[...] += jnp.sum(multi, axis=0, keepdims=True)
    cnt_ref[0] = jnp.broadcast_to(run_ref[...], cnt_ref.shape[1:])


def _post_attention(o, wo, x2, g1, n2, sc2, sh2, rw, rb, tri, seq, tm):
    n, d = x2.shape
    per = seq // tm
    nchunk = d // LANES
    row = lambda i: (i, 0)
    fix = lambda i: (0, 0)
    bat = lambda i: (i // per, 0, 0)
    lane_out = jax.ShapeDtypeStruct((n, LANES), f32)
    return pl.pallas_call(
        _post_kernel,
        out_shape=(jax.ShapeDtypeStruct((n, d), f32), jax.ShapeDtypeStruct((n * nchunk, LANES), f32),
                   lane_out, lane_out, lane_out, jax.ShapeDtypeStruct((2, 8, LANES), f32)),
        grid=(n // tm,),
        in_specs=[pl.BlockSpec((tm, d), row), pl.BlockSpec((d, d), fix),
                  pl.BlockSpec((tm, d), row), pl.BlockSpec((1, 1, d), bat),
                  pl.BlockSpec((1, d), fix), pl.BlockSpec((1, 1, d), bat),
                  pl.BlockSpec((1, 1, d), bat), pl.BlockSpec((2, d, LANES), lambda i: (0, 0, 0)),
                  pl.BlockSpec((1, LANES), fix), pl.BlockSpec((tm, tm), fix)],
        out_specs=(pl.BlockSpec((tm, d), row), pl.BlockSpec((tm * nchunk, LANES), row),
                   pl.BlockSpec((tm, LANES), row), pl.BlockSpec((tm, LANES), row),
                   pl.BlockSpec((tm, LANES), row),
                   pl.BlockSpec((1, 8, LANES), lambda i: (i // (n // tm // 2), 0, 0))),
        scratch_shapes=[pltpu.VMEM((1, LANES), f32)],
        compiler_params=_cparams(("arbitrary",)),
        name="attn_out_norm_router",
    )(o, wo, x2, g1, n2, sc2, sh2, rw, rb, tri)


def _deint_kernel(w_ref, wd_ref, p_ref, o_ref, od_ref):
    ff = o_ref.shape[2] // 2
    wide = 2 * LANES
    for b in range(o_ref.shape[2] // wide):
        x = w_ref[0, :, b * wide:(b + 1) * wide].astype(bf16)
        r = _dot(x, p_ref[...]).astype(bf16)
        o_ref[0, :, b * LANES:(b + 1) * LANES] = r[:, :LANES]
        o_ref[0, :, ff + b * LANES:ff + (b + 1) * LANES] = r[:, LANES:]
    od_ref[...] = wd_ref[...].astype(bf16)


def _expert_weight_prep(w, wd):
    ne, d, ff2 = w.shape
    wide = 2 * LANES
    src = jnp.arange(wide)
    perm = jnp.where(src < LANES, 2 * src, 2 * (src - LANES) + 1)
    p = (jnp.arange(wide)[:, None] == perm[None, :]).astype(bf16)
    blk = lambda e: (e, 0, 0)
    return pl.pallas_call(
        _deint_kernel,
        out_shape=(jax.ShapeDtypeStruct((ne, d, ff2), bf16),
                   jax.ShapeDtypeStruct(wd.shape, bf16)),
        grid=(ne,),
        in_specs=[pl.BlockSpec((1, d, ff2), blk), pl.BlockSpec((1,) + wd.shape[1:], blk),
                  pl.BlockSpec((wide, wide), lambda e: (0, 0))],
        out_specs=(pl.BlockSpec((1, d, ff2), blk), pl.BlockSpec((1,) + wd.shape[1:], blk)),
        compiler_params=_cparams(("arbitrary",)),
        name="expert_weight_prep",
    )(w, wd, p)


def _dispatch_step(i, nstep, zt_ref, pos_ref, hf_ref, xs_ref, zbuf_ref, ring_ref, sem, zsem,
                   t_tok, tm, nchunk):
    ne = zt_ref.shape[0] - 1
    ntile = xs_ref.shape[0] // (tm * nchunk)

    def zero_tile(e):
        return zt_ref[e] if e < ne else zt_ref[ne] + (e - ne)

    def zero_copy(e):
        start = pl.multiple_of(zero_tile(e) * (tm * nchunk), tm * nchunk)
        return pltpu.make_async_copy(zbuf_ref, xs_ref.at[pl.ds(start, tm * nchunk)], zsem)

    def zero_wanted(e):
        return zero_tile(e) >= 0 if e < ne else zero_tile(e) < ntile

    @pl.when(i == 0)
    def _():
        zbuf_ref[...] = jnp.zeros(zbuf_ref.shape, f32)
        for e in range(2 * ne):
            @pl.when(zero_wanted(e))
            def _():
                zero_copy(e).start()
        for e in range(2 * ne):
            @pl.when(zero_wanted(e))
            def _():
                zero_copy(e).wait()

    slot = i % 2
    ring_ref[slot] = hf_ref[...]

    def issue(tt, _):
        for u in range(DMA_UNROLL):
            t = tt * DMA_UNROLL + u
            src = ring_ref.at[slot, pl.ds(pl.multiple_of(t * nchunk, nchunk), nchunk)]
            for k in range(TOP_K):
                p = pos_ref[0, 0, t * TOP_K + k]
                dst = xs_ref.at[pl.ds(pl.multiple_of(p * nchunk, nchunk), nchunk)]
                pltpu.make_async_copy(src, dst, sem.at[slot]).start()
        return 0

    lax.fori_loop(0, t_tok // DMA_UNROLL, issue, 0)

    def drain(s):
        for k in range(TOP_K):
            pltpu.make_async_copy(ring_ref.at[s], xs_ref.at[pl.ds(0, t_tok * nchunk)],
                                  sem.at[s]).wait()

    @pl.when(i > 0)
    def _():
        drain(1 - slot)

    @pl.when(i == nstep - 1)
    def _():
        drain(slot)


def _combine_step(i, nstep, pos_ref, nxt_ref, y_ref, gate_ref, x_ref, g2_ref, o_ref, buf_ref, sem,
                  t_tok, nchunk):
    def gather(p_ref, slot):
        def issue(tt, _):
            for u in range(DMA_UNROLL):
                t = tt * DMA_UNROLL + u
                for k in range(TOP_K):
                    p = p_ref[0, 0, t * TOP_K + k]
                    src = y_ref.at[pl.ds(pl.multiple_of(p * nchunk, nchunk), nchunk)]
                    row = pl.multiple_of((k * t_tok + t) * nchunk, nchunk)
                    pltpu.make_async_copy(src, buf_ref.at[slot, pl.ds(row, nchunk)],
                                          sem.at[slot]).start()
            return 0

        lax.fori_loop(0, t_tok // DMA_UNROLL, issue, 0)

    @pl.when(i == 0)
    def _():
        gather(pos_ref, 0)

    @pl.when(i + 1 < nstep)
    def _():
        gather(nxt_ref, (i + 1) % 2)

    slot = i % 2
    pltpu.make_async_copy(y_ref.at[pl.ds(0, TOP_K * t_tok * nchunk)], buf_ref.at[slot],
                          sem.at[slot]).wait()
    gates = gate_ref[...]
    g2 = g2_ref[0]
    for c in range(nchunk):
        acc = None
        for k in range(TOP_K):
            rows = buf_ref[slot, pl.ds(k * t_tok * nchunk + c, t_tok, stride=nchunk), :]
            term = gates[:, k:k + 1] * rows
            acc = term if acc is None else acc + term
        sl = slice(c * LANES, (c + 1) * LANES)
        o_ref[:, sl] = x_ref[:, sl] + g2[:, sl] * acc


def _experts_step(j, used_ref, xs_ref, wgu_ref, bgu_ref, wd_ref, bd_ref, y_ref, ff, nchunk):
    @pl.when(j < used_ref[0])
    def _():
        tm = xs_ref.shape[0] // nchunk
        x = jnp.concatenate([xs_ref[pl.ds(c, tm, stride=nchunk), :] for c in range(nchunk)],
                            axis=1).astype(bf16)
        gu = _dot(x, wgu_ref[0]) + bgu_ref[0]
        g = jnp.minimum(gu[:, :ff], SWIGLU_LIMIT)
        u = jnp.clip(gu[:, ff:], -SWIGLU_LIMIT, SWIGLU_LIMIT)
        act = (u + 1.0) * (g * (1.0 / (1.0 + jnp.exp(-SWIGLU_ALPHA * g))))
        y = _dot(act.astype(bf16), wd_ref[0]) + bd_ref[0]
        for c in range(nchunk):
            y_ref[pl.ds(c, tm, stride=nchunk), :] = y[:, c * LANES:(c + 1) * LANES]

    @pl.when(j >= used_ref[0])
    def _():
        y_ref[...] = jnp.zeros(y_ref.shape, f32)


def _moe_stage_kernel(*refs, has_main, side, nside, t_tok, tm, nchunk, ff):
    it = iter(refs)
    take = lambda k: [next(it) for _ in range(k)]
    j = pl.program_id(0)
    if has_main:
        _, used_ref = take(2)
    if side == "dispatch":
        (zt_ref,) = take(1)
    if has_main:
        main_in = take(5)
    if side == "dispatch":
        side_in = take(2)
    elif side == "combine":
        side_in = take(6)
    if has_main:
        (y_ref,) = take(1)
    (side_out,) = take(1)
    scratch = list(it)

    if side == "dispatch":
        @pl.when(j < nside)
        def _():
            _dispatch_step(j, nside, zt_ref, *side_in, side_out, *scratch, t_tok, tm, nchunk)
    else:
        @pl.when(j < nside)
        def _():
            _combine_step(j, nside, *side_in, side_out, *scratch, t_tok, nchunk)
    if has_main:
        _experts_step(j, used_ref, *main_in, y_ref, ff, nchunk)


def _moe_stage(main, side, kind, *, seq, tm, t_tok, nchunk):
    has_main = main is not None
    nside = side["pos3"].shape[0]
    prefetch, in_specs, operands, out_specs, out_shape, scratch = [], [], [], [], [], []
    smem = pltpu.SMEM
    off = side["off"]
    clamp = lambda j: jnp.minimum(j, nside - 1)
    ff2 = d = None
    if has_main:
        _, d, ff2 = main["wgu"].shape
        eoff = main["eoff"]
        prefetch += [main["te"], main["used"]]
    if kind == "dispatch":
        prefetch += [side["zt"]]
    npre = len(prefetch)

    def spec(shape, fn, **kw):
        return pl.BlockSpec(shape, lambda j, *pre: fn(j, pre), **kw)

    if has_main:
        in_specs += [spec((tm * nchunk, LANES), lambda j, p: (jnp.minimum(j, p[1][0] - 1), 0)),
                     spec((1, d, ff2), lambda j, p: (p[0][j] + eoff, 0, 0)),
                     spec((1, 1, ff2), lambda j, p: (p[0][j], 0, 0)),
                     spec((1, ff2 // 2, d), lambda j, p: (p[0][j] + eoff, 0, 0)),
                     spec((1, 1, d), lambda j, p: (p[0][j], 0, 0))]
        operands += [main["xs"], main["wgu"], main["bgu"], main["wd"], main["bd"]]
    tokens = t_tok * TOP_K
    if kind == "dispatch":
        in_specs += [spec((1, 1, tokens), lambda j, p: (clamp(j), 0, 0), memory_space=smem),
                     spec((t_tok * nchunk, LANES), lambda j, p: (clamp(j) + off, 0))]
        operands += [side["pos3"], side["hflat"]]
    else:
        n, dd = side["x2"].shape
        per = seq // t_tok
        in_specs += [spec((1, 1, tokens), lambda j, p: (clamp(j), 0, 0), memory_space=smem),
                     spec((1, 1, tokens), lambda j, p: (clamp(j + 1), 0, 0), memory_space=smem),
                     pl.BlockSpec(memory_space=pl.ANY),
                     spec((t_tok, LANES), lambda j, p: (clamp(j) + off, 0)),
                     spec((t_tok, dd), lambda j, p: (clamp(j) + off, 0)),
                     spec((1, 1, dd), lambda j, p: ((clamp(j) + off) // per, 0, 0))]
        operands += [side["pos3"], side["pos3"], side["ys"], side["gate"], side["x2"], side["g2"]]
        x_operand = npre + len(operands) - 2
    if has_main:
        out_specs += [spec((tm * nchunk, LANES), lambda j, p: (j, 0))]
        out_shape += [jax.ShapeDtypeStruct(main["xs"].shape, f32)]
    aliases = {}
    if kind == "dispatch":
        out_specs += [pl.BlockSpec(memory_space=pl.ANY)]
        out_shape += [jax.ShapeDtypeStruct((side["rows_pad"] * nchunk, LANES), f32)]
        scratch += [pltpu.VMEM((tm * nchunk, LANES), f32), pltpu.VMEM((2, t_tok * nchunk, LANES), f32),
                    pltpu.SemaphoreType.DMA((2,)), pltpu.SemaphoreType.DMA]
    else:
        out_specs += [spec((t_tok, dd), lambda j, p: (clamp(j) + off, 0))]
        out_shape += [jax.ShapeDtypeStruct((n, dd), f32)]
        scratch += [pltpu.VMEM((2, TOP_K * t_tok * nchunk, LANES), f32),
                    pltpu.SemaphoreType.DMA((2,))]
        aliases = {x_operand: len(out_shape) - 1}
    grid = (main["te"].shape[0],) if has_main else (nside,)
    outs = pl.pallas_call(
        functools.partial(_moe_stage_kernel, has_main=has_main, side=kind, nside=nside,
                          t_tok=t_tok, tm=tm, nchunk=nchunk, ff=None if ff2 is None else ff2 // 2),
        out_shape=tuple(out_shape),
        grid_spec=pltpu.PrefetchScalarGridSpec(
            num_scalar_prefetch=npre, grid=grid, in_specs=in_specs, out_specs=tuple(out_specs),
            scratch_shapes=scratch),
        input_output_aliases=aliases,
        compiler_params=_cparams(("arbitrary",)),
        name=("moe_experts_" if has_main else "moe_") + kind,
    )(*prefetch, *operands)
    return outs


def _moe_plan(idx, rank, cnt, ne, tm, ntile_max):
    counts = cnt[:ne].astype(jnp.int32)
    ntile_e = (counts + tm - 1) // tm
    tile_end = jnp.cumsum(ntile_e)
    tile_start = tile_end - ntile_e
    used = tile_end[-1]
    e_idx = idx[:, :TOP_K].astype(jnp.int32)
    onehot = e_idx[:, :, None] == jnp.arange(ne, dtype=jnp.int32)[None, None, :]
    start = jnp.sum(jnp.where(onehot, (tile_start * tm)[None, None, :], 0), axis=-1)
    pos = start + rank[:, :TOP_K].astype(jnp.int32)
    tiles = jnp.minimum(jnp.arange(ntile_max, dtype=jnp.int32), used - 1)
    tile_expert = jnp.sum(tile_end[None, :] <= tiles[:, None], axis=1).astype(jnp.int32)
    ztile = jnp.where(ntile_e > 0, tile_end - 1, -1)
    ztile = jnp.concatenate([ztile, used.reshape(1)]).astype(jnp.int32)
    return pos, tile_expert, used.reshape(1).astype(jnp.int32), ztile


def _lane_tile(v):
    return jnp.tile(v.astype(f32), LANES // v.shape[0]).reshape(1, LANES)


def kernel(x, c, positions, ada_w, ada_b, norm1_g, norm2_g, sb_w_qkv, sb_w_o, dsa_w_in,
           dsa_q_gain, dsa_k_gain, dsa_w_o, router_w, router_b, exp_w_gu, exp_b_gu,
           exp_w_down, exp_b_down):
    batch, seq, d = x.shape
    depth = ada_w.shape[0]
    n = batch * seq
    ne = router_w.shape[-1]
    ff = exp_w_down.shape[2]
    topk = min(TOPK_MAX, seq // 4)
    assert seq % KB == 0 and d % LANES == 0 and ne <= LANES

    mod = _modulation(c, ada_w, ada_b)
    x2 = x.reshape(n, d)

    ncols = dsa_w_in.shape[-1]
    ncols_pad = -(-ncols // LANES) * LANES
    half = ROPE_DIMS // 2
    inv = jnp.exp(-math.log(ROPE_THETA) * (2.0 * jnp.arange(half, dtype=f32) / ROPE_DIMS))
    l64 = jnp.arange(LANES) % HEAD_DIM
    inv_lane = jnp.where(l64 < ROPE_DIMS, inv[l64 % half], 0.0).reshape(1, LANES)
    bd = (jnp.arange(LANES)[:, None] // HEAD_DIM == jnp.arange(LANES)[None, :] // HEAD_DIM)
    bd = bd.astype(bf16)
    pos = positions.reshape(n, 1)

    nchunk = d // LANES
    half_tok = n // 2
    nside = half_tok // MOE_TOK
    ntile_half = (half_tok * TOP_K) // MOE_TM + ne
    assert (half_tok * TOP_K) % MOE_TM == 0 and half_tok % MOE_TOK == 0 and seq % MOE_TOK == 0
    assert half_tok % POST_TM == 0
    tri = (jnp.arange(POST_TM)[:, None] > jnp.arange(POST_TM)[None, :]).astype(bf16)
    wgu_all, wd_all = _expert_weight_prep(exp_w_gu.reshape(depth * ne, d, 2 * ff),
                                          exp_w_down.reshape(depth * ne, ff, d))

    for layer in range(depth):
        m6 = [mod[layer][:, k * d:(k + 1) * d].reshape(batch, 1, d) for k in range(6)]
        sh1, sc1, g1, sh2, sc2, g2 = m6
        n1 = norm1_g[layer].reshape(1, d)
        n2 = norm2_g[layer].reshape(1, d)
        j = layer // 2
        if layer % 2 == 0:
            qkv = _proj(x2, n1, sc1, sh1, sb_w_qkv[j].astype(bf16), seq)
            o = _stick_breaking(qkv, batch, seq, d)
            wo = sb_w_o[j].astype(bf16)
        else:
            w = dsa_w_in[j]
            kv0 = d
            qi0 = d + 2 * LANES
            ki0 = qi0 + IDX_HEADS * IDX_DIM
            w = jnp.concatenate([w[:, :kv0], w[:, qi0:ki0], w[:, kv0:qi0], w[:, ki0:]], axis=1)
            w = jnp.pad(w, ((0, 0), (0, ncols_pad - ncols))).astype(bf16)
            proj, wi = _dsa_proj(x2, n1, sc1, sh1, w, pos, inv_lane, _lane_tile(dsa_q_gain[j]),
                                 _lane_tile(dsa_k_gain[j]), bd, seq)
            o = _dsa_attention(proj, wi, batch, seq, d, topk)
            wo = dsa_w_o[j].astype(bf16)
        rw = jnp.pad(router_w[layer], ((0, 0), (0, LANES - ne)))
        rw_hi = rw.astype(bf16)
        rw = jnp.stack([rw_hi, (rw - rw_hi.astype(f32)).astype(bf16)])
        rb = jnp.pad(router_b[layer], (0, LANES - ne), constant_values=-jnp.inf).reshape(1, LANES)
        x2, hflat, idx, gate, rank, cnt = _post_attention(o, wo, x2, g1, n2, sc2, sh2, rw, rb,
                                                          tri, seq, POST_TM)
        bgu = exp_b_gu[layer]
        bgu = jnp.concatenate([bgu[..., 0::2], bgu[..., 1::2]], axis=-1).reshape(ne, 1, 2 * ff)
        halves = []
        for hh in range(2):
            tok = slice(hh * half_tok, (hh + 1) * half_tok)
            spos, te, used, zt = _moe_plan(idx[tok], rank[tok], cnt[hh, 0], ne, MOE_TM, ntile_half)
            halves.append(dict(
                pos3=spos.reshape(nside, 1, MOE_TOK * TOP_K), zt=zt, off=hh * nside,
                experts=dict(te=te, used=used, wgu=wgu_all, bgu=bgu, wd=wd_all,
                             bd=exp_b_down[layer].reshape(ne, 1, d), eoff=layer * ne)))
        stage = functools.partial(_moe_stage, seq=seq, tm=MOE_TM, t_tok=MOE_TOK, nchunk=nchunk)

        def disp(h):
            return dict(zt=h["zt"], pos3=h["pos3"], hflat=hflat, off=h["off"],
                        rows_pad=ntile_half * MOE_TM)

        def comb(h, ys, xin):
            return dict(pos3=h["pos3"], ys=ys, gate=gate, x2=xin, g2=g2, off=h["off"])

        ha, hb = halves
        (xs_a,) = stage(None, disp(ha), "dispatch")
        ys_a, xs_b = stage(dict(ha["experts"], xs=xs_a), disp(hb), "dispatch")
        ys_b, x2 = stage(dict(hb["experts"], xs=xs_b), comb(ha, ys_a, x2), "combine")
        (x2,) = stage(None, comb(hb, ys_b, x2), "combine")
    return x2.reshape(batch, seq, d)
```

```python
import functools
import math

import jax
import jax.numpy as jnp
from jax import lax
from jax.experimental import pallas as pl
from jax.experimental.pallas import tpu as pltpu

HEAD_DIM = 64
DSA_KV_HEADS = 2
IDX_HEADS = 8
IDX_DIM = 64
CHUNK_SHIFT = 6
TOPK_MAX = 256
TOP_K = 4
ROPE_THETA = 500000.0
ROPE_DIMS = HEAD_DIM // 4
SWIGLU_ALPHA = 1.702
SWIGLU_LIMIT = 7.0
EPS = 1e-6

LANES = 128
QB = 128
KB = 256
DQ = 256
NEG = -1e30
INT_MIN = -(2 ** 31)
KEY_NEG_INF = -2139095041
NORM_SLACK = 1.02
L_TINY = 1e-30
VMEM_LIMIT = 48 * 1024 * 1024
POST_TM = 512
MOE_TM = 512
MOE_TOK = 256
DMA_UNROLL = 8
SB_GROUP = 6
SB_WINDOW = 3
SB_CUTOFF = -110.0

f32 = jnp.float32
bf16 = jnp.bfloat16


def _cparams(sem):
    return pltpu.CompilerParams(dimension_semantics=sem, vmem_limit_bytes=VMEM_LIMIT)


def _dot(a, b):
    return jnp.dot(a, b, preferred_element_type=f32)


def _dot_nt(a, b):
    return lax.dot_general(a, b, (((1,), (1,)), ((), ())), preferred_element_type=f32)


def _dot_split(x, m01, passes):
    acc = None
    r = x
    for p in range(passes):
        t = r.astype(bf16)
        d = _dot(t, m01)
        acc = d if acc is None else acc + d
        if p + 1 < passes:
            r = r - t.astype(f32)
    return acc


def _norm_mod(x, g, sc, sh):
    ms = jnp.mean(x * x, axis=-1, keepdims=True)
    return (x * lax.rsqrt(ms + EPS) * g) * (1.0 + sc) + sh


def _mod_kernel(c_ref, w_ref, b_ref, o_ref):
    c = c_ref[...]
    cs = c * (1.0 / (1.0 + jnp.exp(-c)))
    o_ref[0] = jnp.dot(cs, w_ref[0], preferred_element_type=f32,
                       precision=lax.Precision.HIGHEST) + b_ref[0]


def _modulation(c, ada_w, ada_b):
    depth, d, n6 = ada_w.shape
    b = c.shape[0]
    tn = 1024
    return pl.pallas_call(
        _mod_kernel,
        out_shape=jax.ShapeDtypeStruct((depth, b, n6), f32),
        grid=(depth, n6 // tn),
        in_specs=[pl.BlockSpec((b, d), lambda l, j: (0, 0)),
                  pl.BlockSpec((1, d, tn), lambda l, j: (l, 0, j)),
                  pl.BlockSpec((1, 1, tn), lambda l, j: (l, 0, j))],
        out_specs=pl.BlockSpec((1, b, tn), lambda l, j: (l, 0, j)),
        compiler_params=_cparams(("arbitrary", "arbitrary")),
        name="adaln_mod",
    )(c, ada_w, ada_b.reshape(depth, 1, n6))


def _proj_kernel(x_ref, g_ref, sc_ref, sh_ref, w_ref, o_ref):
    h = _norm_mod(x_ref[...], g_ref[...], sc_ref[0], sh_ref[0])
    o_ref[...] = _dot(h.astype(bf16), w_ref[...]).astype(o_ref.dtype)


def _proj(x2, g, sc, sh, w, seq, tm=512):
    n, d = x2.shape
    nc = w.shape[1]
    per = seq // tm
    return pl.pallas_call(
        _proj_kernel,
        out_shape=jax.ShapeDtypeStruct((n, nc), bf16),
        grid=(n // tm,),
        in_specs=[pl.BlockSpec((tm, d), lambda i: (i, 0)),
                  pl.BlockSpec((1, d), lambda i: (0, 0)),
                  pl.BlockSpec((1, 1, d), lambda i: (i // per, 0, 0)),
                  pl.BlockSpec((1, 1, d), lambda i: (i // per, 0, 0)),
                  pl.BlockSpec((d, nc), lambda i: (0, 0))],
        out_specs=pl.BlockSpec((tm, nc), lambda i: (i, 0)),
        compiler_params=_cparams(("arbitrary",)),
        name="sb_qkv_proj",
    )(x2, g, sc, sh, w)


def _sb_kernel(q_ref, k_ref, v_ref, o_ref, k0_ref, k1_ref):
    seq = q_ref.shape[0]
    nqb = seq // QB
    lane = lax.broadcasted_iota(jnp.int32, (1, LANES), 1)
    kk = k_ref[...]
    zero = jnp.zeros_like(kk)
    k0_ref[...] = jnp.where(lane < HEAD_DIM, kk, zero)
    k1_ref[...] = jnp.where(lane >= HEAD_DIM, kk, zero)
    row = lax.broadcasted_iota(jnp.int32, (QB, QB), 0)
    col = lax.broadcasted_iota(jnp.int32, (QB, QB), 1)
    before = col < row
    tri = jnp.where(row > col, 1.0, 0.0).astype(bf16)

    def windows(jobs, nb, diag):
        zs = [_dot_nt(q, kh_ref[pl.ds(k0, nb * QB), :]) for q, kh_ref, k0, _ in jobs]
        lms, lss = [], []
        for z in zs:
            l1p = jnp.log(1.0 + jnp.exp(-jnp.abs(z)))
            lm = jnp.minimum(-z, 0.0) - l1p
            lss.append(lm + z)
            parts = [lm[:, s * QB:(s + 1) * QB] for s in range(nb)]
            if diag:
                parts[-1] = jnp.where(before, parts[-1], 0.0)
            lms.append(parts)
        cums = [[_dot_split(p, tri, 2) for p in parts] for parts in lms]
        outs = []
        for (q, kh_ref, k0, c_in), parts, cum, ls in zip(jobs, lms, cums, lss):
            c = c_in
            probs = [None] * nb
            for s in reversed(range(nb)):
                a = jnp.exp(ls[:, s * QB:(s + 1) * QB] + (cum[s] + c))
                if diag and s == nb - 1:
                    a = jnp.where(before, a, 0.0)
                probs[s] = a.astype(bf16)
                c = c + jnp.sum(parts[s], axis=1, keepdims=True)
            outs.append((c, probs[0] if nb == 1 else jnp.concatenate(probs, axis=1)))
        return [(c, _dot(a_all, v_ref[pl.ds(k0, nb * QB), :]))
                for (c, a_all), (_, _, k0, _) in zip(outs, jobs)]

    def qheads(blocks, nb):
        jobs, meta = [], []
        zc = jnp.zeros((QB, 1), f32)
        for i in blocks:
            q0 = i * QB
            k0 = q0 - (nb - 1) * QB
            if not isinstance(i, int):
                q0 = pl.multiple_of(q0, QB)
                k0 = pl.multiple_of(k0, QB)
            q = q_ref[pl.ds(q0, QB), :] * (HEAD_DIM ** -0.5)
            jobs += [(q, k0_ref, k0, zc), (q, k1_ref, k0, zc)]
            meta.append((q0, q))
        res = windows(jobs, nb, True)
        return [(q0, q) + res[2 * n] + res[2 * n + 1] for n, (q0, q) in enumerate(meta)]

    def qhead(i, nb):
        return qheads([i], nb)[0]

    def qtail(i, nb, st):
        q0, q, c0, a0, c1, a1 = st
        if nb <= SB_WINDOW and not (isinstance(i, int) and i < nb):
            def cond(s):
                jb, c0, _, c1, _ = s
                live = jnp.max(jnp.maximum(c0, c1)) > SB_CUTOFF
                return jnp.logical_and(jb >= 0, live)

            def body(s):
                jb, c0, a0, c1, a1 = s
                kb = pl.multiple_of(jb * QB, QB)
                (c0, d0), (c1, d1) = windows([(q, k0_ref, kb, c0), (q, k1_ref, kb, c1)], 1, False)
                return jb - 1, c0, a0 + d0, c1, a1 + d1

            jb0 = jnp.asarray(i - nb, jnp.int32)
            _, c0, a0, c1, a1 = lax.while_loop(cond, body, (jb0, c0, a0, c1, a1))
        o_ref[pl.ds(q0, QB), :] = jnp.where(lane < HEAD_DIM, a0, a1).astype(o_ref.dtype)

    first = min(SB_WINDOW, nqb)
    npairs = (nqb - first) // SB_GROUP
    singles = list(range(first + SB_GROUP * npairs, nqb))
    for i in range(first):
        if i + 1 == SB_WINDOW and singles:
            continue
        qtail(i, i + 1, qhead(i, i + 1))

    def pair(j, _):
        i = first + SB_GROUP * j
        blocks = [i + n for n in range(SB_GROUP)]
        for b, st in zip(blocks, qheads(blocks, SB_WINDOW)):
            qtail(b, SB_WINDOW, st)
        return 0

    lax.fori_loop(0, npairs, pair, 0)
    if singles:
        group = ([first - 1] if first == SB_WINDOW else []) + singles
        for i, st in zip(group, qheads(group, SB_WINDOW)):
            qtail(i, SB_WINDOW, st)


def _stick_breaking(qkv, batch, seq, d):
    npair = d // LANES
    return pl.pallas_call(
        _sb_kernel,
        out_shape=jax.ShapeDtypeStruct((batch * seq, d), bf16),
        grid=(batch, npair),
        in_specs=[pl.BlockSpec((seq, LANES), lambda b, p: (b, p)),
                  pl.BlockSpec((seq, LANES), lambda b, p: (b, npair + p)),
                  pl.BlockSpec((seq, LANES), lambda b, p: (b, 2 * npair + p))],
        out_specs=pl.BlockSpec((seq, LANES), lambda b, p: (b, p)),
        scratch_shapes=[pltpu.VMEM((seq, LANES), bf16), pltpu.VMEM((seq, LANES), bf16)],
        compiler_params=_cparams(("arbitrary", "arbitrary")),
        name="stick_breaking_attn",
    )(qkv, qkv, qkv)


def _dsa_proj_kernel(x_ref, g_ref, sc_ref, sh_ref, w_ref, pos_ref, inv_ref, qg_ref, kg_ref,
                     bd_ref, o_ref, wi_ref, *, d):
    h = _norm_mod(x_ref[...], g_ref[...], sc_ref[0], sh_ref[0])
    p = _dot(h.astype(bf16), w_ref[...])
    lane = lax.broadcasted_iota(jnp.int32, (1, LANES), 1)
    ang = pos_ref[...].astype(f32) * inv_ref[...]
    cos_t = jnp.cos(ang)
    sin_t = jnp.sin(ang)
    upper = (lane % ROPE_DIMS) >= (ROPE_DIMS // 2)
    s_up = jnp.where(upper, sin_t, 0.0)
    s_lo = jnp.where(upper, 0.0, -sin_t)
    half = ROPE_DIMS // 2

    def rope(y):
        return y * cos_t + pltpu.roll(y, half, 1) * s_up + pltpu.roll(y, LANES - half, 1) * s_lo

    def headnorm(y, gain):
        ms = _dot_split(y * y, bd_ref[...], 2) * (1.0 / HEAD_DIM)
        return y * lax.rsqrt(ms + EPS) * gain

    nq = d // LANES
    for c in range(nq):
        y = p[:, c * LANES:(c + 1) * LANES]
        o_ref[:, c * LANES:(c + 1) * LANES] = rope(headnorm(y, qg_ref[...])).astype(bf16)
    c0 = nq
    for c in range(c0, c0 + IDX_HEADS * IDX_DIM // LANES):
        o_ref[:, c * LANES:(c + 1) * LANES] = rope(p[:, c * LANES:(c + 1) * LANES]).astype(bf16)
    c0 += IDX_HEADS * IDX_DIM // LANES
    y = p[:, c0 * LANES:(c0 + 1) * LANES]
    o_ref[:, c0 * LANES:(c0 + 1) * LANES] = rope(headnorm(y, kg_ref[...])).astype(bf16)
    c0 += 1
    o_ref[:, c0 * LANES:(c0 + 1) * LANES] = p[:, c0 * LANES:(c0 + 1) * LANES].astype(bf16)
    c0 += 1
    y = p[:, c0 * LANES:(c0 + 1) * LANES]
    o_ref[:, c0 * LANES:(c0 + 1) * LANES] = jnp.where(lane < IDX_DIM, rope(y), 0.0).astype(bf16)
    wi = pltpu.roll(y, LANES - IDX_DIM, 1) * (IDX_HEADS ** -0.5) * (IDX_DIM ** -0.5)
    wi_ref[...] = jnp.where(lane < IDX_HEADS, wi, 0.0)


def _dsa_proj(x2, g, sc, sh, w, pos, inv_lane, qg, kg, bd, seq, tm=512):
    n, d = x2.shape
    nc = w.shape[1]
    per = seq // tm
    return pl.pallas_call(
        functools.partial(_dsa_proj_kernel, d=d),
        out_shape=(jax.ShapeDtypeStruct((n, nc), bf16), jax.ShapeDtypeStruct((n, LANES), f32)),
        grid=(n // tm,),
        in_specs=[pl.BlockSpec((tm, d), lambda i: (i, 0)),
                  pl.BlockSpec((1, d), lambda i: (0, 0)),
                  pl.BlockSpec((1, 1, d), lambda i: (i // per, 0, 0)),
                  pl.BlockSpec((1, 1, d), lambda i: (i // per, 0, 0)),
                  pl.BlockSpec((d, nc), lambda i: (0, 0)),
                  pl.BlockSpec((tm, 1), lambda i: (i, 0)),
                  pl.BlockSpec((1, LANES), lambda i: (0, 0)),
                  pl.BlockSpec((1, LANES), lambda i: (0, 0)),
                  pl.BlockSpec((1, LANES), lambda i: (0, 0)),
                  pl.BlockSpec((LANES, LANES), lambda i: (0, 0))],
        out_specs=(pl.BlockSpec((tm, nc), lambda i: (i, 0)),
                   pl.BlockSpec((tm, LANES), lambda i: (i, 0))),
        compiler_params=_cparams(("arbitrary",)),
        name="dsa_in_proj",
    )(x2, g, sc, sh, w, pos, inv_lane, qg, kg, bd)


def _dsa_kernel(q_ref, k_ref, v_ref, qi_ref, ki_ref, wi_ref, o_ref,
                kk_ref, vv_ref, ki2_ref, keys_ref, bias_ref, m_ref, acc_ref, thr_ref, need_ref,
                qs_ref, kmax_ref, *, topk):
    seq = k_ref.shape[0]
    nchunk = q_ref.shape[1] // LANES
    group = (2 * nchunk) // DSA_KV_HEADS
    i = pl.program_id(1)
    lane = lax.broadcasted_iota(jnp.int32, (1, LANES), 1)
    lo = lane < HEAD_DIM

    @pl.when(i == 0)
    def _():
        k = k_ref[...]
        v = v_ref[...]
        ki = ki_ref[...]
        kr = pltpu.roll(k.astype(f32), HEAD_DIM, 1).astype(bf16)
        vr = pltpu.roll(v.astype(f32), HEAD_DIM, 1).astype(bf16)
        zero = jnp.zeros_like(k)
        kk_ref[0] = jnp.where(lo, k, zero)
        kk_ref[1] = jnp.where(lo, zero, kr)
        kk_ref[2] = jnp.where(lo, kr, zero)
        kk_ref[3] = jnp.where(lo, zero, k)
        lane_full = lax.broadcasted_iota(jnp.int32, k.shape, 1)
        oh = [jnp.where(lane_full < HEAD_DIM, 1.0, 0.0).astype(bf16),
              jnp.where(lane_full < HEAD_DIM, 0.0, 1.0).astype(bf16)]
        vv_ref[0] = jnp.concatenate([jnp.where(lo, v, zero), oh[0]], axis=1)
        vv_ref[1] = jnp.concatenate([jnp.where(lo, zero, vr), oh[1]], axis=1)
        vv_ref[2] = jnp.concatenate([jnp.where(lo, vr, zero), oh[0]], axis=1)
        vv_ref[3] = jnp.concatenate([jnp.where(lo, zero, v), oh[1]], axis=1)
        ki2_ref[0] = ki
        ki2_ref[1] = pltpu.roll(ki.astype(f32), HEAD_DIM, 1).astype(bf16)
        kf = k.astype(f32)
        rr = lax.broadcasted_iota(jnp.int32, (LANES, LANES), 0) < HEAD_DIM
        cc = lax.broadcasted_iota(jnp.int32, (LANES, LANES), 1) < HEAD_DIM
        bd = jnp.where(rr == cc, 1.0, 0.0).astype(bf16)
        n2 = jnp.max(_dot((kf * kf).astype(bf16), bd), axis=0, keepdims=True)
        for g in range(DSA_KV_HEADS):
            sel = lo if g == 0 else jnp.logical_not(lo)
            top = jnp.max(jnp.where(sel, n2, 0.0), axis=1, keepdims=True)
            kmax_ref[g] = jnp.broadcast_to(top, (1, LANES))

    nkb = i + 1
    qrow = lax.broadcasted_iota(jnp.int32, (DQ, KB), 0) + i * DQ
    kcol = lax.broadcasted_iota(jnp.int32, (DQ, KB), 1)

    wi = wi_ref[...]
    wcols = [wi[:, hd:hd + 1] for hd in range(IDX_HEADS)]

    def score_block(jb, _):
        k0 = pl.multiple_of(jb * KB, KB)
        score = jnp.zeros((DQ, KB), f32)
        for hd in range(IDX_HEADS):
            qc = qi_ref[:, (hd // 2) * LANES:(hd // 2 + 1) * LANES]
            kb = ki2_ref[hd % 2, pl.ds(k0, KB), :]
            score = score + jnp.maximum(_dot_nt(qc, kb), 0.0) * wcols[hd]
        score = score + 0.0
        bits = pltpu.bitcast(score, jnp.int32)
        key = bits ^ ((bits >> 31) & 0x7FFFFFFF)
        adm = ((kcol + k0) >> CHUNK_SHIFT) <= (qrow >> CHUNK_SHIFT)
        keys_ref[jb] = jnp.where(adm, key, INT_MIN)
        return 0

    lax.fori_loop(0, nkb, score_block, 0)

    kf = float(topk)
    ones_l = jnp.ones((LANES, LANES), bf16)

    def search(nk):
        def count(r, pred):
            part = None
            for jb in range(nk):
                for hl in range(KB // LANES):
                    key = keys_ref[jb, r * QB:(r + 1) * QB, hl * LANES:(hl + 1) * LANES]
                    hit = jnp.where(pred(key), 1.0, 0.0)
                    part = hit if part is None else part + hit
            return _dot(part.astype(bf16), ones_l)

        groups = range(DQ // QB)
        t0 = tuple(jnp.where(count(r, lambda key: key >= 0) >= kf, 0, INT_MIN).astype(jnp.int32)
                   for r in groups)

        def bit_step(s, ts):
            bit = jnp.left_shift(jnp.int32(1), 30 - s)
            out = []
            for r in groups:
                cand = ts[r] + bit
                enough = count(r, lambda key, cand=cand: key >= cand) >= kf
                out.append(jnp.where(enough, cand, ts[r]))
            return tuple(out)

        ts = lax.fori_loop(0, 31, bit_step, t0)
        for r in groups:
            thr_ref[r * QB:(r + 1) * QB, :] = ts[r]
            need_ref[r * QB:(r + 1) * QB, :] = kf - count(r, lambda key, t=ts[r]: key > t)

    for nk in range(1, seq // KB + 1):
        @pl.when(nkb == nk)
        def _(nk=nk):
            search(nk)

    thr = jnp.concatenate([thr_ref[...]] * (KB // LANES), axis=1)
    need = jnp.concatenate([need_ref[...]] * (KB // LANES), axis=1)

    r2 = lax.broadcasted_iota(jnp.int32, (KB, KB), 0)
    c2 = lax.broadcasted_iota(jnp.int32, (KB, KB), 1)
    tri = jnp.where(r2 < c2, 1.0, 0.0).astype(bf16)
    ones_k = jnp.ones((KB, KB), bf16)

    def bias_block(jb, carry):
        key = keys_ref[jb]
        eq = key == thr
        eqb = jnp.where(eq, 1.0, 0.0).astype(bf16)
        rank = _dot(eqb, tri) + carry
        sel = (key > thr) | (eq & (rank < need))
        sel = sel & (key > KEY_NEG_INF)
        bias_ref[jb] = jnp.where(sel, 0.0, NEG)
        return carry + _dot(eqb, ones_k)

    lax.fori_loop(0, nkb, bias_block, jnp.zeros((DQ, KB), f32))

    cpg = group // 2
    for g in range(DSA_KV_HEADS):
        qs_ref[g] = jnp.concatenate(
            [q_ref[:, c * LANES:(c + 1) * LANES] for c in range(g * cpg, (g + 1) * cpg)],
            axis=0) * (HEAD_DIM ** -0.5)

    def scores(g, hf, k0, bias):
        return _dot_nt(qs_ref[g], kk_ref[2 * g + hf, pl.ds(k0, KB), :]) + bias

    def sweep_max():
        m_ref[...] = jnp.full(m_ref.shape, NEG, f32)

        def max_block(jb, _):
            k0 = pl.multiple_of(jb * KB, KB)
            bias = jnp.concatenate([bias_ref[jb]] * cpg, axis=0)
            for g in range(DSA_KV_HEADS):
                for hf in range(2):
                    s = scores(g, hf, k0, bias)
                    fold = jnp.maximum(s[:, :LANES], s[:, LANES:])
                    m_ref[2 * g + hf] = jnp.maximum(m_ref[2 * g + hf], fold)
            return 0

        lax.fori_loop(0, nkb, max_block, 0)
        for h in range(2 * DSA_KV_HEADS):
            m = jnp.max(m_ref[h], axis=1, keepdims=True)
            m_ref[h] = jnp.broadcast_to(m, (cpg * DQ, LANES))

    def sweep_exp():
        acc_ref[...] = jnp.zeros(acc_ref.shape, f32)

        def attn_block(jb, _):
            k0 = pl.multiple_of(jb * KB, KB)
            bias = jnp.concatenate([bias_ref[jb]] * cpg, axis=0)
            for g in range(DSA_KV_HEADS):
                ps = []
                for hf in range(2):
                    m = m_ref[2 * g + hf]
                    s = scores(g, hf, k0, bias)
                    ps.append(jnp.exp(s - jnp.concatenate([m, m], axis=1)).astype(bf16))
                p2 = jnp.concatenate(ps, axis=1)
                v2 = jnp.concatenate([vv_ref[2 * g, pl.ds(k0, KB), :],
                                      vv_ref[2 * g + 1, pl.ds(k0, KB), :]], axis=0)
                acc_ref[g] += _dot(p2, v2)
            return 0

        lax.fori_loop(0, nkb, attn_block, 0)

    def write_out():
        for c in range(nchunk):
            r = acc_ref[c // cpg, (c % cpg) * DQ:(c % cpg + 1) * DQ, :]
            o_ref[:, c * LANES:(c + 1) * LANES] = (r[:, :LANES] / r[:, LANES:]).astype(o_ref.dtype)

    for g in range(DSA_KV_HEADS):
        q = qs_ref[g].astype(f32)
        q2 = (q * q).astype(bf16)
        for hf in range(2):
            half = jnp.where((lax.broadcasted_iota(jnp.int32, (LANES, LANES), 0) < HEAD_DIM)
                             == (hf == 0), 1.0, 0.0).astype(bf16)
            qn2 = _dot(q2, half)
            m_ref[2 * g + hf] = NORM_SLACK * jnp.sqrt(qn2 * kmax_ref[g])
    sweep_exp()
    lmin = jnp.min(acc_ref[:, :, LANES:])

    @pl.when(lmin > L_TINY)
    def _():
        write_out()

    @pl.when(jnp.logical_not(lmin > L_TINY))
    def _():
        sweep_max()
        sweep_exp()
        write_out()


def group_even(d):
    return ((d // HEAD_DIM) // DSA_KV_HEADS) % 2 == 0


def _dsa_attention(proj, wi, batch, seq, d, topk):
    nqb = seq // DQ
    nchunk = d // LANES
    cpg = nchunk // DSA_KV_HEADS
    qiw = IDX_HEADS * IDX_DIM
    assert d % qiw == 0 and group_even(d) and DQ == KB
    qicol = d // qiw
    kcol = nchunk + qiw // LANES
    vcol = kcol + 1
    kicol = kcol + 2
    return pl.pallas_call(
        functools.partial(_dsa_kernel, topk=topk),
        out_shape=jax.ShapeDtypeStruct((batch * seq, d), bf16),
        grid=(batch, nqb),
        in_specs=[pl.BlockSpec((DQ, d), lambda b, i: (b * nqb + i, 0)),
                  pl.BlockSpec((seq, LANES), lambda b, i: (b, kcol)),
                  pl.BlockSpec((seq, LANES), lambda b, i: (b, vcol)),
                  pl.BlockSpec((DQ, IDX_HEADS * IDX_DIM), lambda b, i: (b * nqb + i, qicol)),
                  pl.BlockSpec((seq, LANES), lambda b, i: (b, kicol)),
                  pl.BlockSpec((DQ, LANES), lambda b, i: (b * nqb + i, 0))],
        out_specs=pl.BlockSpec((DQ, d), lambda b, i: (b * nqb + i, 0)),
        scratch_shapes=[pltpu.VMEM((4, seq, LANES), bf16),
                        pltpu.VMEM((4, seq, 2 * LANES), bf16),
                        pltpu.VMEM((2, seq, LANES), bf16),
                        pltpu.VMEM((seq // KB, DQ, KB), jnp.int32),
                        pltpu.VMEM((seq // KB, DQ, KB), f32),
                        pltpu.VMEM((2 * DSA_KV_HEADS, cpg * DQ, LANES), f32),
                        pltpu.VMEM((DSA_KV_HEADS, cpg * DQ, 2 * LANES), f32),
                        pltpu.VMEM((DQ, LANES), jnp.int32),
                        pltpu.VMEM((DQ, LANES), f32),
                        pltpu.VMEM((DSA_KV_HEADS, cpg * DQ, LANES), bf16),
                        pltpu.VMEM((DSA_KV_HEADS, 1, LANES), f32)],
        compiler_params=_cparams(("arbitrary", "arbitrary")),
        name="dsa_sparse_attn",
    )(proj, proj, proj, proj, proj, wi)


def _post_kernel(o_ref, wo_ref, x_ref, g1_ref, n2_ref, sc_ref, sh_ref, rw_ref, rb_ref, tri_ref,
                 xo_ref, hf_ref, idx_ref, gate_ref, rank_ref, cnt_ref, run_ref):
    half = pl.num_programs(0) // 2

    @pl.when((pl.program_id(0) == 0) | (pl.program_id(0) == half))
    def _():
        run_ref[...] = jnp.zeros(run_ref.shape, f32)

    y = _dot(o_ref[...], wo_ref[...])
    x = x_ref[...] + g1_ref[0] * y
    xo_ref[...] = x
    h = _norm_mod(x, n2_ref[...], sc_ref[0], sh_ref[0])
    tm, d = h.shape
    nchunk = d // LANES
    for c in range(nchunk):
        hf_ref[pl.ds(c, tm, stride=nchunk), :] = h[:, c * LANES:(c + 1) * LANES]
    h_hi = h.astype(bf16)
    h_lo = (h - h_hi.astype(f32)).astype(bf16)
    logits = (_dot(h_hi, rw_ref[0]) + _dot(h_hi, rw_ref[1]) + _dot(h_lo, rw_ref[0])) + rb_ref[...]
    lane = lax.broadcasted_iota(jnp.int32, logits.shape, 1).astype(f32)
    work = logits
    vals, idxs, hits = [], [], []
    for _ in range(TOP_K):
        m = jnp.max(work, axis=1, keepdims=True)
        idx = jnp.min(jnp.where(work == m, lane, float(LANES)), axis=1, keepdims=True)
        hit = lane == idx
        vals.append(m)
        idxs.append(idx)
        hits.append(hit)
        work = jnp.where(hit, -jnp.inf, work)
    es = [jnp.exp(v - vals[0]) for v in vals]
    inv = 1.0 / (es[0] + es[1] + es[2] + es[3])
    multi = jnp.zeros_like(logits)
    for hit in hits:
        multi = multi + jnp.where(hit, 1.0, 0.0)
    before = _dot(tri_ref[...], multi.astype(bf16)) + run_ref[...]
    idx_o = jnp.zeros_like(logits)
    gate_o = jnp.zeros_like(logits)
    rank_o = jnp.zeros_like(logits)
    for k in range(TOP_K):
        slot = lane == float(k)
        rank = jnp.sum(jnp.where(hits[k], before, 0.0), axis=1, keepdims=True)
        idx_o = jnp.where(slot, idxs[k], idx_o)
        gate_o = jnp.where(slot, es[k] * inv, gate_o)
        rank_o = jnp.where(slot, rank, rank_o)
    idx_ref[...] = idx_o
    gate_ref[...] = gate_o
    rank_ref[...] = rank_o
    run_ref[...] += jnp.sum(multi, axis=0, keepdims=True)
    cnt_ref[0] = jnp.broadcast_to(run_ref[...], cnt_ref.shape[1:])


def _post_attention(o, wo, x2, g1, n2, sc2, sh2, rw, rb, tri, seq, tm):
    n, d = x2.shape
    per = seq // tm
    nchunk = d // LANES
    row = lambda i: (i, 0)
    fix = lambda i: (0, 0)
    bat = lambda i: (i // per, 0, 0)
    lane_out = jax.ShapeDtypeStruct((n, LANES), f32)
    return pl.pallas_call(
        _post_kernel,
        out_shape=(jax.ShapeDtypeStruct((n, d), f32), jax.ShapeDtypeStruct((n * nchunk, LANES), f32),
                   lane_out, lane_out, lane_out, jax.ShapeDtypeStruct((2, 8, LANES), f32)),
        grid=(n // tm,),
        in_specs=[pl.BlockSpec((tm, d), row), pl.BlockSpec((d, d), fix),
                  pl.BlockSpec((tm, d), row), pl.BlockSpec((1, 1, d), bat),
                  pl.BlockSpec((1, d), fix), pl.BlockSpec((1, 1, d), bat),
                  pl.BlockSpec((1, 1, d), bat), pl.BlockSpec((2, d, LANES), lambda i: (0, 0, 0)),
                  pl.BlockSpec((1, LANES), fix), pl.BlockSpec((tm, tm), fix)],
        out_specs=(pl.BlockSpec((tm, d), row), pl.BlockSpec((tm * nchunk, LANES), row),
                   pl.BlockSpec((tm, LANES), row), pl.BlockSpec((tm, LANES), row),
                   pl.BlockSpec((tm, LANES), row),
                   pl.BlockSpec((1, 8, LANES), lambda i: (i // (n // tm // 2), 0, 0))),
        scratch_shapes=[pltpu.VMEM((1, LANES), f32)],
        compiler_params=_cparams(("arbitrary",)),
        name="attn_out_norm_router",
    )(o, wo, x2, g1, n2, sc2, sh2, rw, rb, tri)


def _deint_kernel(w_ref, wd_ref, p_ref, o_ref, od_ref):
    ff = o_ref.shape[2] // 2
    wide = 2 * LANES
    for b in range(o_ref.shape[2] // wide):
        x = w_ref[0, :, b * wide:(b + 1) * wide].astype(bf16)
        r = _dot(x, p_ref[...]).astype(bf16)
        o_ref[0, :, b * LANES:(b + 1) * LANES] = r[:, :LANES]
        o_ref[0, :, ff + b * LANES:ff + (b + 1) * LANES] = r[:, LANES:]
    od_ref[...] = wd_ref[...].astype(bf16)


def _expert_weight_prep(w, wd):
    ne, d, ff2 = w.shape
    wide = 2 * LANES
    src = jnp.arange(wide)
    perm = jnp.where(src < LANES, 2 * src, 2 * (src - LANES) + 1)
    p = (jnp.arange(wide)[:, None] == perm[None, :]).astype(bf16)
    blk = lambda e: (e, 0, 0)
    return pl.pallas_call(
        _deint_kernel,
        out_shape=(jax.ShapeDtypeStruct((ne, d, ff2), bf16),
                   jax.ShapeDtypeStruct(wd.shape, bf16)),
        grid=(ne,),
        in_specs=[pl.BlockSpec((1, d, ff2), blk), pl.BlockSpec((1,) + wd.shape[1:], blk),
                  pl.BlockSpec((wide, wide), lambda e: (0, 0))],
        out_specs=(pl.BlockSpec((1, d, ff2), blk), pl.BlockSpec((1,) + wd.shape[1:], blk)),
        compiler_params=_cparams(("arbitrary",)),
        name="expert_weight_prep",
    )(w, wd, p)


def _dispatch_step(i, nstep, zt_ref, pos_ref, hf_ref, xs_ref, zbuf_ref, ring_ref, sem, zsem,
                   t_tok, tm, nchunk, fused=None):
    ne = zt_ref.shape[0] - 1
    ntile = xs_ref.shape[0] // (tm * nchunk)

    def zero_tile(e):
        return zt_ref[e] if e < ne else zt_ref[ne] + (e - ne)

    def zero_copy(e):
        start = pl.multiple_of(zero_tile(e) * (tm * nchunk), tm * nchunk)
        return pltpu.make_async_copy(zbuf_ref, xs_ref.at[pl.ds(start, tm * nchunk)], zsem)

    def zero_wanted(e):
        return zero_tile(e) >= 0 if e < ne else zero_tile(e) < ntile

    @pl.when(i == 0)
    def _():
        zbuf_ref[...] = jnp.zeros(zbuf_ref.shape, f32)
        for e in range(2 * ne):
            @pl.when(zero_wanted(e))
            def _():
                zero_copy(e).start()
        for e in range(2 * ne):
            @pl.when(zero_wanted(e))
            def _():
                zero_copy(e).wait()

    slot = i % 2
    ring_ref[slot] = hf_ref[...]

    def issue(tt, _):
        for u in range(DMA_UNROLL):
            t = tt * DMA_UNROLL + u
            if isinstance(t, int):
                src = ring_ref.at[slot, pl.ds(t * nchunk, nchunk)]
            else:
                src = ring_ref.at[slot, pl.ds(pl.multiple_of(t * nchunk, nchunk), nchunk)]
            for k in range(TOP_K):
                p = pos_ref[0, 0, t * TOP_K + k]
                dst = xs_ref.at[pl.ds(pl.multiple_of(p * nchunk, nchunk), nchunk)]
                pltpu.make_async_copy(src, dst, sem.at[slot]).start()
        return 0

    if fused is None:
        lax.fori_loop(0, t_tok // DMA_UNROLL, issue, 0)
    else:
        for tt in range(t_tok // DMA_UNROLL):
            issue(tt, 0)
        fused()

    def drain(s):
        for k in range(TOP_K):
            pltpu.make_async_copy(ring_ref.at[s], xs_ref.at[pl.ds(0, t_tok * nchunk)],
                                  sem.at[s]).wait()

    @pl.when(i > 0)
    def _():
        drain(1 - slot)

    @pl.when(i == nstep - 1)
    def _():
        drain(slot)


def _combine_step(i, nstep, pos_ref, nxt_ref, y_ref, gate_ref, x_ref, g2_ref, o_ref, buf_ref, sem,
                  t_tok, nchunk, fused=None):
    def gather(p_ref, slot, straight=False):
        def issue(tt, _):
            for u in range(DMA_UNROLL):
                t = tt * DMA_UNROLL + u
                for k in range(TOP_K):
                    p = p_ref[0, 0, t * TOP_K + k]
                    src = y_ref.at[pl.ds(pl.multiple_of(p * nchunk, nchunk), nchunk)]
                    row = (k * t_tok + t) * nchunk
                    if not isinstance(row, int):
                        row = pl.multiple_of(row, nchunk)
                    pltpu.make_async_copy(src, buf_ref.at[slot, pl.ds(row, nchunk)],
                                          sem.at[slot]).start()
            return 0

        if straight:
            for tt in range(t_tok // DMA_UNROLL):
                issue(tt, 0)
        else:
            lax.fori_loop(0, t_tok // DMA_UNROLL, issue, 0)

    @pl.when(i == 0)
    def _():
        gather(pos_ref, 0)

    if fused is None:
        @pl.when(i + 1 < nstep)
        def _():
            gather(nxt_ref, (i + 1) % 2)
    else:
        gather(nxt_ref, (i + 1) % 2, straight=True)
        fused()

    slot = i % 2
    pltpu.make_async_copy(y_ref.at[pl.ds(0, TOP_K * t_tok * nchunk)], buf_ref.at[slot],
                          sem.at[slot]).wait()
    gates = gate_ref[...]
    g2 = g2_ref[0]
    for c in range(nchunk):
        acc = None
        for k in range(TOP_K):
            rows = buf_ref[slot, pl.ds(k * t_tok * nchunk + c, t_tok, stride=nchunk), :]
            term = gates[:, k:k + 1] * rows
            acc = term if acc is None else acc + term
        sl = slice(c * LANES, (c + 1) * LANES)
        o_ref[:, sl] = x_ref[:, sl] + g2[:, sl] * acc


def _expert_tile(xs_ref, wgu_ref, bgu_ref, wd_ref, bd_ref, y_ref, ff, nchunk):
    tm = xs_ref.shape[0] // nchunk
    x = jnp.concatenate([xs_ref[pl.ds(c, tm, stride=nchunk), :] for c in range(nchunk)],
                        axis=1).astype(bf16)
    gu = _dot(x, wgu_ref[0]) + bgu_ref[0]
    g = jnp.minimum(gu[:, :ff], SWIGLU_LIMIT)
    u = jnp.clip(gu[:, ff:], -SWIGLU_LIMIT, SWIGLU_LIMIT)
    act = (u + 1.0) * (g * (1.0 / (1.0 + jnp.exp(-SWIGLU_ALPHA * g))))
    y = _dot(act.astype(bf16), wd_ref[0]) + bd_ref[0]
    for c in range(nchunk):
        y_ref[pl.ds(c, tm, stride=nchunk), :] = y[:, c * LANES:(c + 1) * LANES]


def _moe_stage_kernel(*refs, has_main, side, nside, t_tok, tm, nchunk, ff):
    it = iter(refs)
    take = lambda k: [next(it) for _ in range(k)]
    j = pl.program_id(0)
    if has_main:
        _, used_ref = take(2)
    if side == "dispatch":
        (zt_ref,) = take(1)
    if has_main:
        main_in = take(5)
    if side == "dispatch":
        side_in = take(2)
    elif side == "combine":
        side_in = take(6)
    if has_main:
        (y_ref,) = take(1)
    (side_out,) = take(1)
    scratch = list(it)

    def side_step(fused=None):
        if side == "dispatch":
            _dispatch_step(j, nside, zt_ref, *side_in, side_out, *scratch, t_tok, tm, nchunk,
                           fused=fused)
        else:
            _combine_step(j, nside, *side_in, side_out, *scratch, t_tok, nchunk, fused=fused)

    if not has_main:
        side_step()
        return

    tile = functools.partial(_expert_tile, *main_in, y_ref, ff, nchunk)
    used = used_ref[0]
    last_fused = nside if side == "dispatch" else nside - 1
    both = (j < last_fused) & (j < used)
    side_due = j < nside

    @pl.when(both)
    def _():
        side_step(fused=tile)

    @pl.when(side_due & jnp.logical_not(both))
    def _():
        side_step()

    @pl.when((j < used) & jnp.logical_not(both))
    def _():
        tile()

    @pl.when(j >= used)
    def _():
        y_ref[...] = jnp.zeros(y_ref.shape, f32)


def _moe_stage(main, side, kind, *, seq, tm, t_tok, nchunk):
    has_main = main is not None
    nside = side["pos3"].shape[0]
    prefetch, in_specs, operands, out_specs, out_shape, scratch = [], [], [], [], [], []
    smem = pltpu.SMEM
    off = side["off"]
    clamp = lambda j: jnp.minimum(j, nside - 1)
    ff2 = d = None
    if has_main:
        _, d, ff2 = main["wgu"].shape
        eoff = main["eoff"]
        prefetch += [main["te"], main["used"]]
    if kind == "dispatch":
        prefetch += [side["zt"]]
    npre = len(prefetch)

    def spec(shape, fn, **kw):
        return pl.BlockSpec(shape, lambda j, *pre: fn(j, pre), **kw)

    if has_main:
        in_specs += [spec((tm * nchunk, LANES), lambda j, p: (jnp.minimum(j, p[1][0] - 1), 0)),
                     spec((1, d, ff2), lambda j, p: (p[0][j] + eoff, 0, 0)),
                     spec((1, 1, ff2), lambda j, p: (p[0][j], 0, 0)),
                     spec((1, ff2 // 2, d), lambda j, p: (p[0][j] + eoff, 0, 0)),
                     spec((1, 1, d), lambda j, p: (p[0][j], 0, 0))]
        operands += [main["xs"], main["wgu"], main["bgu"], main["wd"], main["bd"]]
    tokens = t_tok * TOP_K
    if kind == "dispatch":
        in_specs += [spec((1, 1, tokens), lambda j, p: (clamp(j), 0, 0), memory_space=smem),
                     spec((t_tok * nchunk, LANES), lambda j, p: (clamp(j) + off, 0))]
        operands += [side["pos3"], side["hflat"]]
    else:
        n, dd = side["x2"].shape
        per = seq // t_tok
        in_specs += [spec((1, 1, tokens), lambda j, p: (clamp(j), 0, 0), memory_space=smem),
                     spec((1, 1, tokens), lambda j, p: (clamp(j + 1), 0, 0), memory_space=smem),
                     pl.BlockSpec(memory_space=pl.ANY),
                     spec((t_tok, LANES), lambda j, p: (clamp(j) + off, 0)),
                     spec((t_tok, dd), lambda j, p: (clamp(j) + off, 0)),
                     spec((1, 1, dd), lambda j, p: ((clamp(j) + off) // per, 0, 0))]
        operands += [side["pos3"], side["pos3"], side["ys"], side["gate"], side["x2"], side["g2"]]
        x_operand = npre + len(operands) - 2
    if has_main:
        out_specs += [spec((tm * nchunk, LANES), lambda j, p: (j, 0))]
        out_shape += [jax.ShapeDtypeStruct(main["xs"].shape, f32)]
    aliases = {}
    if kind == "dispatch":
        out_specs += [pl.BlockSpec(memory_space=pl.ANY)]
        out_shape += [jax.ShapeDtypeStruct((side["rows_pad"] * nchunk, LANES), f32)]
        scratch += [pltpu.VMEM((tm * nchunk, LANES), f32), pltpu.VMEM((2, t_tok * nchunk, LANES), f32),
                    pltpu.SemaphoreType.DMA((2,)), pltpu.SemaphoreType.DMA]
    else:
        out_specs += [spec((t_tok, dd), lambda j, p: (clamp(j) + off, 0))]
        out_shape += [jax.ShapeDtypeStruct((n, dd), f32)]
        scratch += [pltpu.VMEM((2, TOP_K * t_tok * nchunk, LANES), f32),
                    pltpu.SemaphoreType.DMA((2,))]
        aliases = {x_operand: len(out_shape) - 1}
    grid = (main["te"].shape[0],) if has_main else (nside,)
    outs = pl.pallas_call(
        functools.partial(_moe_stage_kernel, has_main=has_main, side=kind, nside=nside,
                          t_tok=t_tok, tm=tm, nchunk=nchunk, ff=None if ff2 is None else ff2 // 2),
        out_shape=tuple(out_shape),
        grid_spec=pltpu.PrefetchScalarGridSpec(
            num_scalar_prefetch=npre, grid=grid, in_specs=in_specs, out_specs=tuple(out_specs),
            scratch_shapes=scratch),
        input_output_aliases=aliases,
        compiler_params=_cparams(("arbitrary",)),
        name=("moe_experts_" if has_main else "moe_") + kind,
    )(*prefetch, *operands)
    return outs


def _moe_plan(idx, rank, cnt, ne, tm, ntile_max):
    counts = cnt[:ne].astype(jnp.int32)
    ntile_e = (counts + tm - 1) // tm
    tile_end = jnp.cumsum(ntile_e)
    tile_start = tile_end - ntile_e
    used = tile_end[-1]
    e_idx = idx[:, :TOP_K].astype(jnp.int32)
    onehot = e_idx[:, :, None] == jnp.arange(ne, dtype=jnp.int32)[None, None, :]
    start = jnp.sum(jnp.where(onehot, (tile_start * tm)[None, None, :], 0), axis=-1)
    pos = start + rank[:, :TOP_K].astype(jnp.int32)
    tiles = jnp.minimum(jnp.arange(ntile_max, dtype=jnp.int32), used - 1)
    tile_expert = jnp.sum(tile_end[None, :] <= tiles[:, None], axis=1).astype(jnp.int32)
    ztile = jnp.where(ntile_e > 0, tile_end - 1, -1)
    ztile = jnp.concatenate([ztile, used.reshape(1)]).astype(jnp.int32)
    return pos, tile_expert, used.reshape(1).astype(jnp.int32), ztile


def _lane_tile(v):
    return jnp.tile(v.astype(f32), LANES // v.shape[0]).reshape(1, LANES)


def kernel(x, c, positions, ada_w, ada_b, norm1_g, norm2_g, sb_w_qkv, sb_w_o, dsa_w_in,
           dsa_q_gain, dsa_k_gain, dsa_w_o, router_w, router_b, exp_w_gu, exp_b_gu,
           exp_w_down, exp_b_down):
    batch, seq, d = x.shape
    depth = ada_w.shape[0]
    n = batch * seq
    ne = router_w.shape[-1]
    ff = exp_w_down.shape[2]
    topk = min(TOPK_MAX, seq // 4)
    assert seq % KB == 0 and d % LANES == 0 and ne <= LANES

    mod = _modulation(c, ada_w, ada_b)
    x2 = x.reshape(n, d)

    ncols = dsa_w_in.shape[-1]
    ncols_pad = -(-ncols // LANES) * LANES
    half = ROPE_DIMS // 2
    inv = jnp.exp(-math.log(ROPE_THETA) * (2.0 * jnp.arange(half, dtype=f32) / ROPE_DIMS))
    l64 = jnp.arange(LANES) % HEAD_DIM
    inv_lane = jnp.where(l64 < ROPE_DIMS, inv[l64 % half], 0.0).reshape(1, LANES)
    bd = (jnp.arange(LANES)[:, None] // HEAD_DIM == jnp.arange(LANES)[None, :] // HEAD_DIM)
    bd = bd.astype(bf16)
    pos = positions.reshape(n, 1)

    nchunk = d // LANES
    half_tok = n // 2
    nside = half_tok // MOE_TOK
    ntile_half = (half_tok * TOP_K) // MOE_TM + ne
    assert (half_tok * TOP_K) % MOE_TM == 0 and half_tok % MOE_TOK == 0 and seq % MOE_TOK == 0
    assert half_tok % POST_TM == 0
    tri = (jnp.arange(POST_TM)[:, None] > jnp.arange(POST_TM)[None, :]).astype(bf16)
    wgu_all, wd_all = _expert_weight_prep(exp_w_gu.reshape(depth * ne, d, 2 * ff),
                                          exp_w_down.reshape(depth * ne, ff, d))

    for layer in range(depth):
        m6 = [mod[layer][:, k * d:(k + 1) * d].reshape(batch, 1, d) for k in range(6)]
        sh1, sc1, g1, sh2, sc2, g2 = m6
        n1 = norm1_g[layer].reshape(1, d)
        n2 = norm2_g[layer].reshape(1, d)
        j = layer // 2
        if layer % 2 == 0:
            qkv = _proj(x2, n1, sc1, sh1, sb_w_qkv[j].astype(bf16), seq)
            o = _stick_breaking(qkv, batch, seq, d)
            wo = sb_w_o[j].astype(bf16)
        else:
            w = dsa_w_in[j]
            kv0 = d
            qi0 = d + 2 * LANES
            ki0 = qi0 + IDX_HEADS * IDX_DIM
            w = jnp.concatenate([w[:, :kv0], w[:, qi0:ki0], w[:, kv0:qi0], w[:, ki0:]], axis=1)
            w = jnp.pad(w, ((0, 0), (0, ncols_pad - ncols))).astype(bf16)
            proj, wi = _dsa_proj(x2, n1, sc1, sh1, w, pos, inv_lane, _lane_tile(dsa_q_gain[j]),
                                 _lane_tile(dsa_k_gain[j]), bd, seq)
            o = _dsa_attention(proj, wi, batch, seq, d, topk)
            wo = dsa_w_o[j].astype(bf16)
        rw = jnp.pad(router_w[layer], ((0, 0), (0, LANES - ne)))
        rw_hi = rw.astype(bf16)
        rw = jnp.stack([rw_hi, (rw - rw_hi.astype(f32)).astype(bf16)])
        rb = jnp.pad(router_b[layer], (0, LANES - ne), constant_values=-jnp.inf).reshape(1, LANES)
        x2, hflat, idx, gate, rank, cnt = _post_attention(o, wo, x2, g1, n2, sc2, sh2, rw, rb,
                                                          tri, seq, POST_TM)
        bgu = exp_b_gu[layer]
        bgu = jnp.concatenate([bgu[..., 0::2], bgu[..., 1::2]], axis=-1).reshape(ne, 1, 2 * ff)
        halves = []
        for hh in range(2):
            tok = slice(hh * half_tok, (hh + 1) * half_tok)
            spos, te, used, zt = _moe_plan(idx[tok], rank[tok], cnt[hh, 0], ne, MOE_TM, ntile_half)
            halves.append(dict(
                pos3=spos.reshape(nside, 1, MOE_TOK * TOP_K), zt=zt, off=hh * nside,
                experts=dict(te=te, used=used, wgu=wgu_all, bgu=bgu, wd=wd_all,
                             bd=exp_b_down[layer].reshape(ne, 1, d), eoff=layer * ne)))
        stage = functools.partial(_moe_stage, seq=seq, tm=MOE_TM, t_tok=MOE_TOK, nchunk=nchunk)

        def disp(h):
            return dict(zt=h["zt"], pos3=h["pos3"], hflat=hflat, off=h["off"],
                        rows_pad=ntile_half * MOE_TM)

        def comb(h, ys, xin):
            return dict(pos3=h["pos3"], ys=ys, gate=gate, x2=xin, g2=g2, off=h["off"])

        ha, hb = halves
        (xs_a,) = stage(None, disp(ha), "dispatch")
        ys_a, xs_b = stage(dict(ha["experts"], xs=xs_a), disp(hb), "dispatch")
        ys_b, x2 = stage(dict(hb["experts"], xs=xs_b), comb(ha, ys_a, x2), "combine")
        (x2,) = stage(None, comb(hb, ys_b, x2), "combine")
    return x2.reshape(batch, seq, d)
```

```python
import functools
import math

import jax
import jax.numpy as jnp
from jax import lax
from jax.experimental import pallas as pl
from jax.experimental.pallas import tpu as pltpu

HEAD_DIM = 64
DSA_KV_HEADS = 2
IDX_HEADS = 8
IDX_DIM = 64
CHUNK_SHIFT = 6
TOPK_MAX = 256
TOP_K = 4
ROPE_THETA = 500000.0
ROPE_DIMS = HEAD_DIM // 4
SWIGLU_ALPHA = 1.702
SWIGLU_LIMIT = 7.0
EPS = 1e-6

LANES = 128
QB = 128
KB = 256
DQ = 256
NEG = -1e30
INT_MIN = -(2 ** 31)
KEY_NEG_INF = -2139095041
NORM_SLACK = 1.02
L_TINY = 1e-30
VMEM_LIMIT = 48 * 1024 * 1024
POST_TM = 512
MOE_TM = 512
MOE_TOK = 256
DMA_UNROLL = 8
SB_GROUP = 6
SB_WINDOW = 3
SB_CUTOFF = -110.0

f32 = jnp.float32
bf16 = jnp.bfloat16


def _cparams(sem):
    return pltpu.CompilerParams(dimension_semantics=sem, vmem_limit_bytes=VMEM_LIMIT)


def _dot(a, b):
    return jnp.dot(a, b, preferred_element_type=f32)


def _dot_nt(a, b):
    return lax.dot_general(a, b, (((1,), (1,)), ((), ())), preferred_element_type=f32)


def _dot_split(x, m01, passes):
    acc = None
    r = x
    for p in range(passes):
        t = r.astype(bf16)
        d = _dot(t, m01)
        acc = d if acc is None else acc + d
        if p + 1 < passes:
            r = r - t.astype(f32)
    return acc


def _norm_mod(x, g, sc, sh):
    ms = jnp.mean(x * x, axis=-1, keepdims=True)
    return (x * lax.rsqrt(ms + EPS) * g) * (1.0 + sc) + sh


def _mod_kernel(c_ref, w_ref, b_ref, o_ref):
    c = c_ref[...]
    cs = c * (1.0 / (1.0 + jnp.exp(-c)))
    o_ref[0] = jnp.dot(cs, w_ref[0], preferred_element_type=f32,
                       precision=lax.Precision.HIGHEST) + b_ref[0]


def _modulation(c, ada_w, ada_b):
    depth, d, n6 = ada_w.shape
    b = c.shape[0]
    tn = 1024
    return pl.pallas_call(
        _mod_kernel,
        out_shape=jax.ShapeDtypeStruct((depth, b, n6), f32),
        grid=(depth, n6 // tn),
        in_specs=[pl.BlockSpec((b, d), lambda l, j: (0, 0)),
                  pl.BlockSpec((1, d, tn), lambda l, j: (l, 0, j)),
                  pl.BlockSpec((1, 1, tn), lambda l, j: (l, 0, j))],
        out_specs=pl.BlockSpec((1, b, tn), lambda l, j: (l, 0, j)),
        compiler_params=_cparams(("arbitrary", "arbitrary")),
        name="adaln_mod",
    )(c, ada_w, ada_b.reshape(depth, 1, n6))


def _proj_kernel(x_ref, g_ref, sc_ref, sh_ref, w_ref, o_ref):
    h = _norm_mod(x_ref[...], g_ref[...], sc_ref[0], sh_ref[0])
    o_ref[...] = _dot(h.astype(bf16), w_ref[...]).astype(o_ref.dtype)


def _proj(x2, g, sc, sh, w, seq, tm=512):
    n, d = x2.shape
    nc = w.shape[1]
    per = seq // tm
    return pl.pallas_call(
        _proj_kernel,
        out_shape=jax.ShapeDtypeStruct((n, nc), bf16),
        grid=(n // tm,),
        in_specs=[pl.BlockSpec((tm, d), lambda i: (i, 0)),
                  pl.BlockSpec((1, d), lambda i: (0, 0)),
                  pl.BlockSpec((1, 1, d), lambda i: (i // per, 0, 0)),
                  pl.BlockSpec((1, 1, d), lambda i: (i // per, 0, 0)),
                  pl.BlockSpec((d, nc), lambda i: (0, 0))],
        out_specs=pl.BlockSpec((tm, nc), lambda i: (i, 0)),
        compiler_params=_cparams(("arbitrary",)),
        name="sb_qkv_proj",
    )(x2, g, sc, sh, w)


def _sb_kernel(q_ref, k_ref, v_ref, o_ref, k0_ref, k1_ref):
    seq = q_ref.shape[0]
    nqb = seq // QB
    lane = lax.broadcasted_iota(jnp.int32, (1, LANES), 1)
    kk = k_ref[...]
    zero = jnp.zeros_like(kk)
    k0_ref[...] = jnp.where(lane < HEAD_DIM, kk, zero)
    k1_ref[...] = jnp.where(lane >= HEAD_DIM, kk, zero)
    row = lax.broadcasted_iota(jnp.int32, (QB, QB), 0)
    col = lax.broadcasted_iota(jnp.int32, (QB, QB), 1)
    before = col < row
    tri = jnp.where(row > col, 1.0, 0.0).astype(bf16)

    def windows(jobs, nb, diag):
        zs = [_dot_nt(q, kh_ref[pl.ds(k0, nb * QB), :]) for q, kh_ref, k0, _ in jobs]
        lms, lss = [], []
        for z in zs:
            l1p = jnp.log(1.0 + jnp.exp(-jnp.abs(z)))
            lm = jnp.minimum(-z, 0.0) - l1p
            lss.append(lm + z)
            parts = [lm[:, s * QB:(s + 1) * QB] for s in range(nb)]
            if diag:
                parts[-1] = jnp.where(before, parts[-1], 0.0)
            lms.append(parts)
        cums = [[_dot_split(p, tri, 2) for p in parts] for parts in lms]
        outs = []
        for (q, kh_ref, k0, c_in), parts, cum, ls in zip(jobs, lms, cums, lss):
            c = c_in
            probs = [None] * nb
            for s in reversed(range(nb)):
                a = jnp.exp(ls[:, s * QB:(s + 1) * QB] + (cum[s] + c))
                if diag and s == nb - 1:
                    a = jnp.where(before, a, 0.0)
                probs[s] = a.astype(bf16)
                c = c + jnp.sum(parts[s], axis=1, keepdims=True)
            outs.append((c, probs[0] if nb == 1 else jnp.concatenate(probs, axis=1)))
        return [(c, _dot(a_all, v_ref[pl.ds(k0, nb * QB), :]))
                for (c, a_all), (_, _, k0, _) in zip(outs, jobs)]

    def qheads(blocks, nb):
        jobs, meta = [], []
        zc = jnp.zeros((QB, 1), f32)
        for i in blocks:
            q0 = i * QB
            k0 = q0 - (nb - 1) * QB
            if not isinstance(i, int):
                q0 = pl.multiple_of(q0, QB)
                k0 = pl.multiple_of(k0, QB)
            q = q_ref[pl.ds(q0, QB), :] * (HEAD_DIM ** -0.5)
            jobs += [(q, k0_ref, k0, zc), (q, k1_ref, k0, zc)]
            meta.append((q0, q))
        res = windows(jobs, nb, True)
        return [(q0, q) + res[2 * n] + res[2 * n + 1] for n, (q0, q) in enumerate(meta)]

    def qhead(i, nb):
        return qheads([i], nb)[0]

    def qtail(i, nb, st):
        q0, q, c0, a0, c1, a1 = st
        if nb <= SB_WINDOW and not (isinstance(i, int) and i < nb):
            def cond(s):
                jb, c0, _, c1, _ = s
                live = jnp.max(jnp.maximum(c0, c1)) > SB_CUTOFF
                return jnp.logical_and(jb >= 0, live)

            def body(s):
                jb, c0, a0, c1, a1 = s
                kb = pl.multiple_of(jb * QB, QB)
                (c0, d0), (c1, d1) = windows([(q, k0_ref, kb, c0), (q, k1_ref, kb, c1)], 1, False)
                return jb - 1, c0, a0 + d0, c1, a1 + d1

            jb0 = jnp.asarray(i - nb, jnp.int32)
            _, c0, a0, c1, a1 = lax.while_loop(cond, body, (jb0, c0, a0, c1, a1))
        o_ref[pl.ds(q0, QB), :] = jnp.where(lane < HEAD_DIM, a0, a1).astype(o_ref.dtype)

    first = min(SB_WINDOW, nqb)
    npairs = (nqb - first) // SB_GROUP
    singles = list(range(first + SB_GROUP * npairs, nqb))
    for i in range(first):
        if i + 1 == SB_WINDOW and singles:
            continue
        qtail(i, i + 1, qhead(i, i + 1))

    def pair(j, _):
        i = first + SB_GROUP * j
        blocks = [i + n for n in range(SB_GROUP)]
        for b, st in zip(blocks, qheads(blocks, SB_WINDOW)):
            qtail(b, SB_WINDOW, st)
        return 0

    lax.fori_loop(0, npairs, pair, 0)
    if singles:
        group = ([first - 1] if first == SB_WINDOW else []) + singles
        for i, st in zip(group, qheads(group, SB_WINDOW)):
            qtail(i, SB_WINDOW, st)


def _stick_breaking(qkv, batch, seq, d):
    npair = d // LANES
    return pl.pallas_call(
        _sb_kernel,
        out_shape=jax.ShapeDtypeStruct((batch * seq, d), bf16),
        grid=(batch, npair),
        in_specs=[pl.BlockSpec((seq, LANES), lambda b, p: (b, p)),
                  pl.BlockSpec((seq, LANES), lambda b, p: (b, npair + p)),
                  pl.BlockSpec((seq, LANES), lambda b, p: (b, 2 * npair + p))],
        out_specs=pl.BlockSpec((seq, LANES), lambda b, p: (b, p)),
        scratch_shapes=[pltpu.VMEM((seq, LANES), bf16), pltpu.VMEM((seq, LANES), bf16)],
        compiler_params=_cparams(("arbitrary", "arbitrary")),
        name="stick_breaking_attn",
    )(qkv, qkv, qkv)


def _dsa_proj_kernel(x_ref, g_ref, sc_ref, sh_ref, w_ref, pos_ref, inv_ref, qg_ref, kg_ref,
                     bd_ref, o_ref, wi_ref, *, d):
    h = _norm_mod(x_ref[...], g_ref[...], sc_ref[0], sh_ref[0])
    p = _dot(h.astype(bf16), w_ref[...])
    lane = lax.broadcasted_iota(jnp.int32, (1, LANES), 1)
    ang = pos_ref[...].astype(f32) * inv_ref[...]
    cos_t = jnp.cos(ang)
    sin_t = jnp.sin(ang)
    upper = (lane % ROPE_DIMS) >= (ROPE_DIMS // 2)
    s_up = jnp.where(upper, sin_t, 0.0)
    s_lo = jnp.where(upper, 0.0, -sin_t)
    half = ROPE_DIMS // 2

    def rope(y):
        return y * cos_t + pltpu.roll(y, half, 1) * s_up + pltpu.roll(y, LANES - half, 1) * s_lo

    def headnorm(y, gain):
        ms = _dot_split(y * y, bd_ref[...], 2) * (1.0 / HEAD_DIM)
        return y * lax.rsqrt(ms + EPS) * gain

    nq = d // LANES
    for c in range(nq):
        y = p[:, c * LANES:(c + 1) * LANES]
        o_ref[:, c * LANES:(c + 1) * LANES] = rope(headnorm(y, qg_ref[...])).astype(bf16)
    c0 = nq
    for c in range(c0, c0 + IDX_HEADS * IDX_DIM // LANES):
        o_ref[:, c * LANES:(c + 1) * LANES] = rope(p[:, c * LANES:(c + 1) * LANES]).astype(bf16)
    c0 += IDX_HEADS * IDX_DIM // LANES
    y = p[:, c0 * LANES:(c0 + 1) * LANES]
    o_ref[:, c0 * LANES:(c0 + 1) * LANES] = rope(headnorm(y, kg_ref[...])).astype(bf16)
    c0 += 1
    o_ref[:, c0 * LANES:(c0 + 1) * LANES] = p[:, c0 * LANES:(c0 + 1) * LANES].astype(bf16)
    c0 += 1
    y = p[:, c0 * LANES:(c0 + 1) * LANES]
    o_ref[:, c0 * LANES:(c0 + 1) * LANES] = jnp.where(lane < IDX_DIM, rope(y), 0.0).astype(bf16)
    wi = pltpu.roll(y, LANES - IDX_DIM, 1) * (IDX_HEADS ** -0.5) * (IDX_DIM ** -0.5)
    wi_ref[...] = jnp.where(lane < IDX_HEADS, wi, 0.0)


def _dsa_proj(x2, g, sc, sh, w, pos, inv_lane, qg, kg, bd, seq, tm=512):
    n, d = x2.shape
    nc = w.shape[1]
    per = seq // tm
    return pl.pallas_call(
        functools.partial(_dsa_proj_kernel, d=d),
        out_shape=(jax.ShapeDtypeStruct((n, nc), bf16), jax.ShapeDtypeStruct((n, LANES), f32)),
        grid=(n // tm,),
        in_specs=[pl.BlockSpec((tm, d), lambda i: (i, 0)),
                  pl.BlockSpec((1, d), lambda i: (0, 0)),
                  pl.BlockSpec((1, 1, d), lambda i: (i // per, 0, 0)),
                  pl.BlockSpec((1, 1, d), lambda i: (i // per, 0, 0)),
                  pl.BlockSpec((d, nc), lambda i: (0, 0)),
                  pl.BlockSpec((tm, 1), lambda i: (i, 0)),
                  pl.BlockSpec((1, LANES), lambda i: (0, 0)),
                  pl.BlockSpec((1, LANES), lambda i: (0, 0)),
                  pl.BlockSpec((1, LANES), lambda i: (0, 0)),
                  pl.BlockSpec((LANES, LANES), lambda i: (0, 0))],
        out_specs=(pl.BlockSpec((tm, nc), lambda i: (i, 0)),
                   pl.BlockSpec((tm, LANES), lambda i: (i, 0))),
        compiler_params=_cparams(("arbitrary",)),
        name="dsa_in_proj",
    )(x2, g, sc, sh, w, pos, inv_lane, qg, kg, bd)


def _dsa_kernel(q_ref, k_ref, v_ref, qi_ref, ki_ref, wi_ref, o_ref,
                kk_ref, vv_ref, ki2_ref, keys_ref, bias_ref, m_ref, acc_ref, thr_ref, need_ref,
                qs_ref, kmax_ref, *, topk):
    seq = k_ref.shape[0]
    nchunk = q_ref.shape[1] // LANES
    group = (2 * nchunk) // DSA_KV_HEADS
    i = pl.program_id(1)
    lane = lax.broadcasted_iota(jnp.int32, (1, LANES), 1)
    lo = lane < HEAD_DIM

    @pl.when(i == 0)
    def _():
        k = k_ref[...]
        v = v_ref[...]
        ki = ki_ref[...]
        kr = pltpu.roll(k.astype(f32), HEAD_DIM, 1).astype(bf16)
        vr = pltpu.roll(v.astype(f32), HEAD_DIM, 1).astype(bf16)
        zero = jnp.zeros_like(k)
        kk_ref[0] = jnp.where(lo, k, zero)
        kk_ref[1] = jnp.where(lo, zero, kr)
        kk_ref[2] = jnp.where(lo, kr, zero)
        kk_ref[3] = jnp.where(lo, zero, k)
        lane_full = lax.broadcasted_iota(jnp.int32, k.shape, 1)
        oh = [jnp.where(lane_full < HEAD_DIM, 1.0, 0.0).astype(bf16),
              jnp.where(lane_full < HEAD_DIM, 0.0, 1.0).astype(bf16)]
        vv_ref[0] = jnp.concatenate([jnp.where(lo, v, zero), oh[0]], axis=1)
        vv_ref[1] = jnp.concatenate([jnp.where(lo, zero, vr), oh[1]], axis=1)
        vv_ref[2] = jnp.concatenate([jnp.where(lo, vr, zero), oh[0]], axis=1)
        vv_ref[3] = jnp.concatenate([jnp.where(lo, zero, v), oh[1]], axis=1)
        ki2_ref[0] = ki
        ki2_ref[1] = pltpu.roll(ki.astype(f32), HEAD_DIM, 1).astype(bf16)
        kf = k.astype(f32)
        rr = lax.broadcasted_iota(jnp.int32, (LANES, LANES), 0) < HEAD_DIM
        cc = lax.broadcasted_iota(jnp.int32, (LANES, LANES), 1) < HEAD_DIM
        bd = jnp.where(rr == cc, 1.0, 0.0).astype(bf16)
        n2 = jnp.max(_dot((kf * kf).astype(bf16), bd), axis=0, keepdims=True)
        for g in range(DSA_KV_HEADS):
            sel = lo if g == 0 else jnp.logical_not(lo)
            top = jnp.max(jnp.where(sel, n2, 0.0), axis=1, keepdims=True)
            kmax_ref[g] = jnp.broadcast_to(top, (1, LANES))

    nkb = i + 1
    qrow = lax.broadcasted_iota(jnp.int32, (DQ, KB), 0) + i * DQ
    kcol = lax.broadcasted_iota(jnp.int32, (DQ, KB), 1)

    wi = wi_ref[...]
    wcols = [wi[:, hd:hd + 1] for hd in range(IDX_HEADS)]

    def score_block(jb, _):
        k0 = pl.multiple_of(jb * KB, KB)
        score = jnp.zeros((DQ, KB), f32)
        for hd in range(IDX_HEADS):
            qc = qi_ref[:, (hd // 2) * LANES:(hd // 2 + 1) * LANES]
            kb = ki2_ref[hd % 2, pl.ds(k0, KB), :]
            score = score + jnp.maximum(_dot_nt(qc, kb), 0.0) * wcols[hd]
        score = score + 0.0
        bits = pltpu.bitcast(score, jnp.int32)
        key = bits ^ ((bits >> 31) & 0x7FFFFFFF)
        adm = ((kcol + k0) >> CHUNK_SHIFT) <= (qrow >> CHUNK_SHIFT)
        keys_ref[jb] = jnp.where(adm, key, INT_MIN)
        return 0

    lax.fori_loop(0, nkb, score_block, 0)

    kf = float(topk)
    ones_l = jnp.ones((LANES, LANES), bf16)

    def search(nk):
        def count(r, pred):
            part = None
            for jb in range(nk):
                for hl in range(KB // LANES):
                    key = keys_ref[jb, r * QB:(r + 1) * QB, hl * LANES:(hl + 1) * LANES]
                    hit = jnp.where(pred(key), 1.0, 0.0)
                    part = hit if part is None else part + hit
            return _dot(part.astype(bf16), ones_l)

        groups = range(DQ // QB)
        t0 = tuple(jnp.where(count(r, lambda key: key >= 0) >= kf, 0, INT_MIN).astype(jnp.int32)
                   for r in groups)

        def bit_step(s, ts):
            bit = jnp.left_shift(jnp.int32(1), 30 - s)
            out = []
            for r in groups:
                cand = ts[r] + bit
                enough = count(r, lambda key, cand=cand: key >= cand) >= kf
                out.append(jnp.where(enough, cand, ts[r]))
            return tuple(out)

        ts = lax.fori_loop(0, 31, bit_step, t0)
        for r in groups:
            thr_ref[r * QB:(r + 1) * QB, :] = ts[r]
            need_ref[r * QB:(r + 1) * QB, :] = kf - count(r, lambda key, t=ts[r]: key > t)

    for nk in range(1, seq // KB + 1):
        @pl.when(nkb == nk)
        def _(nk=nk):
            search(nk)

    thr = jnp.concatenate([thr_ref[...]] * (KB // LANES), axis=1)
    need = jnp.concatenate([need_ref[...]] * (KB // LANES), axis=1)

    r2 = lax.broadcasted_iota(jnp.int32, (KB, KB), 0)
    c2 = lax.broadcasted_iota(jnp.int32, (KB, KB), 1)
    tri = jnp.where(r2 < c2, 1.0, 0.0).astype(bf16)
    ones_k = jnp.ones((KB, KB), bf16)

    def bias_block(jb, carry):
        key = keys_ref[jb]
        eq = key == thr
        eqb = jnp.where(eq, 1.0, 0.0).astype(bf16)
        rank = _dot(eqb, tri) + carry
        sel = (key > thr) | (eq & (rank < need))
        sel = sel & (key > KEY_NEG_INF)
        bias_ref[jb] = jnp.where(sel, 0.0, NEG)
        return carry + _dot(eqb, ones_k)

    lax.fori_loop(0, nkb, bias_block, jnp.zeros((DQ, KB), f32))

    cpg = group // 2
    for g in range(DSA_KV_HEADS):
        qs_ref[g] = jnp.concatenate(
            [q_ref[:, c * LANES:(c + 1) * LANES] for c in range(g * cpg, (g + 1) * cpg)],
            axis=0) * (HEAD_DIM ** -0.5)

    def scores(g, hf, k0, bias):
        return _dot_nt(qs_ref[g], kk_ref[2 * g + hf, pl.ds(k0, KB), :]) + bias

    def sweep_max():
        m_ref[...] = jnp.full(m_ref.shape, NEG, f32)

        def max_block(jb, _):
            k0 = pl.multiple_of(jb * KB, KB)
            bias = jnp.concatenate([bias_ref[jb]] * cpg, axis=0)
            for g in range(DSA_KV_HEADS):
                for hf in range(2):
                    s = scores(g, hf, k0, bias)
                    fold = jnp.maximum(s[:, :LANES], s[:, LANES:])
                    m_ref[2 * g + hf] = jnp.maximum(m_ref[2 * g + hf], fold)
            return 0

        lax.fori_loop(0, nkb, max_block, 0)
        for h in range(2 * DSA_KV_HEADS):
            m = jnp.max(m_ref[h], axis=1, keepdims=True)
            m_ref[h] = jnp.broadcast_to(m, (cpg * DQ, LANES))

    def sweep_exp():
        acc_ref[...] = jnp.zeros(acc_ref.shape, f32)

        def attn_block(jb, _):
            k0 = pl.multiple_of(jb * KB, KB)
            bias = jnp.concatenate([bias_ref[jb]] * cpg, axis=0)
            for g in range(DSA_KV_HEADS):
                ps = []
                for hf in range(2):
                    m = m_ref[2 * g + hf]
                    s = scores(g, hf, k0, bias)
                    ps.append(jnp.exp(s - jnp.concatenate([m, m], axis=1)).astype(bf16))
                p2 = jnp.concatenate(ps, axis=1)
                v2 = jnp.concatenate([vv_ref[2 * g, pl.ds(k0, KB), :],
                                      vv_ref[2 * g + 1, pl.ds(k0, KB), :]], axis=0)
                acc_ref[g] += _dot(p2, v2)
            return 0

        lax.fori_loop(0, nkb, attn_block, 0)

    def write_out():
        for c in range(nchunk):
            r = acc_ref[c // cpg, (c % cpg) * DQ:(c % cpg + 1) * DQ, :]
            o_ref[:, c * LANES:(c + 1) * LANES] = (r[:, :LANES] / r[:, LANES:]).astype(o_ref.dtype)

    for g in range(DSA_KV_HEADS):
        q = qs_ref[g].astype(f32)
        q2 = (q * q).astype(bf16)
        for hf in range(2):
            half = jnp.where((lax.broadcasted_iota(jnp.int32, (LANES, LANES), 0) < HEAD_DIM)
                             == (hf == 0), 1.0, 0.0).astype(bf16)
            qn2 = _dot(q2, half)
            m_ref[2 * g + hf] = NORM_SLACK * jnp.sqrt(qn2 * kmax_ref[g])
    sweep_exp()
    lmin = jnp.min(acc_ref[:, :, LANES:])

    @pl.when(lmin > L_TINY)
    def _():
        write_out()

    @pl.when(jnp.logical_not(lmin > L_TINY))
    def _():
        sweep_max()
        sweep_exp()
        write_out()


def group_even(d):
    return ((d // HEAD_DIM) // DSA_KV_HEADS) % 2 == 0


def _dsa_attention(proj, wi, batch, seq, d, topk):
    nqb = seq // DQ
    nchunk = d // LANES
    cpg = nchunk // DSA_KV_HEADS
    qiw = IDX_HEADS * IDX_DIM
    assert d % qiw == 0 and group_even(d) and DQ == KB
    qicol = d // qiw
    kcol = nchunk + qiw // LANES
    vcol = kcol + 1
    kicol = kcol + 2
    return pl.pallas_call(
        functools.partial(_dsa_kernel, topk=topk),
        out_shape=jax.ShapeDtypeStruct((batch * seq, d), bf16),
        grid=(batch, nqb),
        in_specs=[pl.BlockSpec((DQ, d), lambda b, i: (b * nqb + i, 0)),
                  pl.BlockSpec((seq, LANES), lambda b, i: (b, kcol)),
                  pl.BlockSpec((seq, LANES), lambda b, i: (b, vcol)),
                  pl.BlockSpec((DQ, IDX_HEADS * IDX_DIM), lambda b, i: (b * nqb + i, qicol)),
                  pl.BlockSpec((seq, LANES), lambda b, i: (b, kicol)),
                  pl.BlockSpec((DQ, LANES), lambda b, i: (b * nqb + i, 0))],
        out_specs=pl.BlockSpec((DQ, d), lambda b, i: (b * nqb + i, 0)),
        scratch_shapes=[pltpu.VMEM((4, seq, LANES), bf16),
                        pltpu.VMEM((4, seq, 2 * LANES), bf16),
                        pltpu.VMEM((2, seq, LANES), bf16),
                        pltpu.VMEM((seq // KB, DQ, KB), jnp.int32),
                        pltpu.VMEM((seq // KB, DQ, KB), f32),
                        pltpu.VMEM((2 * DSA_KV_HEADS, cpg * DQ, LANES), f32),
                        pltpu.VMEM((DSA_KV_HEADS, cpg * DQ, 2 * LANES), f32),
                        pltpu.VMEM((DQ, LANES), jnp.int32),
                        pltpu.VMEM((DQ, LANES), f32),
                        pltpu.VMEM((DSA_KV_HEADS, cpg * DQ, LANES), bf16),
                        pltpu.VMEM((DSA_KV_HEADS, 1, LANES), f32)],
        compiler_params=_cparams(("arbitrary", "arbitrary")),
        name="dsa_sparse_attn",
    )(proj, proj, proj, proj, proj, wi)


def _post_kernel(o_ref, wo_ref, x_ref, g1_ref, n2_ref, sc_ref, sh_ref, rw_ref, rb_ref, tri_ref,
                 xo_ref, hf_ref, idx_ref, gate_ref, rank_ref, cnt_ref, run_ref):
    half = pl.num_programs(0) // 2

    @pl.when((pl.program_id(0) == 0) | (pl.program_id(0) == half))
    def _():
        run_ref[...] = jnp.zeros(run_ref.shape, f32)

    y = _dot(o_ref[...], wo_ref[...])
    x = x_ref[...] + g1_ref[0] * y
    xo_ref[...] = x
    h = _norm_mod(x, n2_ref[...], sc_ref[0], sh_ref[0])
    tm, d = h.shape
    nchunk = d // LANES
    for c in range(nchunk):
        hf_ref[pl.ds(c, tm, stride=nchunk), :] = h[:, c * LANES:(c + 1) * LANES]
    h_hi = h.astype(bf16)
    h_lo = (h - h_hi.astype(f32)).astype(bf16)
    logits = (_dot(h_hi, rw_ref[0]) + _dot(h_hi, rw_ref[1]) + _dot(h_lo, rw_ref[0])) + rb_ref[...]
    lane = lax.broadcasted_iota(jnp.int32, logits.shape, 1).astype(f32)
    work = logits
    vals, idxs, hits = [], [], []
    for _ in range(TOP_K):
        m = jnp.max(work, axis=1, keepdims=True)
        idx = jnp.min(jnp.where(work == m, lane, float(LANES)), axis=1, keepdims=True)
        hit = lane == idx
        vals.append(m)
        idxs.append(idx)
        hits.append(hit)
        work = jnp.where(hit, -jnp.inf, work)
    es = [jnp.exp(v - vals[0]) for v in vals]
    inv = 1.0 / (es[0] + es[1] + es[2] + es[3])
    multi = jnp.zeros_like(logits)
    for hit in hits:
        multi = multi + jnp.where(hit, 1.0, 0.0)
    before = _dot(tri_ref[...], multi.astype(bf16)) + run_ref[...]
    idx_o = jnp.zeros_like(logits)
    gate_o = jnp.zeros_like(logits)
    rank_o = jnp.zeros_like(logits)
    for k in range(TOP_K):
        slot = lane == float(k)
        rank = jnp.sum(jnp.where(hits[k], before, 0.0), axis=1, keepdims=True)
        idx_o = jnp.where(slot, idxs[k], idx_o)
        gate_o = jnp.where(slot, es[k] * inv, gate_o)
        rank_o = jnp.where(slot, rank, rank_o)
    idx_ref[...] = idx_o
    gate_ref[...] = gate_o
    rank_ref[...] = rank_o
    run_ref[...] += jnp.sum(multi, axis=0, keepdims=True)
    cnt_ref[0] = jnp.broadcast_to(run_ref[...], cnt_ref.shape[1:])


def _post_attention(o, wo, x2, g1, n2, sc2, sh2, rw, rb, tri, seq, tm):
    n, d = x2.shape
    per = seq // tm
    nchunk = d // LANES
    row = lambda i: (i, 0)
    fix = lambda i: (0, 0)
    bat = lambda i: (i // per, 0, 0)
    lane_out = jax.ShapeDtypeStruct((n, LANES), f32)
    return pl.pallas_call(
        _post_kernel,
        out_shape=(jax.ShapeDtypeStruct((n, d), f32), jax.ShapeDtypeStruct((n * nchunk, LANES), f32),
                   lane_out, lane_out, lane_out, jax.ShapeDtypeStruct((2, 8, LANES), f32)),
        grid=(n // tm,),
        in_specs=[pl.BlockSpec((tm, d), row), pl.BlockSpec((d, d), fix),
                  pl.BlockSpec((tm, d), row), pl.BlockSpec((1, 1, d), bat),
                  pl.BlockSpec((1, d), fix), pl.BlockSpec((1, 1, d), bat),
                  pl.BlockSpec((1, 1, d), bat), pl.BlockSpec((2, d, LANES), lambda i: (0, 0, 0)),
                  pl.BlockSpec((1, LANES), fix), pl.BlockSpec((tm, tm), fix)],
        out_specs=(pl.BlockSpec((tm, d), row), pl.BlockSpec((tm * nchunk, LANES), row),
                   pl.BlockSpec((tm, LANES), row), pl.BlockSpec((tm, LANES), row),
                   pl.BlockSpec((tm, LANES), row),
                   pl.BlockSpec((1, 8, LANES), lambda i: (i // (n // tm // 2), 0, 0))),
        scratch_shapes=[pltpu.VMEM((1, LANES), f32)],
        compiler_params=_cparams(("arbitrary",)),
        name="attn_out_norm_router",
    )(o, wo, x2, g1, n2, sc2, sh2, rw, rb, tri)


def _deint_kernel(w_ref, wd_ref, p_ref, o_ref, od_ref):
    ff = o_ref.shape[2] // 2
    wide = 2 * LANES
    for b in range(o_ref.shape[2] // wide):
        x = w_ref[0, :, b * wide:(b + 1) * wide].astype(bf16)
        r = _dot(x, p_ref[...]).astype(bf16)
        o_ref[0, :, b * LANES:(b + 1) * LANES] = r[:, :LANES]
        o_ref[0, :, ff + b * LANES:ff + (b + 1) * LANES] = r[:, LANES:]
    od_ref[...] = wd_ref[...].astype(bf16)


def _expert_weight_prep(w, wd):
    ne, d, ff2 = w.shape
    wide = 2 * LANES
    src = jnp.arange(wide)
    perm = jnp.where(src < LANES, 2 * src, 2 * (src - LANES) + 1)
    p = (jnp.arange(wide)[:, None] == perm[None, :]).astype(bf16)
    blk = lambda e: (e, 0, 0)
    return pl.pallas_call(
        _deint_kernel,
        out_shape=(jax.ShapeDtypeStruct((ne, d, ff2), bf16),
                   jax.ShapeDtypeStruct(wd.shape, bf16)),
        grid=(ne,),
        in_specs=[pl.BlockSpec((1, d, ff2), blk), pl.BlockSpec((1,) + wd.shape[1:], blk),
                  pl.BlockSpec((wide, wide), lambda e: (0, 0))],
        out_specs=(pl.BlockSpec((1, d, ff2), blk), pl.BlockSpec((1,) + wd.shape[1:], blk)),
        compiler_params=_cparams(("arbitrary",)),
        name="expert_weight_prep",
    )(w, wd, p)


def _dispatch_step(i, nstep, zt_ref, pos_ref, hf_ref, xs_ref, zbuf_ref, ring_ref, sem, zsem,
                   t_tok, tm, nchunk, fused=None):
    ne = zt_ref.shape[0] - 1
    ntile = xs_ref.shape[0] // (tm * nchunk)

    def zero_tile(e):
        return zt_ref[e] if e < ne else zt_ref[ne] + (e - ne)

    def zero_copy(e):
        start = pl.multiple_of(zero_tile(e) * (tm * nchunk), tm * nchunk)
        return pltpu.make_async_copy(zbuf_ref, xs_ref.at[pl.ds(start, tm * nchunk)], zsem)

    def zero_wanted(e):
        return zero_tile(e) >= 0 if e < ne else zero_tile(e) < ntile

    @pl.when(i == 0)
    def _():
        zbuf_ref[...] = jnp.zeros(zbuf_ref.shape, f32)
        for e in range(2 * ne):
            @pl.when(zero_wanted(e))
            def _():
                zero_copy(e).start()
        for e in range(2 * ne):
            @pl.when(zero_wanted(e))
            def _():
                zero_copy(e).wait()

    slot = i % 2
    ring_ref[slot] = hf_ref[...]

    def issue(tt, _):
        for u in range(DMA_UNROLL):
            t = tt * DMA_UNROLL + u
            if isinstance(t, int):
                src = ring_ref.at[slot, pl.ds(t * nchunk, nchunk)]
            else:
                src = ring_ref.at[slot, pl.ds(pl.multiple_of(t * nchunk, nchunk), nchunk)]
            for k in range(TOP_K):
                p = pos_ref[0, 0, t * TOP_K + k]
                dst = xs_ref.at[pl.ds(pl.multiple_of(p * nchunk, nchunk), nchunk)]
                pltpu.make_async_copy(src, dst, sem.at[slot]).start(
                    priority=k % 2 if fused is None else 0)
        return 0

    if fused is None:
        lax.fori_loop(0, t_tok // DMA_UNROLL, issue, 0)
    else:
        for tt in range(t_tok // DMA_UNROLL):
            issue(tt, 0)
        fused()

    def drain(s):
        for k in range(TOP_K):
            pltpu.make_async_copy(ring_ref.at[s], xs_ref.at[pl.ds(0, t_tok * nchunk)],
                                  sem.at[s]).wait()

    @pl.when(i > 0)
    def _():
        drain(1 - slot)

    @pl.when(i == nstep - 1)
    def _():
        drain(slot)


def _combine_step(i, nstep, pos_ref, nxt_ref, y_ref, gate_ref, x_ref, g2_ref, o_ref, buf_ref, sem,
                  t_tok, nchunk, fused=None):
    def gather(p_ref, slot, straight=False):
        def issue(tt, _):
            for u in range(DMA_UNROLL):
                t = tt * DMA_UNROLL + u
                for k in range(TOP_K):
                    p = p_ref[0, 0, t * TOP_K + k]
                    src = y_ref.at[pl.ds(pl.multiple_of(p * nchunk, nchunk), nchunk)]
                    row = (k * t_tok + t) * nchunk
                    if not isinstance(row, int):
                        row = pl.multiple_of(row, nchunk)
                    pltpu.make_async_copy(src, buf_ref.at[slot, pl.ds(row, nchunk)],
                                          sem.at[slot]).start(priority=0 if straight else k % 2)
            return 0

        if straight:
            for tt in range(t_tok // DMA_UNROLL):
                issue(tt, 0)
        else:
            lax.fori_loop(0, t_tok // DMA_UNROLL, issue, 0)

    @pl.when(i == 0)
    def _():
        gather(pos_ref, 0)

    if fused is None:
        @pl.when(i + 1 < nstep)
        def _():
            gather(nxt_ref, (i + 1) % 2)
    else:
        gather(nxt_ref, (i + 1) % 2, straight=True)
        fused()

    slot = i % 2
    pltpu.make_async_copy(y_ref.at[pl.ds(0, TOP_K * t_tok * nchunk)], buf_ref.at[slot],
                          sem.at[slot]).wait()
    gates = gate_ref[...]
    g2 = g2_ref[0]
    for c in range(nchunk):
        acc = None
        for k in range(TOP_K):
            rows = buf_ref[slot, pl.ds(k * t_tok * nchunk + c, t_tok, stride=nchunk), :]
            term = gates[:, k:k + 1] * rows
            acc = term if acc is None else acc + term
        sl = slice(c * LANES, (c + 1) * LANES)
        o_ref[:, sl] = x_ref[:, sl] + g2[:, sl] * acc


def _expert_tile(xs_ref, wgu_ref, bgu_ref, wd_ref, bd_ref, y_ref, ff, nchunk):
    tm = xs_ref.shape[0] // nchunk
    x = jnp.concatenate([xs_ref[pl.ds(c, tm, stride=nchunk), :] for c in range(nchunk)],
                        axis=1).astype(bf16)
    gu = _dot(x, wgu_ref[0]) + bgu_ref[0]
    g = jnp.minimum(gu[:, :ff], SWIGLU_LIMIT)
    u = jnp.clip(gu[:, ff:], -SWIGLU_LIMIT, SWIGLU_LIMIT)
    act = (u + 1.0) * (g * (1.0 / (1.0 + jnp.exp(-SWIGLU_ALPHA * g))))
    y = _dot(act.astype(bf16), wd_ref[0]) + bd_ref[0]
    for c in range(nchunk):
        y_ref[pl.ds(c, tm, stride=nchunk), :] = y[:, c * LANES:(c + 1) * LANES]


def _moe_stage_kernel(*refs, has_main, side, nside, t_tok, tm, nchunk, ff):
    it = iter(refs)
    take = lambda k: [next(it) for _ in range(k)]
    j = pl.program_id(0)
    if has_main:
        _, used_ref = take(2)
    if side == "dispatch":
        (zt_ref,) = take(1)
    if has_main:
        main_in = take(5)
    if side == "dispatch":
        side_in = take(2)
    elif side == "combine":
        side_in = take(6)
    if has_main:
        (y_ref,) = take(1)
    (side_out,) = take(1)
    scratch = list(it)

    def side_step(fused=None):
        if side == "dispatch":
            _dispatch_step(j, nside, zt_ref, *side_in, side_out, *scratch, t_tok, tm, nchunk,
                           fused=fused)
        else:
            _combine_step(j, nside, *side_in, side_out, *scratch, t_tok, nchunk, fused=fused)

    if not has_main:
        side_step()
        return

    tile = functools.partial(_expert_tile, *main_in, y_ref, ff, nchunk)
    used = used_ref[0]
    last_fused = nside if side == "dispatch" else nside - 1
    both = (j < last_fused) & (j < used)
    side_due = j < nside

    @pl.when(both)
    def _():
        side_step(fused=tile)

    @pl.when(side_due & jnp.logical_not(both))
    def _():
        side_step()

    @pl.when((j < used) & jnp.logical_not(both))
    def _():
        tile()

    @pl.when(j >= used)
    def _():
        y_ref[...] = jnp.zeros(y_ref.shape, f32)


def _moe_stage(main, side, kind, *, seq, tm, t_tok, nchunk):
    has_main = main is not None
    nside = side["pos3"].shape[0]
    prefetch, in_specs, operands, out_specs, out_shape, scratch = [], [], [], [], [], []
    smem = pltpu.SMEM
    off = side["off"]
    clamp = lambda j: jnp.minimum(j, nside - 1)
    ff2 = d = None
    if has_main:
        _, d, ff2 = main["wgu"].shape
        eoff = main["eoff"]
        prefetch += [main["te"], main["used"]]
    if kind == "dispatch":
        prefetch += [side["zt"]]
    npre = len(prefetch)

    def spec(shape, fn, **kw):
        return pl.BlockSpec(shape, lambda j, *pre: fn(j, pre), **kw)

    if has_main:
        in_specs += [spec((tm * nchunk, LANES), lambda j, p: (jnp.minimum(j, p[1][0] - 1), 0)),
                     spec((1, d, ff2), lambda j, p: (p[0][j] + eoff, 0, 0)),
                     spec((1, 1, ff2), lambda j, p: (p[0][j], 0, 0)),
                     spec((1, ff2 // 2, d), lambda j, p: (p[0][j] + eoff, 0, 0)),
                     spec((1, 1, d), lambda j, p: (p[0][j], 0, 0))]
        operands += [main["xs"], main["wgu"], main["bgu"], main["wd"], main["bd"]]
    tokens = t_tok * TOP_K
    if kind == "dispatch":
        in_specs += [spec((1, 1, tokens), lambda j, p: (clamp(j), 0, 0), memory_space=smem),
                     spec((t_tok * nchunk, LANES), lambda j, p: (clamp(j) + off, 0))]
        operands += [side["pos3"], side["hflat"]]
    else:
        n, dd = side["x2"].shape
        per = seq // t_tok
        in_specs += [spec((1, 1, tokens), lambda j, p: (clamp(j), 0, 0), memory_space=smem),
                     spec((1, 1, tokens), lambda j, p: (clamp(j + 1), 0, 0), memory_space=smem),
                     pl.BlockSpec(memory_space=pl.ANY),
                     spec((t_tok, LANES), lambda j, p: (clamp(j) + off, 0)),
                     spec((t_tok, dd), lambda j, p: (clamp(j) + off, 0)),
                     spec((1, 1, dd), lambda j, p: ((clamp(j) + off) // per, 0, 0))]
        operands += [side["pos3"], side["pos3"], side["ys"], side["gate"], side["x2"], side["g2"]]
        x_operand = npre + len(operands) - 2
    if has_main:
        out_specs += [spec((tm * nchunk, LANES), lambda j, p: (j, 0))]
        out_shape += [jax.ShapeDtypeStruct(main["xs"].shape, f32)]
    aliases = {}
    if kind == "dispatch":
        out_specs += [pl.BlockSpec(memory_space=pl.ANY)]
        out_shape += [jax.ShapeDtypeStruct((side["rows_pad"] * nchunk, LANES), f32)]
        scratch += [pltpu.VMEM((tm * nchunk, LANES), f32), pltpu.VMEM((2, t_tok * nchunk, LANES), f32),
                    pltpu.SemaphoreType.DMA((2,)), pltpu.SemaphoreType.DMA]
    else:
        out_specs += [spec((t_tok, dd), lambda j, p: (clamp(j) + off, 0))]
        out_shape += [jax.ShapeDtypeStruct((n, dd), f32)]
        scratch += [pltpu.VMEM((2, TOP_K * t_tok * nchunk, LANES), f32),
                    pltpu.SemaphoreType.DMA((2,))]
        aliases = {x_operand: len(out_shape) - 1}
    grid = (main["te"].shape[0],) if has_main else (nside,)
    outs = pl.pallas_call(
        functools.partial(_moe_stage_kernel, has_main=has_main, side=kind, nside=nside,
                          t_tok=t_tok, tm=tm, nchunk=nchunk, ff=None if ff2 is None else ff2 // 2),
        out_shape=tuple(out_shape),
        grid_spec=pltpu.PrefetchScalarGridSpec(
            num_scalar_prefetch=npre, grid=grid, in_specs=in_specs, out_specs=tuple(out_specs),
            scratch_shapes=scratch),
        input_output_aliases=aliases,
        compiler_params=_cparams(("arbitrary",)),
        name=("moe_experts_" if has_main else "moe_") + kind,
    )(*prefetch, *operands)
    return outs


def _moe_plan(idx, rank, cnt, ne, tm, ntile_max):
    counts = cnt[:ne].astype(jnp.int32)
    ntile_e = (counts + tm - 1) // tm
    tile_end = jnp.cumsum(ntile_e)
    tile_start = tile_end - ntile_e
    used = tile_end[-1]
    e_idx = idx[:, :TOP_K].astype(jnp.int32)
    onehot = e_idx[:, :, None] == jnp.arange(ne, dtype=jnp.int32)[None, None, :]
    start = jnp.sum(jnp.where(onehot, (tile_start * tm)[None, None, :], 0), axis=-1)
    pos = start + rank[:, :TOP_K].astype(jnp.int32)
    tiles = jnp.minimum(jnp.arange(ntile_max, dtype=jnp.int32), used - 1)
    tile_expert = jnp.sum(tile_end[None, :] <= tiles[:, None], axis=1).astype(jnp.int32)
    ztile = jnp.where(ntile_e > 0, tile_end - 1, -1)
    ztile = jnp.concatenate([ztile, used.reshape(1)]).astype(jnp.int32)
    return pos, tile_expert, used.reshape(1).astype(jnp.int32), ztile


def _lane_tile(v):
    return jnp.tile(v.astype(f32), LANES // v.shape[0]).reshape(1, LANES)


def kernel(x, c, positions, ada_w, ada_b, norm1_g, norm2_g, sb_w_qkv, sb_w_o, dsa_w_in,
           dsa_q_gain, dsa_k_gain, dsa_w_o, router_w, router_b, exp_w_gu, exp_b_gu,
           exp_w_down, exp_b_down):
    batch, seq, d = x.shape
    depth = ada_w.shape[0]
    n = batch * seq
    ne = router_w.shape[-1]
    ff = exp_w_down.shape[2]
    topk = min(TOPK_MAX, seq // 4)
    assert seq % KB == 0 and d % LANES == 0 and ne <= LANES

    mod = _modulation(c, ada_w, ada_b)
    x2 = x.reshape(n, d)

    ncols = dsa_w_in.shape[-1]
    ncols_pad = -(-ncols // LANES) * LANES
    half = ROPE_DIMS // 2
    inv = jnp.exp(-math.log(ROPE_THETA) * (2.0 * jnp.arange(half, dtype=f32) / ROPE_DIMS))
    l64 = jnp.arange(LANES) % HEAD_DIM
    inv_lane = jnp.where(l64 < ROPE_DIMS, inv[l64 % half], 0.0).reshape(1, LANES)
    bd = (jnp.arange(LANES)[:, None] // HEAD_DIM == jnp.arange(LANES)[None, :] // HEAD_DIM)
    bd = bd.astype(bf16)
    pos = positions.reshape(n, 1)

    nchunk = d // LANES
    half_tok = n // 2
    nside = half_tok // MOE_TOK
    ntile_half = (half_tok * TOP_K) // MOE_TM + ne
    assert (half_tok * TOP_K) % MOE_TM == 0 and half_tok % MOE_TOK == 0 and seq % MOE_TOK == 0
    assert half_tok % POST_TM == 0
    tri = (jnp.arange(POST_TM)[:, None] > jnp.arange(POST_TM)[None, :]).astype(bf16)
    wgu_all, wd_all = _expert_weight_prep(exp_w_gu.reshape(depth * ne, d, 2 * ff),
                                          exp_w_down.reshape(depth * ne, ff, d))

    for layer in range(depth):
        m6 = [mod[layer][:, k * d:(k + 1) * d].reshape(batch, 1, d) for k in range(6)]
        sh1, sc1, g1, sh2, sc2, g2 = m6
        n1 = norm1_g[layer].reshape(1, d)
        n2 = norm2_g[layer].reshape(1, d)
        j = layer // 2
        if layer % 2 == 0:
            qkv = _proj(x2, n1, sc1, sh1, sb_w_qkv[j].astype(bf16), seq)
            o = _stick_breaking(qkv, batch, seq, d)
            wo = sb_w_o[j].astype(bf16)
        else:
            w = dsa_w_in[j]
            kv0 = d
            qi0 = d + 2 * LANES
            ki0 = qi0 + IDX_HEADS * IDX_DIM
            w = jnp.concatenate([w[:, :kv0], w[:, qi0:ki0], w[:, kv0:qi0], w[:, ki0:]], axis=1)
            w = jnp.pad(w, ((0, 0), (0, ncols_pad - ncols))).astype(bf16)
            proj, wi = _dsa_proj(x2, n1, sc1, sh1, w, pos, inv_lane, _lane_tile(dsa_q_gain[j]),
                                 _lane_tile(dsa_k_gain[j]), bd, seq)
            o = _dsa_attention(proj, wi, batch, seq, d, topk)
            wo = dsa_w_o[j].astype(bf16)
        rw = jnp.pad(router_w[layer], ((0, 0), (0, LANES - ne)))
        rw_hi = rw.astype(bf16)
        rw = jnp.stack([rw_hi, (rw - rw_hi.astype(f32)).astype(bf16)])
        rb = jnp.pad(router_b[layer], (0, LANES - ne), constant_values=-jnp.inf).reshape(1, LANES)
        x2, hflat, idx, gate, rank, cnt = _post_attention(o, wo, x2, g1, n2, sc2, sh2, rw, rb,
                                                          tri, seq, POST_TM)
        bgu = exp_b_gu[layer]
        bgu = jnp.concatenate([bgu[..., 0::2], bgu[..., 1::2]], axis=-1).reshape(ne, 1, 2 * ff)
        halves = []
        for hh in range(2):
            tok = slice(hh * half_tok, (hh + 1) * half_tok)
            spos, te, used, zt = _moe_plan(idx[tok], rank[tok], cnt[hh, 0], ne, MOE_TM, ntile_half)
            halves.append(dict(
                pos3=spos.reshape(nside, 1, MOE_TOK * TOP_K), zt=zt, off=hh * nside,
                experts=dict(te=te, used=used, wgu=wgu_all, bgu=bgu, wd=wd_all,
                             bd=exp_b_down[layer].reshape(ne, 1, d), eoff=layer * ne)))
        stage = functools.partial(_moe_stage, seq=seq, tm=MOE_TM, t_tok=MOE_TOK, nchunk=nchunk)

        def disp(h):
            return dict(zt=h["zt"], pos3=h["pos3"], hflat=hflat, off=h["off"],
                        rows_pad=ntile_half * MOE_TM)

        def comb(h, ys, xin):
            return dict(pos3=h["pos3"], ys=ys, gate=gate, x2=xin, g2=g2, off=h["off"])

        ha, hb = halves
        (xs_a,) = stage(None, disp(ha), "dispatch")
        ys_a, xs_b = stage(dict(ha["experts"], xs=xs_a), disp(hb), "dispatch")
        ys_b, x2 = stage(dict(hb["experts"], xs=xs_b), comb(ha, ys_a, x2), "combine")
        (x2,) = stage(None, comb(hb, ys_b, x2), "combine")
    return x2.reshape(batch, seq, d)
```
